```python
import math
import numpy as np
import jax
import jax.numpy as jnp
from jax import lax

D_MODEL = 1024
BATCH = 8
SEQ = 2048
DEPTH = 4
DEC_BATCH = 128
DEC_SEQ = 1
PAST_LEN = 16384
PAGE_SIZE = 128

N_HYB = (DEPTH + 1) // 2
N_SSM = DEPTH // 2
RET_HEADS = 4
RET_DK = D_MODEL // 8
RET_DV = D_MODEL // 8
GDN_HEADS = 4
GDN_DK = D_MODEL // 8
GDN_DV = D_MODEL // 8
RET_WK = RET_HEADS * RET_DK
RET_WV = RET_HEADS * RET_DV
GDN_WK = GDN_HEADS * GDN_DK
GDN_WV = GDN_HEADS * GDN_DV
GDN_CONV_CH = 2 * GDN_WK + GDN_WV
HYB_SIZES = (RET_WK, RET_WK, RET_WV, RET_WV, GDN_CONV_CH, GDN_WV, GDN_HEADS, GDN_HEADS)
HYB_PROJ = 2 * RET_WK + 2 * RET_WV + GDN_CONV_CH + GDN_WV + 2 * GDN_HEADS
HYB_OUT = RET_WV + GDN_WV
SSM_EXPAND = 2
SSM_DINNER = SSM_EXPAND * D_MODEL
SSM_HEADDIM = 64
SSM_HEADS = SSM_DINNER // SSM_HEADDIM
SSM_GROUPS = 4
SSM_RPG = SSM_HEADS // SSM_GROUPS
SSM_DSTATE = 128
SSM_GN = SSM_GROUPS * SSM_DSTATE
SSM_CONV_CH = SSM_DINNER + 2 * SSM_GN
SSM_PROJ = 2 * SSM_DINNER + 2 * SSM_GN + SSM_HEADS
CONV_W = 4
D_FF = ((8 * D_MODEL + 3 * 256 - 1) // (3 * 256)) * 256
CHUNK = 64
ROPE_BASE = 10000.0
EPS = 1e-6

kernel_name = 'hybrid_retnet_gdn_mamba2_step'


def _split(x, sizes):
    return jnp.split(x, np.cumsum(sizes)[:-1].tolist(), axis=-1)


def rmsnorm(x, w):
    xf = x.astype(jnp.float32)
    y = xf * lax.rsqrt(jnp.mean(xf * xf, axis=-1, keepdims=True) + EPS)
    return (y * w.astype(jnp.float32)).astype(x.dtype)


def l2norm(x):
    return x * lax.rsqrt(jnp.sum(x * x, axis=-1, keepdims=True) + EPS)


def rotary(x, pos):
    d = x.shape[-1]
    inv_freq = 1.0 / (ROPE_BASE ** jnp.linspace(0.0, 1.0, d // 2, dtype=jnp.float32))
    ang = pos.astype(jnp.float32)[:, None] * inv_freq[None, :]
    cos = jnp.cos(ang)[None, :, None, :]
    sin = jnp.sin(ang)[None, :, None, :]
    xp = x.astype(jnp.float32).reshape(x.shape[:-1] + (d // 2, 2))
    x0, x1 = xp[..., 0], xp[..., 1]
    return jnp.stack([x0 * cos - x1 * sin, x1 * cos + x0 * sin], axis=-1).reshape(x.shape)


def causal_conv(u, buf, w, b):
    L = u.shape[1]
    xx = jnp.concatenate([buf.astype(u.dtype), u], axis=1)
    wf = w.astype(jnp.float32)
    out = xx[:, 0:L].astype(jnp.float32) * wf[:, 0]
    for i in range(1, CONV_W):
        out = out + xx[:, i:i + L].astype(jnp.float32) * wf[:, i]
    if b is not None:
        out = out + b.astype(jnp.float32)
    return out, xx[:, L:]


def swiglu(h, wg, wu, wd):
    return (jax.nn.silu(h @ wg) * (h @ wu)) @ wd


def retention_chunked(q, k, v, log_gamma, s0):
    Bsz, H, L, _ = q.shape
    C = min(CHUNK, L)
    n = L // C
    to_chunks = lambda t: jnp.moveaxis(t.reshape(Bsz, H, n, C, t.shape[-1]), 2, 0)
    idx = jnp.arange(C, dtype=jnp.float32)
    diff = idx[:, None] - idx[None, :]
    intra = jnp.where(diff >= 0, jnp.exp(log_gamma[:, None, None] * jnp.maximum(diff, 0.0)), 0.0)
    q_dec = jnp.exp(log_gamma[:, None] * (idx + 1.0))[:, :, None]
    k_dec = jnp.exp(log_gamma[:, None] * (C - 1.0 - idx))[:, :, None]
    chunk_dec = jnp.exp(log_gamma * C)[:, None, None]

    def step(S, inp):
        qc, kc, vc = inp
        scores = jnp.einsum('bhik,bhjk->bhij', qc, kc) * intra
        o = jnp.einsum('bhij,bhjv->bhiv', scores, vc) + jnp.einsum('bhik,bhkv->bhiv', qc, S) * q_dec
        S = S * chunk_dec + jnp.einsum('bhjk,bhjv->bhkv', kc * k_dec, vc)
        return S, o

    S, o = lax.scan(step, s0, (to_chunks(q), to_chunks(k), to_chunks(v)))
    return jnp.moveaxis(o, 0, 2).reshape(Bsz, H, L, -1), S


def retention_recurrent(q, k, v, log_gamma, s0):
    gamma = jnp.exp(log_gamma)[:, None, None]

    def step(S, inp):
        qt, kt, vt = inp
        S = S * gamma + jnp.einsum('bhk,bhv->bhkv', kt, vt)
        return S, jnp.einsum('bhk,bhkv->bhv', qt, S)

    S, o = lax.scan(step, s0, (jnp.moveaxis(q, 2, 0), jnp.moveaxis(k, 2, 0), jnp.moveaxis(v, 2, 0)))
    return jnp.moveaxis(o, 0, 2), S


def gdn_chunked(q, k, v, g, beta, s0):
    Bsz, H, L, _ = q.shape
    DV = v.shape[-1]
    C = min(CHUNK, L)
    n = L // C
    rs = lambda t: t.reshape((Bsz, H, n, C) + t.shape[3:])
    q, k, v, g, beta = rs(q), rs(k), rs(v), rs(g), rs(beta)
    gc = jnp.cumsum(g, axis=-1)
    ii = jnp.arange(C)
    causal = ii[:, None] >= ii[None, :]
    strict = ii[:, None] > ii[None, :]
    diff = gc[..., :, None] - gc[..., None, :]
    decay = jnp.where(causal, jnp.exp(jnp.where(causal, diff, 0.0)), 0.0)
    kb = k * beta[..., None]
    vb = v * beta[..., None]
    lower = jnp.where(strict, jnp.einsum('bhnik,bhnjk->bhnij', kb, k) * decay, 0.0)
    a_mat = lower + jnp.eye(C, dtype=jnp.float32)
    rhs = jnp.concatenate([vb, kb * jnp.exp(gc)[..., None]], axis=-1)
    sol = lax.linalg.triangular_solve(a_mat, rhs, left_side=True, lower=True)
    u, w = sol[..., :DV], sol[..., DV:]
    mv = lambda t: jnp.moveaxis(t, 2, 0)

    def step(S, inp):
        qc, kc, uc, wc, gcc, dc = inp
        v_new = uc - jnp.einsum('bhik,bhkv->bhiv', wc, S)
        scores = jnp.einsum('bhik,bhjk->bhij', qc, kc) * dc
        o = (jnp.einsum('bhik,bhkv->bhiv', qc * jnp.exp(gcc)[..., None], S)
             + jnp.einsum('bhij,bhjv->bhiv', scores, v_new))
        g_last = gcc[..., -1:]
        S = (S * jnp.exp(g_last)[..., None]
             + jnp.einsum('bhjk,bhjv->bhkv', kc * jnp.exp(g_last - gcc)[..., None], v_new))
        return S, o

    S, o = lax.scan(step, s0, (mv(q), mv(k), mv(u), mv(w), mv(gc), mv(decay)))
    return jnp.moveaxis(o, 0, 2).reshape(Bsz, H, L, DV), S


def gdn_recurrent(q, k, v, g, beta, s0):
    def step(S, inp):
        qt, kt, vt, gt, bt = inp
        S = S * jnp.exp(gt)[..., None, None]
        delta = (vt - jnp.einsum('bhkv,bhk->bhv', S, kt)) * bt[..., None]
        S = S + jnp.einsum('bhk,bhv->bhkv', kt, delta)
        return S, jnp.einsum('bhkv,bhk->bhv', S, qt)

    xs = tuple(jnp.moveaxis(t, 2, 0) for t in (q, k, v, g, beta))
    S, o = lax.scan(step, s0, xs)
    return jnp.moveaxis(o, 0, 2), S


def ssd_chunked(xdt, a, bm, cm, s0):
    Bsz, L = xdt.shape[:2]
    C = min(CHUNK, L)
    n = L // C
    ch = lambda t: jnp.moveaxis(t.reshape((Bsz, n, C) + t.shape[2:]), 1, 0)
    ii = jnp.arange(C)
    causal = (ii[:, None] >= ii[None, :])[None, :, :, None, None]

    def step(S, inp):
        xc, ac, bc, cc = inp
        acs = jnp.cumsum(ac, axis=1)
        seg = acs[:, :, None] - acs[:, None, :]
        lmat = jnp.where(causal, jnp.exp(jnp.where(causal, seg, 0.0)), 0.0)
        cb = jnp.einsum('bign,bjgn->bijg', cc, bc)
        y = jnp.einsum('bijg,bijgr,bjgrp->bigrp', cb, lmat, xc)
        y = y + jnp.einsum('bign,bgrpn->bigrp', cc, S) * jnp.exp(acs)[..., None]
        dec = jnp.exp(acs[:, -1:] - acs)
        S = (S * jnp.exp(acs[:, -1])[..., None, None]
             + jnp.einsum('bjgn,bjgr,bjgrp->bgrpn', bc, dec, xc))
        return S, y

    S, y = lax.scan(step, s0, (ch(xdt), ch(a), ch(bm), ch(cm)))
    return jnp.moveaxis(y, 0, 1).reshape(xdt.shape), S


def ssd_recurrent(xdt, a, bm, cm, s0):
    def step(S, inp):
        xt, at, bt, ct = inp
        S = S * jnp.exp(at)[..., None, None] + jnp.einsum('bgrp,bgn->bgrpn', xt, bt)
        return S, jnp.einsum('bgrpn,bgn->bgrp', S, ct)

    xs = tuple(jnp.moveaxis(t, 1, 0) for t in (xdt, a, bm, cm))
    S, y = lax.scan(step, s0, xs)
    return jnp.moveaxis(y, 0, 1), S


def hybrid_mixer(h, pos0, ret_s0, gdn_s0, gdn_buf, w_in, w_out, gn_w, conv_w, a_log, dt_bias, norm_w, chunked):
    Bsz, L, _ = h.shape
    rq, rk, rv, rg, qkv, gz, ga, gb = _split(h @ w_in, HYB_SIZES)
    sw = lambda t: jnp.swapaxes(t, 1, 2)
    pos = pos0 + jnp.arange(L)
    rq = sw(rotary(rq.reshape(Bsz, L, RET_HEADS, RET_DK), pos))
    rk = sw(rotary(rk.reshape(Bsz, L, RET_HEADS, RET_DK), pos) * RET_DK ** -0.5)
    rv = sw(rv.astype(jnp.float32).reshape(Bsz, L, RET_HEADS, RET_DV))
    log_gamma = jnp.log1p(-jnp.exp2(-5.0 - jnp.arange(RET_HEADS, dtype=jnp.float32)))
    ret_fn = retention_chunked if chunked else retention_recurrent
    ro, ret_s = ret_fn(rq, rk, rv, log_gamma, ret_s0.astype(jnp.float32))
    ro = sw(ro)
    mu = jnp.mean(ro, axis=-1, keepdims=True)
    var = jnp.mean(jnp.square(ro - mu), axis=-1, keepdims=True)
    ro = (ro - mu) * lax.rsqrt(var + EPS) * gn_w.astype(jnp.float32).reshape(RET_HEADS, RET_DV)
    ret_out = jax.nn.silu(rg.astype(jnp.float32)) * ro.reshape(Bsz, L, RET_WV)
    qkv, new_buf = causal_conv(qkv, gdn_buf, conv_w, None)
    gq, gk, gv = _split(jax.nn.silu(qkv), (GDN_WK, GDN_WK, GDN_WV))
    gq = sw(l2norm(gq.reshape(Bsz, L, GDN_HEADS, GDN_DK)) * GDN_DK ** -0.5)
    gk = sw(l2norm(gk.reshape(Bsz, L, GDN_HEADS, GDN_DK)))
    gv = sw(gv.reshape(Bsz, L, GDN_HEADS, GDN_DV))
    g = -jnp.exp(a_log.astype(jnp.float32)) * jax.nn.softplus(ga.astype(jnp.float32) + dt_bias.astype(jnp.float32))
    beta = jax.nn.sigmoid(gb.astype(jnp.float32))
    gdn_fn = gdn_chunked if chunked else gdn_recurrent
    go, gdn_s = gdn_fn(gq, gk, gv, sw(g), sw(beta), gdn_s0.astype(jnp.float32))
    go = sw(go)
    go = go * lax.rsqrt(jnp.mean(go * go, axis=-1, keepdims=True) + EPS) * norm_w.astype(jnp.float32)
    go = go * jax.nn.silu(gz.astype(jnp.float32).reshape(Bsz, L, GDN_HEADS, GDN_DV))
    mixed = jnp.concatenate([ret_out, go.reshape(Bsz, L, GDN_WV)], axis=-1).astype(h.dtype)
    return mixed @ w_out, ret_s, gdn_s, new_buf


def ssm_mixer(h, s0, buf, w_in, conv_w, conv_b, dt_bias, a_log, d_skip, norm_w, w_out, chunked):
    Bsz, L, _ = h.shape
    z, xbc, dt_raw = _split(h @ w_in, (SSM_DINNER, SSM_CONV_CH, SSM_HEADS))
    xbc, new_buf = causal_conv(xbc, buf, conv_w, conv_b)
    xs, bm, cm = _split(jax.nn.silu(xbc), (SSM_DINNER, SSM_GN, SSM_GN))
    xs = xs.reshape(Bsz, L, SSM_GROUPS, SSM_RPG, SSM_HEADDIM)
    bm = bm.reshape(Bsz, L, SSM_GROUPS, SSM_DSTATE)
    cm = cm.reshape(Bsz, L, SSM_GROUPS, SSM_DSTATE)
    dt = jax.nn.softplus(dt_raw.astype(jnp.float32) + dt_bias.astype(jnp.float32))
    dt = dt.reshape(Bsz, L, SSM_GROUPS, SSM_RPG)
    a = dt * (-jnp.exp(a_log.astype(jnp.float32))).reshape(SSM_GROUPS, SSM_RPG)
    s0 = s0.astype(jnp.float32).reshape(Bsz, SSM_GROUPS, SSM_RPG, SSM_HEADDIM, SSM_DSTATE)
    ssd_fn = ssd_chunked if chunked else ssd_recurrent
    y, S = ssd_fn(xs * dt[..., None], a, bm, cm, s0)
    y = y + xs * d_skip.astype(jnp.float32).reshape(SSM_GROUPS, SSM_RPG)[..., None]
    yg = (y.reshape(Bsz, L, SSM_DINNER) * jax.nn.silu(z.astype(jnp.float32)))
    yg = yg.reshape(Bsz, L, SSM_GROUPS, SSM_DINNER // SSM_GROUPS)
    yg = yg * lax.rsqrt(jnp.mean(yg * yg, axis=-1, keepdims=True) + EPS)
    yg = yg.reshape(Bsz, L, SSM_DINNER) * norm_w.astype(jnp.float32)
    out = yg.astype(h.dtype) @ w_out
    return out, S.reshape(Bsz, SSM_HEADS, SSM_HEADDIM, SSM_DSTATE), new_buf


def setup_inputs(seed: int = 0) -> dict:
    key = jax.random.key(seed)
    ks = jax.random.split(key, 28)
    f32 = jnp.float32
    nrm = lambda k, shape, s: jax.random.normal(k, shape, f32) * s

    def dt_bias_init(k, shape):
        dt = jnp.exp(jax.random.uniform(k, shape, f32) * (math.log(0.1) - math.log(0.001)) + math.log(0.001))
        return dt + jnp.log(-jnp.expm1(-dt))

    def a_log_init(k, shape):
        return jnp.log(jax.random.uniform(k, shape, f32, 1.0, 16.0))

    return {
        'x_prompt': nrm(ks[0], (BATCH, SEQ, D_MODEL), 1.0),
        'x_sample': nrm(ks[1], (DEC_BATCH, DEC_SEQ, D_MODEL), 1.0),
        'state_ret': nrm(ks[2], (N_HYB, DEC_BATCH, RET_HEADS, RET_DK, RET_DV), 0.5),
        'state_gdn': nrm(ks[3], (N_HYB, DEC_BATCH, GDN_HEADS, GDN_DK, GDN_DV), 0.3),
        'state_gdn_conv': nrm(ks[4], (N_HYB, DEC_BATCH, CONV_W - 1, GDN_CONV_CH), 1.0),
        'state_ssm': nrm(ks[5], (N_SSM, DEC_BATCH, SSM_HEADS, SSM_HEADDIM, SSM_DSTATE), 0.3),
        'state_ssm_conv': nrm(ks[6], (N_SSM, DEC_BATCH, CONV_W - 1, SSM_CONV_CH), 1.0),
        'norm_mix_w': 1.0 + nrm(ks[7], (DEPTH, D_MODEL), 0.02),
        'norm_ffn_w': 1.0 + nrm(ks[8], (DEPTH, D_MODEL), 0.02),
        'norm_final_w': 1.0 + nrm(ks[9], (D_MODEL,), 0.02),
        'hyb_w_in': nrm(ks[10], (N_HYB, D_MODEL, HYB_PROJ), D_MODEL ** -0.5),
        'hyb_w_out': nrm(ks[11], (N_HYB, HYB_OUT, D_MODEL), HYB_OUT ** -0.5),
        'ret_gn_w': 1.0 + nrm(ks[12], (N_HYB, RET_WV), 0.02),
        'gdn_conv_w': nrm(ks[13], (N_HYB, GDN_CONV_CH, CONV_W), CONV_W ** -0.5),
        'gdn_a_log': a_log_init(ks[14], (N_HYB, GDN_HEADS)),
        'gdn_dt_bias': dt_bias_init(ks[15], (N_HYB, GDN_HEADS)),
        'gdn_norm_w': 1.0 + nrm(ks[16], (N_HYB, GDN_DV), 0.02),
        'ssm_w_in': nrm(ks[17], (N_SSM, D_MODEL, SSM_PROJ), D_MODEL ** -0.5),
        'ssm_conv_w': nrm(ks[18], (N_SSM, SSM_CONV_CH, CONV_W), CONV_W ** -0.5),
        'ssm_conv_b': nrm(ks[19], (N_SSM, SSM_CONV_CH), 0.01),
        'ssm_dt_bias': dt_bias_init(ks[20], (N_SSM, SSM_HEADS)),
        'ssm_a_log': a_log_init(ks[21], (N_SSM, SSM_HEADS)),
        'ssm_d': 1.0 + nrm(ks[22], (N_SSM, SSM_HEADS), 0.1),
        'ssm_norm_w': 1.0 + nrm(ks[23], (N_SSM, SSM_DINNER), 0.02),
        'ssm_w_out': nrm(ks[24], (N_SSM, SSM_DINNER, D_MODEL), SSM_DINNER ** -0.5),
        'ffn_w_gate': nrm(ks[25], (DEPTH, D_MODEL, D_FF), D_MODEL ** -0.5),
        'ffn_w_up': nrm(ks[26], (DEPTH, D_MODEL, D_FF), D_MODEL ** -0.5),
        'ffn_w_down': nrm(ks[27], (DEPTH, D_FF, D_MODEL), D_FF ** -0.5),
    }


def reference(x_prompt, x_sample, state_ret, state_gdn, state_gdn_conv, state_ssm, state_ssm_conv,
              norm_mix_w, norm_ffn_w, norm_final_w, hyb_w_in, hyb_w_out, ret_gn_w, gdn_conv_w,
              gdn_a_log, gdn_dt_bias, gdn_norm_w, ssm_w_in, ssm_conv_w, ssm_conv_b, ssm_dt_bias,
              ssm_a_log, ssm_d, ssm_norm_w, ssm_w_out, ffn_w_gate, ffn_w_up, ffn_w_down):

    def trunk(x, pos0, ret_s, gdn_s, gdn_buf, ssm_s, ssm_buf, chunked):
        new_ret, new_gdn, new_gbuf, new_ssm, new_sbuf = [], [], [], [], []
        for i in range(DEPTH):
            j = i // 2
            h = rmsnorm(x, norm_mix_w[i])
            if i % 2 == 0:
                out, rs, gs, gbf = hybrid_mixer(h, pos0, ret_s[j], gdn_s[j], gdn_buf[j], hyb_w_in[j], hyb_w_out[j],
                                                ret_gn_w[j], gdn_conv_w[j], gdn_a_log[j], gdn_dt_bias[j],
                                                gdn_norm_w[j], chunked)
                new_ret.append(rs)
                new_gdn.append(gs)
                new_gbuf.append(gbf)
            else:
                out, ss, sbf = ssm_mixer(h, ssm_s[j], ssm_buf[j], ssm_w_in[j], ssm_conv_w[j], ssm_conv_b[j],
                                         ssm_dt_bias[j], ssm_a_log[j], ssm_d[j], ssm_norm_w[j], ssm_w_out[j],
                                         chunked)
                new_ssm.append(ss)
                new_sbuf.append(sbf)
            x = x + out.astype(x.dtype)
            x = x + swiglu(rmsnorm(x, norm_ffn_w[i]), ffn_w_gate[i], ffn_w_up[i], ffn_w_down[i]).astype(x.dtype)
        y = rmsnorm(x, norm_final_w)
        return y, jnp.stack(new_ret), jnp.stack(new_gdn), jnp.stack(new_gbuf), jnp.stack(new_ssm), jnp.stack(new_sbuf)

    bp = x_prompt.shape[0]
    dp = x_prompt.dtype
    y_prompt, ret_p, gdn_p, gconv_p, ssm_p, sconv_p = trunk(
        x_prompt, 0,
        jnp.zeros((N_HYB, bp, RET_HEADS, RET_DK, RET_DV), jnp.float32),
        jnp.zeros((N_HYB, bp, GDN_HEADS, GDN_DK, GDN_DV), jnp.float32),
        jnp.zeros((N_HYB, bp, CONV_W - 1, GDN_CONV_CH), dp),
        jnp.zeros((N_SSM, bp, SSM_HEADS, SSM_HEADDIM, SSM_DSTATE), jnp.float32),
        jnp.zeros((N_SSM, bp, CONV_W - 1, SSM_CONV_CH), dp),
        True)
    y_sample, ret_s, gdn_s, gconv_s, ssm_s, sconv_s = trunk(
        x_sample, PAST_LEN, state_ret, state_gdn, state_gdn_conv, state_ssm, state_ssm_conv, False)
    return (y_prompt, y_sample,
            ret_p.astype(dp), gdn_p.astype(dp), gconv_p.astype(dp), ssm_p.astype(dp), sconv_p.astype(dp),
            ret_s.astype(state_ret.dtype), gdn_s.astype(state_gdn.dtype), gconv_s.astype(state_gdn_conv.dtype),
            ssm_s.astype(state_ssm.dtype), sconv_s.astype(state_ssm_conv.dtype))
```

```python
import functools
import math

import jax
import jax.numpy as jnp
from jax import lax
from jax.experimental import pallas as pl
from jax.experimental.pallas import tpu as pltpu

F32 = jnp.float32
BF16 = jnp.bfloat16

D_MODEL = 1024
N_HEADS = 4
HEAD_DIM = 128
HYB_MAIN = 4096
GDN_CONV_CH = 1536
SSM_DINNER = 2048
SSM_HEADS = 32
SSM_HEADDIM = 64
SSM_GROUPS = 4
SSM_DSTATE = 128
SSM_GROUP_W = SSM_DINNER // SSM_GROUPS
SSM_CONV_CH = 3072
SSM_MAIN = 5120
D_FF = 2816
FF_CHUNK = 256
CONV_W = 4
EPS = 1e-6
ROPE_BASE = 10000.0

LANES = 128
SUBLANES = 8
VMEM_LIMIT = 56 * 1024 * 1024

SEQ_TILE = 256
CHUNK = 64
STEP_BATCH = 8
LOG_GAMMA = tuple(math.log1p(-2.0 ** (-5.0 - h)) for h in range(N_HEADS))

_NT = (((1,), (1,)), ((), ()))
_TN = (((0,), (0,)), ((), ()))


def _dot(a, b):
    return jnp.dot(a, b, preferred_element_type=F32)


def _dot_nt(a, b):
    return lax.dot_general(a, b, _NT, preferred_element_type=F32)


def _dot_tn(a, b):
    return lax.dot_general(a, b, _TN, preferred_element_type=F32)


def _dot_exact(a, b):
    return jnp.dot(a, b, preferred_element_type=F32, precision=lax.Precision.HIGHEST)


def _silu(x):
    return x / (1.0 + jnp.exp(-x))


def _sigmoid(x):
    return 1.0 / (1.0 + jnp.exp(-x))


def _softplus(x):
    return jnp.maximum(x, 0.0) + jnp.log1p(jnp.exp(-jnp.abs(x)))


def _rmsnorm(x, w):
    return x * lax.rsqrt(jnp.mean(x * x, axis=-1, keepdims=True) + EPS) * w


def _resident(shape):
    nd = len(shape)
    return pl.BlockSpec(shape, lambda *_: (0,) * nd, pipeline_mode=pl.Buffered(1))


def _params(n_axes):
    return pltpu.CompilerParams(dimension_semantics=("arbitrary",) * n_axes,
                                vmem_limit_bytes=VMEM_LIMIT)


def _row_tile(t):
    return 512 if t % 512 == 0 else t


def _norm_proj_kernel(x_ref, nw_ref, w_ref, wt_ref, o_ref, ot_ref, *, n_chunk):
    h = _rmsnorm(x_ref[...], nw_ref[...]).astype(BF16)
    n = w_ref.shape[1]
    for c in range(n // n_chunk):
        sl = slice(c * n_chunk, (c + 1) * n_chunk)
        o_ref[:, sl] = _dot(h, w_ref[:, sl])
    ot_ref[...] = _dot(h, wt_ref[...])


def _norm_proj(x, nw, w_main, w_tail):
    t, d = x.shape
    n = w_main.shape[1]
    tm = _row_tile(t)
    return pl.pallas_call(
        functools.partial(_norm_proj_kernel, n_chunk=512),
        grid=(t // tm,),
        in_specs=[pl.BlockSpec((tm, d), lambda i: (i, 0)),
                  _resident((1, d)), _resident((d, n)), _resident((d, LANES))],
        out_specs=[pl.BlockSpec((tm, n), lambda i: (i, 0)),
                   pl.BlockSpec((tm, LANES), lambda i: (i, 0))],
        out_shape=[jax.ShapeDtypeStruct((t, n), F32), jax.ShapeDtypeStruct((t, LANES), F32)],
        compiler_params=_params(1),
        name="norm_proj",
    )(x, nw.reshape(1, d), w_main, w_tail)


def _out_proj_kernel(x_ref, a_ref, w_ref, o_ref):
    o_ref[...] = x_ref[...] + _dot(a_ref[...].astype(BF16), w_ref[...])


def _out_proj(x, a, w):
    t, d = x.shape
    k = a.shape[1]
    tm = _row_tile(t)
    return pl.pallas_call(
        _out_proj_kernel,
        grid=(t // tm,),
        in_specs=[pl.BlockSpec((tm, d), lambda i: (i, 0)),
                  pl.BlockSpec((tm, k), lambda i: (i, 0)),
                  _resident((k, d))],
        out_specs=pl.BlockSpec((tm, d), lambda i: (i, 0)),
        out_shape=jax.ShapeDtypeStruct((t, d), F32),
        compiler_params=_params(1),
        name="out_proj",
    )(x, a, w)


def _ffn_kernel(x_ref, nw_ref, wg_ref, wu_ref, wd_ref, fw_ref, o_ref, act_ref, *, final_norm):
    x = x_ref[...]
    h = _rmsnorm(x, nw_ref[...]).astype(BF16)
    for c in range(D_FF // FF_CHUNK):
        sl = slice(c * FF_CHUNK, (c + 1) * FF_CHUNK)
        gate = _dot(h, wg_ref[:, sl])
        up = _dot(h, wu_ref[:, sl])
        act_ref[:, sl] = (_silu(gate) * up).astype(BF16)
    y = x + _dot(act_ref[...], wd_ref[...])
    if final_norm:
        y = _rmsnorm(y, fw_ref[...])
    o_ref[...] = y


def _ffn(x, nw, wg, wu, wd, final_w, final_norm):
    t, d = x.shape
    tm = _row_tile(t)
    return pl.pallas_call(
        functools.partial(_ffn_kernel, final_norm=final_norm),
        grid=(t // tm,),
        in_specs=[pl.BlockSpec((tm, d), lambda i: (i, 0)),
                  _resident((1, d)), _resident((d, D_FF)), _resident((d, D_FF)),
                  _resident((D_FF, d)), _resident((1, d))],
        out_specs=pl.BlockSpec((tm, d), lambda i: (i, 0)),
        out_shape=jax.ShapeDtypeStruct((t, d), F32),
        scratch_shapes=[pltpu.VMEM((tm, D_FF), BF16)],
        compiler_params=_params(1),
        name="ffn",
    )(x, nw.reshape(1, d), wg, wu, wd, final_w.reshape(1, d))


def _rotary(x, cos, sin_signed):
    lane = lax.broadcasted_iota(jnp.int32, x.shape, 1)
    partner = jnp.where((lane & 1) == 0, pltpu.roll(x, LANES - 1, 1), pltpu.roll(x, 1, 1))
    return x * cos + partner * sin_signed


def _l2norm(x):
    return x * lax.rsqrt(jnp.sum(x * x, axis=-1, keepdims=True) + EPS)


def _split3(x):
    hi = x.astype(BF16)
    r1 = x - hi.astype(F32)
    mid = r1.astype(BF16)
    lo = (r1 - mid.astype(F32)).astype(BF16)
    return hi, mid, lo


def _expand_heads(x, expand_mat):
    hi, mid, lo = _split3(x)
    return _dot(hi, expand_mat) + _dot(mid, expand_mat) + _dot(lo, expand_mat)


def _head_expand_matrix():
    row = lax.broadcasted_iota(jnp.int32, (LANES, SSM_DINNER), 0)
    col = lax.broadcasted_iota(jnp.int32, (LANES, SSM_DINNER), 1)
    return jnp.where(row == (col >> 6), 1.0, 0.0).astype(BF16)


def _chunk_sum_matrices(n):
    ii = lax.broadcasted_iota(jnp.int32, (n, n), 0)
    jj = lax.broadcasted_iota(jnp.int32, (n, n), 1)
    same = (ii >> 6) == (jj >> 6)
    prefix = jnp.where(same & (ii >= jj), 1.0, 0.0).astype(F32)
    total = jnp.where(same, 1.0, 0.0).astype(F32)
    return prefix, total


def _columns(rows):
    pad = jnp.zeros((LANES - rows.shape[0], LANES), F32)
    return jnp.concatenate([rows, pad], axis=0).T


def _unit_lower_inverse(low):
    n = low.shape[0]
    ii = lax.broadcasted_iota(jnp.int32, (n, n), 0)
    jj = lax.broadcasted_iota(jnp.int32, (n, n), 1)
    t_off = -jnp.where((ii >> 1) == (jj >> 1), low, 0.0)
    for k in range(1, 6):
        join = ((ii >> (k + 1)) == (jj >> (k + 1))) & ((ii >> k) != (jj >> k))
        x = jnp.where(join, low, 0.0)
        t_b = t_off.astype(BF16)
        tx = x + _dot(t_b, x.astype(BF16))
        t_off = t_off - (tx + _dot(tx.astype(BF16), t_b))
    return t_off


def _hyb_core_kernel(pm_ref, pt_ref, cos_ref, sin_ref, cw_ref, alog_ref, dtb_ref, gnw_ref, gnn_ref,
                     mix_ref, rs_ref, gs_ref, cb_ref, sret, sgdn, xx, o_scr, *, n_tiles):
    t = pl.program_id(1)
    tl = SEQ_TILE
    hd = HEAD_DIM

    @pl.when(t == 0)
    def _():
        sret[...] = jnp.zeros_like(sret)
        sgdn[...] = jnp.zeros_like(sgdn)
        xx[0:SUBLANES, :] = jnp.zeros((SUBLANES, GDN_CONV_CH), F32)

    cos = cos_ref[...]
    sin = sin_ref[...]
    ii = lax.broadcasted_iota(jnp.int32, (tl, tl), 0)
    jj = lax.broadcasted_iota(jnp.int32, (tl, tl), 1)
    diff = (ii - jj).astype(F32)
    idx = lax.broadcasted_iota(jnp.int32, (tl, 1), 0).astype(F32)
    for h in range(N_HEADS):
        lg = LOG_GAMMA[h]
        q = _rotary(pm_ref[:, h * hd:(h + 1) * hd], cos, sin)
        k = _rotary(pm_ref[:, 512 + h * hd:512 + (h + 1) * hd], cos, sin) * (hd ** -0.5)
        v = pm_ref[:, 1024 + h * hd:1024 + (h + 1) * hd]
        gate = pm_ref[:, 1536 + h * hd:1536 + (h + 1) * hd]
        intra = jnp.where(diff >= 0.0, jnp.exp(lg * jnp.maximum(diff, 0.0)), 0.0)
        q_dec = jnp.exp(lg * (idx + 1.0))
        k_dec = jnp.exp(lg * (tl - 1.0 - idx))
        s = sret[h]
        qb = q.astype(BF16)
        vb = v.astype(BF16)
        scores = _dot_nt(qb, k.astype(BF16)) * intra
        o = _dot(scores.astype(BF16), vb) + _dot(qb, s.astype(BF16)) * q_dec
        sret[h] = s * math.exp(lg * tl) + _dot_tn((k * k_dec).astype(BF16), vb)
        mu = jnp.mean(o, axis=-1, keepdims=True)
        var = jnp.mean(jnp.square(o - mu), axis=-1, keepdims=True)
        ro = (o - mu) * lax.rsqrt(var + EPS) * gnw_ref[:, h * hd:(h + 1) * hd]
        mix_ref[:, h * hd:(h + 1) * hd] = (_silu(gate) * ro).astype(BF16)

    u = pm_ref[:, 2048:2048 + GDN_CONV_CH]
    xx[SUBLANES:SUBLANES + tl, :] = u
    conv = xx[SUBLANES - 3:SUBLANES - 3 + tl, :] * cw_ref[0:1, :]
    conv = conv + xx[SUBLANES - 2:SUBLANES - 2 + tl, :] * cw_ref[1:2, :]
    conv = conv + xx[SUBLANES - 1:SUBLANES - 1 + tl, :] * cw_ref[2:3, :]
    conv = conv + u * cw_ref[3:4, :]
    o_scr[...] = _silu(conv)
    xx[0:SUBLANES, :] = xx[tl:tl + SUBLANES, :]

    @pl.when(t == n_tiles - 1)
    def _():
        cb_ref[0] = xx[tl:tl + SUBLANES, :]

    tail = pt_ref[...]
    g_all = -jnp.exp(alog_ref[...]) * _softplus(tail + dtb_ref[...])
    beta_all = _sigmoid(tail)
    prefix, total = _chunk_sum_matrices(tl)
    gc_all = _dot_exact(prefix, g_all)
    gl_all = _dot_exact(total, g_all)

    n = N_HEADS * CHUNK
    si = lax.broadcasted_iota(jnp.int32, (n, n), 0)
    sj = lax.broadcasted_iota(jnp.int32, (n, n), 1)
    same_head = (si >> 6) == (sj >> 6)
    causal = same_head & (si >= sj)
    strict = same_head & (si > sj)

    def stack(fn):
        return jnp.concatenate([fn(h) for h in range(N_HEADS)], axis=0)

    for c in range(tl // CHUNK):
        r = slice(c * CHUNK, (c + 1) * CHUNK)
        q_st = stack(lambda h: _l2norm(o_scr[r, h * hd:(h + 1) * hd]) * (hd ** -0.5))
        k_st = stack(lambda h: _l2norm(o_scr[r, 512 + h * hd:512 + (h + 1) * hd]))
        v_st = stack(lambda h: o_scr[r, 1024 + h * hd:1024 + (h + 1) * hd])
        beta = stack(lambda h: beta_all[r, 4 + h:5 + h])
        gc = stack(lambda h: gc_all[r, h:h + 1])
        gl = stack(lambda h: gl_all[r, h:h + 1])
        gc_col = jnp.broadcast_to(gc, (n, n))
        seg = gc_col - gc_col.T
        decay = jnp.where(causal, jnp.exp(jnp.where(causal, seg, 0.0)), 0.0)
        kb = k_st * beta
        vb = v_st * beta
        k_b16 = k_st.astype(BF16)
        low = jnp.where(strict, _dot_nt(kb.astype(BF16), k_b16) * decay, 0.0)
        rhs = jnp.concatenate([vb, kb * jnp.exp(gc)], axis=1)
        t_off = _unit_lower_inverse(low)
        sol = rhs + _dot(t_off.astype(BF16), rhs.astype(BF16))
        u_st = sol[:, :hd]
        w_st = sol[:, hd:]
        scores = _dot_nt(q_st.astype(BF16), k_b16) * decay
        q_in = (q_st * jnp.exp(gc)).astype(BF16)
        k_out = (k_st * jnp.exp(gl - gc)).astype(BF16)
        w_b16 = w_st.astype(BF16)
        v_new = []
        o_inter = []
        for h in range(N_HEADS):
            hr = slice(h * CHUNK, (h + 1) * CHUNK)
            s_b16 = sgdn[h].astype(BF16)
            v_new.append(u_st[hr] - _dot(w_b16[hr], s_b16))
            o_inter.append(_dot(q_in[hr], s_b16))
        v_new = jnp.concatenate(v_new, axis=0)
        vn_b16 = v_new.astype(BF16)
        o_st = jnp.concatenate(o_inter, axis=0) + _dot(scores.astype(BF16), vn_b16)
        for h in range(N_HEADS):
            hr = slice(h * CHUNK, (h + 1) * CHUNK)
            s_dec = jnp.exp(jnp.broadcast_to(gl[hr], (CHUNK, hd)))
            s_dec = jnp.concatenate([s_dec, s_dec], axis=0)
            sgdn[h] = sgdn[h] * s_dec + _dot_tn(k_out[hr], vn_b16[hr])
            o_h = o_st[hr]
            go = o_h * lax.rsqrt(jnp.mean(o_h * o_h, axis=-1, keepdims=True) + EPS) * gnn_ref[...]
            gz = pm_ref[r, 3584 + h * hd:3584 + (h + 1) * hd]
            mix_ref[r, 512 + h * hd:512 + (h + 1) * hd] = (go * _silu(gz)).astype(BF16)

    @pl.when(t == n_tiles - 1)
    def _():
        rs_ref[0] = sret[...]
        gs_ref[0] = sgdn[...]


def _hyb_core_prompt(pm, pt, cos, sin, cw_t, alog, dtb, gnw, gnn, batch, seq):
    n_tiles = seq // SEQ_TILE
    row = lambda b, t: (b * n_tiles + t, 0)
    state = pl.BlockSpec((1, N_HEADS, HEAD_DIM, HEAD_DIM), lambda b, t: (b, 0, 0, 0))
    return pl.pallas_call(
        functools.partial(_hyb_core_kernel, n_tiles=n_tiles),
        grid=(batch, n_tiles),
        in_specs=[pl.BlockSpec((SEQ_TILE, HYB_MAIN), row),
                  pl.BlockSpec((SEQ_TILE, LANES), row),
                  pl.BlockSpec((SEQ_TILE, LANES), lambda b, t: (t, 0)),
                  pl.BlockSpec((SEQ_TILE, LANES), lambda b, t: (t, 0)),
                  _resident((CONV_W, GDN_CONV_CH)), _resident((1, LANES)), _resident((1, LANES)),
                  _resident((1, 512)), _resident((1, LANES))],
        out_specs=[pl.BlockSpec((SEQ_TILE, D_MODEL), row), state, state,
                   pl.BlockSpec((1, SUBLANES, GDN_CONV_CH), lambda b, t: (b, 0, 0))],
        out_shape=[jax.ShapeDtypeStruct((batch * seq, D_MODEL), BF16),
                   jax.ShapeDtypeStruct((batch, N_HEADS, HEAD_DIM, HEAD_DIM), F32),
                   jax.ShapeDtypeStruct((batch, N_HEADS, HEAD_DIM, HEAD_DIM), F32),
                   jax.ShapeDtypeStruct((batch, SUBLANES, GDN_CONV_CH), F32)],
        scratch_shapes=[pltpu.VMEM((N_HEADS, HEAD_DIM, HEAD_DIM), F32),
                        pltpu.VMEM((N_HEADS, HEAD_DIM, HEAD_DIM), F32),
                        pltpu.VMEM((SEQ_TILE + SUBLANES, GDN_CONV_CH), F32),
                        pltpu.VMEM((SEQ_TILE, GDN_CONV_CH), F32)],
        compiler_params=_params(2),
        name="hyb_core_prompt",
    )(pm, pt, cos, sin, cw_t, alog, dtb, gnw, gnn)


def _group_rmsnorm_gate(y, z, nw_ref, o_ref, rows):
    for g in range(SSM_GROUPS):
        sl = slice(g * SSM_GROUP_W, (g + 1) * SSM_GROUP_W)
        yg = y[:, sl] * _silu(z[:, sl])
        yg = yg * lax.rsqrt(jnp.mean(yg * yg, axis=-1, keepdims=True) + EPS)
        o_ref[rows, sl] = (yg * nw_ref[:, sl]).astype(o_ref.dtype)


def _ssm_core_kernel(pm_ref, pt_ref, cw_ref, cbias_ref, alog_ref, dtb_ref, dskip_ref, nw_ref,
                     y_ref, st_ref, cb_ref, s_t, xx, xbc, xdt, xdec, eacs, y_scr, *, n_tiles):
    t = pl.program_id(1)
    tl = SEQ_TILE

    @pl.when(t == 0)
    def _():
        s_t[...] = jnp.zeros_like(s_t)
        xx[0:SUBLANES, :] = jnp.zeros((SUBLANES, SSM_CONV_CH), F32)

    u = pm_ref[:, SSM_DINNER:SSM_MAIN]
    xx[SUBLANES:SUBLANES + tl, :] = u
    conv = xx[SUBLANES - 3:SUBLANES - 3 + tl, :] * cw_ref[0:1, :]
    conv = conv + xx[SUBLANES - 2:SUBLANES - 2 + tl, :] * cw_ref[1:2, :]
    conv = conv + xx[SUBLANES - 1:SUBLANES - 1 + tl, :] * cw_ref[2:3, :]
    conv = conv + u * cw_ref[3:4, :]
    xbc[...] = _silu(conv + cbias_ref[...])
    xx[0:SUBLANES, :] = xx[tl:tl + SUBLANES, :]

    @pl.when(t == n_tiles - 1)
    def _():
        cb_ref[0] = xx[tl:tl + SUBLANES, :]

    expand = _head_expand_matrix()
    dt = _softplus(pt_ref[...] + dtb_ref[...])
    a = dt * (-jnp.exp(alog_ref[...]))
    prefix, total = _chunk_sum_matrices(tl)
    acs = _dot_exact(prefix, a)
    atot = _dot_exact(total, a)
    xdt[...] = xbc[:, 0:SSM_DINNER] * _expand_heads(dt, expand)
    xdec[...] = xdt[...] * _expand_heads(jnp.exp(atot - acs), expand)
    eacs[...] = _expand_heads(jnp.exp(acs), expand)
    chunk_dec = _expand_heads(jnp.exp(atot), expand)

    lane = lax.broadcasted_iota(jnp.int32, (CHUNK, LANES), 1)
    row_i = lax.broadcasted_iota(jnp.int32, (CHUNK, LANES), 0)
    first_half = lane < SSM_HEADDIM
    causal2 = row_i >= (lane & (CHUNK - 1))
    lane2 = lax.broadcasted_iota(jnp.int32, (2 * CHUNK, LANES), 1)
    row2 = lax.broadcasted_iota(jnp.int32, (2 * CHUNK, LANES), 0)
    pair_diag = (row2 < CHUNK) == (lane2 < SSM_HEADDIM)

    for c in range(tl // CHUNK):
        r = slice(c * CHUNK, (c + 1) * CHUNK)
        acs_c = acs[r]
        acs_t = jnp.concatenate([acs_c, acs_c], axis=0).T
        for g in range(SSM_GROUPS):
            gs = slice(g * SSM_GROUP_W, (g + 1) * SSM_GROUP_W)
            bc = xbc[r, SSM_DINNER + g * LANES:SSM_DINNER + (g + 1) * LANES]
            cc = xbc[r, SSM_DINNER + 512 + g * LANES:SSM_DINNER + 512 + (g + 1) * LANES]
            bc_b16 = bc.astype(BF16)
            cc_b16 = cc.astype(BF16)
            cb2 = _dot_nt(cc_b16, jnp.concatenate([bc_b16, bc_b16], axis=0))
            y_inter = _dot(cc_b16, s_t[:, gs].astype(BF16))
            for p in range(4):
                ha = 8 * g + 2 * p
                ks = slice((4 * g + p) * LANES, (4 * g + p + 1) * LANES)
                col = jnp.where(first_half, acs_c[:, ha:ha + 1], acs_c[:, ha + 1:ha + 2])
                rowv = jnp.where(first_half, acs_t[ha:ha + 1, :], acs_t[ha + 1:ha + 2, :])
                seg = col - rowv
                lmat = jnp.where(causal2, jnp.exp(jnp.where(causal2, seg, 0.0)), 0.0)
                m = (cb2 * lmat).astype(BF16)
                xblk = xdt[r, ks]
                xblk2 = jnp.concatenate([xblk, xblk], axis=0)
                x_bd = jnp.where(pair_diag, xblk2, 0.0).astype(BF16)
                y_scr[r, ks] = _dot(m, x_bd) + y_inter[:, p * LANES:(p + 1) * LANES] * eacs[r, ks]
            s_t[:, gs] = (s_t[:, gs] * chunk_dec[c * CHUNK:c * CHUNK + 1, gs]
                          + _dot_tn(bc_b16, xdec[r, gs].astype(BF16)))

    y = y_scr[...] + xbc[:, 0:SSM_DINNER] * dskip_ref[...]
    _group_rmsnorm_gate(y, pm_ref[:, 0:SSM_DINNER], nw_ref, y_ref, slice(None))

    @pl.when(t == n_tiles - 1)
    def _():
        for kblk in range(SSM_DINNER // LANES):
            ks = slice(kblk * LANES, (kblk + 1) * LANES)
            st_ref[0, ks, :] = s_t[:, ks].T


def _ssm_core_prompt(pm, pt, cw_t, cbias, alog, dtb, dskip, nw, batch, seq):
    n_tiles = seq // SEQ_TILE
    row = lambda b, t: (b * n_tiles + t, 0)
    return pl.pallas_call(
        functools.partial(_ssm_core_kernel, n_tiles=n_tiles),
        grid=(batch, n_tiles),
        in_specs=[pl.BlockSpec((SEQ_TILE, SSM_MAIN), row),
                  pl.BlockSpec((SEQ_TILE, LANES), row),
                  _resident((CONV_W, SSM_CONV_CH)), _resident((1, SSM_CONV_CH)),
                  _resident((1, LANES)), _resident((1, LANES)),
                  _resident((1, SSM_DINNER)), _resident((1, SSM_DINNER))],
        out_specs=[pl.BlockSpec((SEQ_TILE, SSM_DINNER), row),
                   pl.BlockSpec((1, SSM_DINNER, SSM_DSTATE), lambda b, t: (b, 0, 0)),
                   pl.BlockSpec((1, SUBLANES, SSM_CONV_CH), lambda b, t: (b, 0, 0))],
        out_shape=[jax.ShapeDtypeStruct((batch * seq, SSM_DINNER), BF16),
                   jax.ShapeDtypeStruct((batch, SSM_DINNER, SSM_DSTATE), F32),
                   jax.ShapeDtypeStruct((batch, SUBLANES, SSM_CONV_CH), F32)],
        scratch_shapes=[pltpu.VMEM((SSM_DSTATE, SSM_DINNER), F32),
                        pltpu.VMEM((SEQ_TILE + SUBLANES, SSM_CONV_CH), F32),
                        pltpu.VMEM((SEQ_TILE, SSM_CONV_CH), F32),
                        pltpu.VMEM((SEQ_TILE, SSM_DINNER), F32),
                        pltpu.VMEM((SEQ_TILE, SSM_DINNER), F32),
                        pltpu.VMEM((SEQ_TILE, SSM_DINNER), F32),
                        pltpu.VMEM((SEQ_TILE, SSM_DINNER), F32)],
        compiler_params=_params(2),
        name="ssm_core_prompt",
    )(pm, pt, cw_t, cbias, alog, dtb, dskip, nw)


def _hyb_step_kernel(pm_ref, pt_ref, conv_ref, rs_in, gs_in, cos_ref, sin_ref, cw_ref, alog_ref,
                     dtb_ref, gnw_ref, gnn_ref, mix_ref, conv_out, rs_out, gs_out, o_scr):
    bb = STEP_BATCH
    hd = HEAD_DIM
    ch = GDN_CONV_CH
    cos = cos_ref[...]
    sin = sin_ref[...]
    u = pm_ref[:, 2048:2048 + ch]
    c0 = conv_ref[:, 0:ch]
    c1 = conv_ref[:, ch:2 * ch]
    c2 = conv_ref[:, 2 * ch:3 * ch]
    conv = c0 * cw_ref[0:1, :]
    conv = conv + c1 * cw_ref[1:2, :]
    conv = conv + c2 * cw_ref[2:3, :]
    conv = conv + u * cw_ref[3:4, :]
    qkv = _silu(conv)
    conv_out[:, 0:ch] = c1
    conv_out[:, ch:2 * ch] = c2
    conv_out[:, 2 * ch:3 * ch] = u
    tail = pt_ref[...]
    g_all = -jnp.exp(alog_ref[...]) * _softplus(tail + dtb_ref[...])
    beta_all = _sigmoid(tail)

    for h in range(N_HEADS):
        q = _rotary(pm_ref[:, h * hd:(h + 1) * hd], cos, sin)
        k = _rotary(pm_ref[:, 512 + h * hd:512 + (h + 1) * hd], cos, sin) * (hd ** -0.5)
        v = pm_ref[:, 1024 + h * hd:1024 + (h + 1) * hd]
        k_cols = _columns(k)
        gamma = math.exp(LOG_GAMMA[h])
        for b in range(bb):
            s = rs_in[b, h] * gamma + k_cols[:, b:b + 1] * v[b:b + 1, :]
            rs_out[b, h] = s
            o_scr[b:b + 1, h * hd:(h + 1) * hd] = _dot(q, s)[b:b + 1, :]
        gq = _l2norm(qkv[:, h * hd:(h + 1) * hd]) * (hd ** -0.5)
        gk = _l2norm(qkv[:, 512 + h * hd:512 + (h + 1) * hd])
        gv = qkv[:, 1024 + h * hd:1024 + (h + 1) * hd]
        k_cols = _columns(gk)
        decay = jnp.broadcast_to(jnp.exp(g_all[:, h:h + 1]), (bb, hd))
        beta = jnp.broadcast_to(beta_all[:, 4 + h:5 + h], (bb, hd))
        for b in range(bb):
            s = gs_in[b, h] * decay[b:b + 1, :]
            delta = (gv[b:b + 1, :] - _dot(gk, s)[b:b + 1, :]) * beta[b:b + 1, :]
            s = s + k_cols[:, b:b + 1] * delta
            gs_out[b, h] = s
            o_scr[b:b + 1, 512 + h * hd:512 + (h + 1) * hd] = _dot(gq, s)[b:b + 1, :]

    for h in range(N_HEADS):
        o = o_scr[:, h * hd:(h + 1) * hd]
        mu = jnp.mean(o, axis=-1, keepdims=True)
        var = jnp.mean(jnp.square(o - mu), axis=-1, keepdims=True)
        ro = (o - mu) * lax.rsqrt(var + EPS) * gnw_ref[:, h * hd:(h + 1) * hd]
        mix_ref[:, h * hd:(h + 1) * hd] = _silu(pm_ref[:, 1536 + h * hd:1536 + (h + 1) * hd]) * ro
        go = o_scr[:, 512 + h * hd:512 + (h + 1) * hd]
        go = go * lax.rsqrt(jnp.mean(go * go, axis=-1, keepdims=True) + EPS) * gnn_ref[...]
        mix_ref[:, 512 + h * hd:512 + (h + 1) * hd] = go * _silu(pm_ref[:, 3584 + h * hd:3584 + (h + 1) * hd])


def _hyb_core_step(pm, pt, conv_state, ret_state, gdn_state, cos, sin, cw_t, alog, dtb, gnw, gnn):
    batch = pm.shape[0]
    bb = STEP_BATCH
    row = lambda i: (i, 0)
    state = pl.BlockSpec((bb, N_HEADS, HEAD_DIM, HEAD_DIM), lambda i: (i, 0, 0, 0))
    state_shape = jax.ShapeDtypeStruct((batch, N_HEADS, HEAD_DIM, HEAD_DIM), F32)
    return pl.pallas_call(
        _hyb_step_kernel,
        grid=(batch // bb,),
        in_specs=[pl.BlockSpec((bb, HYB_MAIN), row), pl.BlockSpec((bb, LANES), row),
                  pl.BlockSpec((bb, 3 * GDN_CONV_CH), row), state, state,
                  _resident((1, LANES)), _resident((1, LANES)),
                  _resident((CONV_W, GDN_CONV_CH)), _resident((1, LANES)), _resident((1, LANES)),
                  _resident((1, 512)), _resident((1, LANES))],
        out_specs=[pl.BlockSpec((bb, D_MODEL), row), pl.BlockSpec((bb, 3 * GDN_CONV_CH), row),
                   state, state],
        out_shape=[jax.ShapeDtypeStruct((batch, D_MODEL), F32),
                   jax.ShapeDtypeStruct((batch, 3 * GDN_CONV_CH), F32), state_shape, state_shape],
        scratch_shapes=[pltpu.VMEM((bb, D_MODEL), F32)],
        compiler_params=_params(1),
        name="hyb_core_step",
    )(pm, pt, conv_state, ret_state, gdn_state, cos, sin, cw_t, alog, dtb, gnw, gnn)


def _ssm_step_kernel(pm_ref, pt_ref, conv_ref, s_in, cw_ref, cbias_ref, alog_ref, dtb_ref,
                     dskip_ref, nw_ref, y_ref, conv_out, s_out, y_scr):
    bb = STEP_BATCH
    ch = SSM_CONV_CH
    u = pm_ref[:, SSM_DINNER:SSM_MAIN]
    c0 = conv_ref[:, 0:ch]
    c1 = conv_ref[:, ch:2 * ch]
    c2 = conv_ref[:, 2 * ch:3 * ch]
    conv = c0 * cw_ref[0:1, :]
    conv = conv + c1 * cw_ref[1:2, :]
    conv = conv + c2 * cw_ref[2:3, :]
    conv = conv + u * cw_ref[3:4, :]
    xbc = _silu(conv + cbias_ref[...])
    conv_out[:, 0:ch] = c1
    conv_out[:, ch:2 * ch] = c2
    conv_out[:, 2 * ch:3 * ch] = u
    xs = xbc[:, 0:SSM_DINNER]
    expand = _head_expand_matrix()
    dt = _softplus(pt_ref[...] + dtb_ref[...])
    a = dt * (-jnp.exp(alog_ref[...]))
    x_dt = xs * _expand_heads(dt, expand)
    decay = _expand_heads(jnp.exp(a), expand)
    for kblk in range(SSM_DINNER // LANES):
        ks = slice(kblk * LANES, (kblk + 1) * LANES)
        g = kblk // 4
        x_cols = _columns(x_dt[:, ks])
        d_cols = _columns(decay[:, ks])
        bm = xbc[:, SSM_DINNER + g * LANES:SSM_DINNER + (g + 1) * LANES]
        cm = xbc[:, SSM_DINNER + 512 + g * LANES:SSM_DINNER + 512 + (g + 1) * LANES]
        for b in range(bb):
            s = s_in[b, ks, :] * d_cols[:, b:b + 1] + x_cols[:, b:b + 1] * bm[b:b + 1, :]
            s_out[b, ks, :] = s
            y_scr[b:b + 1, ks] = _dot_nt(cm, s)[b:b + 1, :]
    y = y_scr[...] + xs * dskip_ref[...]
    _group_rmsnorm_gate(y, pm_ref[:, 0:SSM_DINNER], nw_ref, y_ref, slice(None))


def _ssm_core_step(pm, pt, conv_state, ssm_state, cw_t, cbias, alog, dtb, dskip, nw):
    batch = pm.shape[0]
    bb = STEP_BATCH
    row = lambda i: (i, 0)
    state = pl.BlockSpec((bb, SSM_DINNER, SSM_DSTATE), lambda i: (i, 0, 0))
    return pl.pallas_call(
        _ssm_step_kernel,
        grid=(batch // bb,),
        in_specs=[pl.BlockSpec((bb, SSM_MAIN), row), pl.BlockSpec((bb, LANES), row),
                  pl.BlockSpec((bb, 3 * SSM_CONV_CH), row), state,
                  _resident((CONV_W, SSM_CONV_CH)), _resident((1, SSM_CONV_CH)),
                  _resident((1, LANES)), _resident((1, LANES)),
                  _resident((1, SSM_DINNER)), _resident((1, SSM_DINNER))],
        out_specs=[pl.BlockSpec((bb, SSM_DINNER), row), pl.BlockSpec((bb, 3 * SSM_CONV_CH), row),
                   state],
        out_shape=[jax.ShapeDtypeStruct((batch, SSM_DINNER), F32),
                   jax.ShapeDtypeStruct((batch, 3 * SSM_CONV_CH), F32),
                   jax.ShapeDtypeStruct((batch, SSM_DINNER, SSM_DSTATE), F32)],
        scratch_shapes=[pltpu.VMEM((bb, SSM_DINNER), F32)],
        compiler_params=_params(1),
        name="ssm_core_step",
    )(pm, pt, conv_state, ssm_state, cw_t, cbias, alog, dtb, dskip, nw)


def _rope_tables(pos):
    inv_freq = 1.0 / (ROPE_BASE ** jnp.linspace(0.0, 1.0, HEAD_DIM // 2, dtype=F32))
    ang = pos.astype(F32)[:, None] * inv_freq[None, :]
    cos = jnp.repeat(jnp.cos(ang), 2, axis=-1)
    sign = jnp.tile(jnp.array([-1.0, 1.0], F32), HEAD_DIM // 2)
    sin = jnp.repeat(jnp.sin(ang), 2, axis=-1) * sign
    return cos, sin


def _lane_row(v):
    return jnp.pad(v.astype(F32), (0, LANES - v.shape[0])).reshape(1, LANES)


def kernel(x_prompt, x_sample, state_ret, state_gdn, state_gdn_conv, state_ssm, state_ssm_conv,
           norm_mix_w, norm_ffn_w, norm_final_w, hyb_w_in, hyb_w_out, ret_gn_w, gdn_conv_w,
           gdn_a_log, gdn_dt_bias, gdn_norm_w, ssm_w_in, ssm_conv_w, ssm_conv_b, ssm_dt_bias,
           ssm_a_log, ssm_d, ssm_norm_w, ssm_w_out, ffn_w_gate, ffn_w_up, ffn_w_down):
    depth = norm_mix_w.shape[0]
    bp, seq, d = x_prompt.shape
    bs = x_sample.shape[0]
    past_len = 16384

    hyb_main = hyb_w_in[:, :, :HYB_MAIN].astype(BF16)
    hyb_tail = jnp.pad(hyb_w_in[:, :, HYB_MAIN:], ((0, 0), (0, 0), (0, LANES - 2 * N_HEADS))).astype(BF16)
    ssm_main = ssm_w_in[:, :, :SSM_MAIN].astype(BF16)
    ssm_tail = jnp.pad(ssm_w_in[:, :, SSM_MAIN:], ((0, 0), (0, 0), (0, LANES - SSM_HEADS))).astype(BF16)
    hyb_out = hyb_w_out.astype(BF16)
    ssm_out = ssm_w_out.astype(BF16)
    w_gate = ffn_w_gate.astype(BF16)
    w_up = ffn_w_up.astype(BF16)
    w_down = ffn_w_down.astype(BF16)

    gdn_cw_t = jnp.swapaxes(gdn_conv_w, 1, 2)
    ssm_cw_t = jnp.swapaxes(ssm_conv_w, 1, 2)
    dskip = jnp.repeat(ssm_d, SSM_HEADDIM, axis=-1)
    cos_p, sin_p = _rope_tables(jnp.arange(seq))
    cos_s, sin_s = _rope_tables(jnp.arange(past_len, past_len + 1))

    def trunk(x, prompt):
        t = x.shape[0]
        ret_l, gdn_l, gconv_l, ssm_l, sconv_l = [], [], [], [], []
        for i in range(depth):
            j = i // 2
            if i % 2 == 0:
                pm, pt = _norm_proj(x, norm_mix_w[i], hyb_main[j], hyb_tail[j])
                args = (gdn_cw_t[j], _lane_row(gdn_a_log[j]), _lane_row(gdn_dt_bias[j]),
                        ret_gn_w[j].reshape(1, 512), gdn_norm_w[j].reshape(1, LANES))
                if prompt:
                    mixed, rs, gs, cb = _hyb_core_prompt(pm, pt, cos_p, sin_p, *args, bp, seq)
                    cb = cb[:, SUBLANES - (CONV_W - 1):, :]
                else:
                    mixed, cb, rs, gs = _hyb_core_step(
                        pm, pt, state_gdn_conv[j].reshape(t, -1), state_ret[j], state_gdn[j],
                        cos_s, sin_s, *args)
                    cb = cb.reshape(t, CONV_W - 1, GDN_CONV_CH)
                ret_l.append(rs)
                gdn_l.append(gs)
                gconv_l.append(cb)
                x = _out_proj(x, mixed, hyb_out[j])
            else:
                pm, pt = _norm_proj(x, norm_mix_w[i], ssm_main[j], ssm_tail[j])
                args = (ssm_cw_t[j], ssm_conv_b[j].reshape(1, -1), _lane_row(ssm_a_log[j]),
                        _lane_row(ssm_dt_bias[j]), dskip[j].reshape(1, -1),
                        ssm_norm_w[j].reshape(1, -1))
                if prompt:
                    y, ss, cb = _ssm_core_prompt(pm, pt, *args, bp, seq)
                    cb = cb[:, SUBLANES - (CONV_W - 1):, :]
                else:
                    y, cb, ss = _ssm_core_step(
                        pm, pt, state_ssm_conv[j].reshape(t, -1),
                        state_ssm[j].reshape(t, SSM_DINNER, SSM_DSTATE), *args)
                    cb = cb.reshape(t, CONV_W - 1, SSM_CONV_CH)
                ssm_l.append(ss.reshape(-1, SSM_HEADS, SSM_HEADDIM, SSM_DSTATE))
                sconv_l.append(cb)
                x = _out_proj(x, y, ssm_out[j])
            x = _ffn(x, norm_ffn_w[i], w_gate[i], w_up[i], w_down[i], norm_final_w,
                     final_norm=(i == depth - 1))
        return (x, jnp.stack(ret_l), jnp.stack(gdn_l), jnp.stack(gconv_l), jnp.stack(ssm_l),
                jnp.stack(sconv_l))

    yp, ret_p, gdn_p, gconv_p, ssm_p, sconv_p = trunk(x_prompt.reshape(bp * seq, d), True)
    ys, ret_s, gdn_s, gconv_s, ssm_s, sconv_s = trunk(x_sample.reshape(bs, d), False)
    return (yp.reshape(bp, seq, d), ys.reshape(bs, 1, d),
            ret_p, gdn_p, gconv_p, ssm_p, sconv_p,
            ret_s, gdn_s, gconv_s, ssm_s, sconv_s)
```

```python
import functools
import math

import jax
import jax.numpy as jnp
from jax import lax
from jax.experimental import pallas as pl
from jax.experimental.pallas import tpu as pltpu

F32 = jnp.float32
BF16 = jnp.bfloat16

D_MODEL = 1024
N_HEADS = 4
HEAD_DIM = 128
HYB_MAIN = 4096
GDN_CONV_CH = 1536
SSM_DINNER = 2048
SSM_HEADS = 32
SSM_HEADDIM = 64
SSM_GROUPS = 4
SSM_DSTATE = 128
SSM_GROUP_W = SSM_DINNER // SSM_GROUPS
SSM_CONV_CH = 3072
SSM_MAIN = 5120
D_FF = 2816
FF_CHUNK = 256
CONV_W = 4
EPS = 1e-6
ROPE_BASE = 10000.0

LANES = 128
SUBLANES = 8
VMEM_LIMIT = 56 * 1024 * 1024

SEQ_TILE = 256
CHUNK = 64
STEP_BATCH = 8
LOG_GAMMA = tuple(math.log1p(-2.0 ** (-5.0 - h)) for h in range(N_HEADS))

_NT = (((1,), (1,)), ((), ()))
_TN = (((0,), (0,)), ((), ()))


def _dot(a, b):
    return jnp.dot(a, b, preferred_element_type=F32)


def _dot_nt(a, b):
    return lax.dot_general(a, b, _NT, preferred_element_type=F32)


def _dot_tn(a, b):
    return lax.dot_general(a, b, _TN, preferred_element_type=F32)


def _dot_exact(a, b):
    return jnp.dot(a, b, preferred_element_type=F32, precision=lax.Precision.HIGHEST)


def _silu(x):
    return x / (1.0 + jnp.exp(-x))


def _sigmoid(x):
    return 1.0 / (1.0 + jnp.exp(-x))


def _softplus(x):
    return jnp.maximum(x, 0.0) + jnp.log1p(jnp.exp(-jnp.abs(x)))


def _rmsnorm(x, w):
    return x * lax.rsqrt(jnp.mean(x * x, axis=-1, keepdims=True) + EPS) * w


def _resident(shape):
    nd = len(shape)
    return pl.BlockSpec(shape, lambda *_: (0,) * nd, pipeline_mode=pl.Buffered(1))


def _resident_layer(shape, layer):
    nd = len(shape)
    return pl.BlockSpec((None,) + tuple(shape), lambda *_: (layer,) + (0,) * nd,
                        pipeline_mode=pl.Buffered(1))


def _params(n_axes):
    return pltpu.CompilerParams(dimension_semantics=("arbitrary",) * n_axes,
                                vmem_limit_bytes=VMEM_LIMIT)


def _row_tile(t):
    return 512 if t % 512 == 0 else t


def _norm_proj_kernel(x_ref, nw_ref, w_ref, wt_ref, o_ref, ot_ref, *, n_chunk):
    h = _rmsnorm(x_ref[...], nw_ref[...]).astype(BF16)
    n = w_ref.shape[1]
    for c in range(n // n_chunk):
        sl = slice(c * n_chunk, (c + 1) * n_chunk)
        o_ref[:, sl] = _dot(h, w_ref[:, sl])
    ot_ref[...] = _dot(h, wt_ref[...])


def _norm_proj(x, nw, w_all, layer, n, w_tail):
    t, d = x.shape
    tm = _row_tile(t)
    return pl.pallas_call(
        functools.partial(_norm_proj_kernel, n_chunk=512),
        grid=(t // tm,),
        in_specs=[pl.BlockSpec((tm, d), lambda i: (i, 0)),
                  _resident((1, d)), _resident_layer((d, n), layer), _resident((d, LANES))],
        out_specs=[pl.BlockSpec((tm, n), lambda i: (i, 0)),
                   pl.BlockSpec((tm, LANES), lambda i: (i, 0))],
        out_shape=[jax.ShapeDtypeStruct((t, n), F32), jax.ShapeDtypeStruct((t, LANES), F32)],
        compiler_params=_params(1),
        name="norm_proj",
    )(x, nw.reshape(1, d), w_all, w_tail)


def _out_proj_kernel(x_ref, a_ref, w_ref, o_ref):
    o_ref[...] = x_ref[...] + _dot(a_ref[...].astype(BF16), w_ref[...])


def _out_proj(x, a, w_all, layer):
    t, d = x.shape
    k = a.shape[1]
    tm = _row_tile(t)
    return pl.pallas_call(
        _out_proj_kernel,
        grid=(t // tm,),
        in_specs=[pl.BlockSpec((tm, d), lambda i: (i, 0)),
                  pl.BlockSpec((tm, k), lambda i: (i, 0)),
                  _resident_layer((k, d), layer)],
        out_specs=pl.BlockSpec((tm, d), lambda i: (i, 0)),
        out_shape=jax.ShapeDtypeStruct((t, d), F32),
        compiler_params=_params(1),
        name="out_proj",
    )(x, a, w_all)


def _ffn_kernel(x_ref, nw_ref, wg_ref, wu_ref, wd_ref, fw_ref, o_ref, act_ref, *, final_norm):
    x = x_ref[...]
    h = _rmsnorm(x, nw_ref[...]).astype(BF16)
    for c in range(D_FF // FF_CHUNK):
        sl = slice(c * FF_CHUNK, (c + 1) * FF_CHUNK)
        gate = _dot(h, wg_ref[:, sl])
        up = _dot(h, wu_ref[:, sl])
        act_ref[:, sl] = (_silu(gate) * up).astype(BF16)
    y = x + _dot(act_ref[...], wd_ref[...])
    if final_norm:
        y = _rmsnorm(y, fw_ref[...])
    o_ref[...] = y


def _ffn(x, nw, wg, wu, wd, layer, final_w, final_norm):
    t, d = x.shape
    tm = _row_tile(t)
    return pl.pallas_call(
        functools.partial(_ffn_kernel, final_norm=final_norm),
        grid=(t // tm,),
        in_specs=[pl.BlockSpec((tm, d), lambda i: (i, 0)),
                  _resident((1, d)), _resident_layer((d, D_FF), layer),
                  _resident_layer((d, D_FF), layer), _resident_layer((D_FF, d), layer),
                  _resident((1, d))],
        out_specs=pl.BlockSpec((tm, d), lambda i: (i, 0)),
        out_shape=jax.ShapeDtypeStruct((t, d), F32),
        scratch_shapes=[pltpu.VMEM((tm, D_FF), BF16)],
        compiler_params=_params(1),
        name="ffn",
    )(x, nw.reshape(1, d), wg, wu, wd, final_w.reshape(1, d))


def _rotary(x, cos, sin_signed):
    lane = lax.broadcasted_iota(jnp.int32, x.shape, 1)
    partner = jnp.where((lane & 1) == 0, pltpu.roll(x, LANES - 1, 1), pltpu.roll(x, 1, 1))
    return x * cos + partner * sin_signed


def _l2norm(x):
    return x * lax.rsqrt(jnp.sum(x * x, axis=-1, keepdims=True) + EPS)


def _split3(x):
    hi = x.astype(BF16)
    r1 = x - hi.astype(F32)
    mid = r1.astype(BF16)
    lo = (r1 - mid.astype(F32)).astype(BF16)
    return hi, mid, lo


def _expand_heads(x, expand_mat):
    hi, mid, lo = _split3(x)
    return _dot(hi, expand_mat) + _dot(mid, expand_mat) + _dot(lo, expand_mat)


def _head_expand_matrix():
    row = lax.broadcasted_iota(jnp.int32, (LANES, SSM_DINNER), 0)
    col = lax.broadcasted_iota(jnp.int32, (LANES, SSM_DINNER), 1)
    return jnp.where(row == (col >> 6), 1.0, 0.0).astype(BF16)


def _chunk_sum_matrices(n):
    ii = lax.broadcasted_iota(jnp.int32, (n, n), 0)
    jj = lax.broadcasted_iota(jnp.int32, (n, n), 1)
    same = (ii >> 6) == (jj >> 6)
    prefix = jnp.where(same & (ii >= jj), 1.0, 0.0).astype(F32)
    total = jnp.where(same, 1.0, 0.0).astype(F32)
    return prefix, total


def _columns(rows):
    pad = jnp.zeros((LANES - rows.shape[0], LANES), F32)
    return jnp.concatenate([rows, pad], axis=0).T


def _unit_lower_inverses(lows, interleaved=()):
    interleaved = list(interleaved)
    n = lows[0].shape[0]
    ii = lax.broadcasted_iota(jnp.int32, (n, n), 0)
    jj = lax.broadcasted_iota(jnp.int32, (n, n), 1)
    eye = (ii == jj).astype(F32)
    pair = (ii >> 1) == (jj >> 1)
    ts = [eye - jnp.where(pair, low, 0.0) for low in lows]
    lows_b = [low.astype(BF16) for low in lows]
    for k in range(1, 6):
        join = ((ii >> (k + 1)) == (jj >> (k + 1))) & ((ii >> k) != (jj >> k))
        join_b = join.astype(F32).astype(BF16)
        ts_b = [t.astype(BF16) for t in ts]
        txs = [_dot(t_b, low_b * join_b).astype(BF16) for t_b, low_b in zip(ts_b, lows_b)]
        ts = [t - _dot(tx, t_b) for t, tx, t_b in zip(ts, txs, ts_b)]
        if interleaved:
            interleaved.pop(0)()
    for fn in interleaved:
        fn()
    return [t - eye for t in ts]


def _hyb_core_kernel(pm_ref, pt_ref, cos_ref, sin_ref, cw_ref, alog_ref, dtb_ref, gnw_ref, gnn_ref,
                     mix_ref, rs_ref, gs_ref, cb_ref, sret, sgdn, xx, qkv, intra_scr, *, n_tiles):
    t = pl.program_id(1)
    tl = SEQ_TILE
    hd = HEAD_DIM

    @pl.when(t == 0)
    def _():
        sret[...] = jnp.zeros_like(sret)
        sgdn[...] = jnp.zeros_like(sgdn)
        xx[0:SUBLANES, :] = jnp.zeros((SUBLANES, GDN_CONV_CH), F32)

    @pl.when((t == 0) & (pl.program_id(0) == 0))
    def _():
        ii = lax.broadcasted_iota(jnp.int32, (tl, tl), 0)
        jj = lax.broadcasted_iota(jnp.int32, (tl, tl), 1)
        diff = (ii - jj).astype(F32)
        for h in range(N_HEADS):
            intra_scr[h] = jnp.where(diff >= 0.0, jnp.exp(LOG_GAMMA[h] * jnp.maximum(diff, 0.0)), 0.0)

    for blk in range(GDN_CONV_CH // LANES):
        cs = slice(blk * LANES, (blk + 1) * LANES)
        u = pm_ref[:, 2048 + blk * LANES:2048 + (blk + 1) * LANES]
        xx[SUBLANES:SUBLANES + tl, cs] = u
        conv = xx[SUBLANES - 3:SUBLANES - 3 + tl, cs] * cw_ref[0:1, cs]
        conv = conv + xx[SUBLANES - 2:SUBLANES - 2 + tl, cs] * cw_ref[1:2, cs]
        conv = conv + xx[SUBLANES - 1:SUBLANES - 1 + tl, cs] * cw_ref[2:3, cs]
        conv = conv + u * cw_ref[3:4, cs]
        act = _silu(conv)
        if blk < N_HEADS:
            act = _l2norm(act) * (hd ** -0.5)
        elif blk < 2 * N_HEADS:
            act = _l2norm(act)
        qkv[:, cs] = act
    xx[0:SUBLANES, :] = xx[tl:tl + SUBLANES, :]

    @pl.when(t == n_tiles - 1)
    def _():
        cb_ref[0] = xx[tl:tl + SUBLANES, :]

    cos = cos_ref[...]
    sin = sin_ref[...]
    idx = lax.broadcasted_iota(jnp.int32, (tl, 1), 0).astype(F32)

    def retention_head(h):
        lg = LOG_GAMMA[h]
        q = _rotary(pm_ref[:, h * hd:(h + 1) * hd], cos, sin)
        k = _rotary(pm_ref[:, 512 + h * hd:512 + (h + 1) * hd], cos, sin) * (hd ** -0.5)
        v = pm_ref[:, 1024 + h * hd:1024 + (h + 1) * hd]
        gate = pm_ref[:, 1536 + h * hd:1536 + (h + 1) * hd]
        q_dec = jnp.exp(lg * (idx + 1.0))
        k_dec = jnp.exp(lg * (tl - 1.0 - idx))
        s = sret[h]
        qb = q.astype(BF16)
        vb = v.astype(BF16)
        scores = _dot_nt(qb, k.astype(BF16)) * intra_scr[h]
        o = _dot(scores.astype(BF16), vb) + _dot(qb, s.astype(BF16)) * q_dec
        sret[h] = s * math.exp(lg * tl) + _dot_tn((k * k_dec).astype(BF16), vb)
        mu = jnp.mean(o, axis=-1, keepdims=True)
        var = jnp.mean(jnp.square(o - mu), axis=-1, keepdims=True)
        ro = (o - mu) * lax.rsqrt(var + EPS) * gnw_ref[:, h * hd:(h + 1) * hd]
        mix_ref[:, h * hd:(h + 1) * hd] = (_silu(gate) * ro).astype(BF16)

    tail = pt_ref[...]
    g_all = -jnp.exp(alog_ref[...]) * _softplus(tail + dtb_ref[...])
    beta_all = _sigmoid(tail)
    prefix, total = _chunk_sum_matrices(tl)
    gc_all = _dot_exact(prefix, g_all)
    gl_all = _dot_exact(total, g_all)

    n = N_HEADS * CHUNK
    si = lax.broadcasted_iota(jnp.int32, (n, n), 0)
    sj = lax.broadcasted_iota(jnp.int32, (n, n), 1)
    same_head = (si >> 6) == (sj >> 6)
    causal = same_head & (si >= sj)
    strict = same_head & (si > sj)

    def stack(fn):
        return jnp.concatenate([fn(h) for h in range(N_HEADS)], axis=0)

    chunks = []
    for c in range(tl // CHUNK):
        r = slice(c * CHUNK, (c + 1) * CHUNK)
        q_st = stack(lambda h: qkv[r, h * hd:(h + 1) * hd])
        k_st = stack(lambda h: qkv[r, 512 + h * hd:512 + (h + 1) * hd])
        v_st = stack(lambda h: qkv[r, 1024 + h * hd:1024 + (h + 1) * hd])
        beta = stack(lambda h: beta_all[r, 4 + h:5 + h])
        gc = stack(lambda h: gc_all[r, h:h + 1])
        gl = stack(lambda h: gl_all[r, h:h + 1])
        gc_col = jnp.broadcast_to(gc, (n, n))
        seg = gc_col - gc_col.T
        decay = jnp.where(causal, jnp.exp(jnp.where(causal, seg, 0.0)), 0.0)
        k_b16 = k_st.astype(BF16)
        qk = _dot_nt(jnp.concatenate([q_st.astype(BF16), k_b16], axis=0), k_b16)
        egc = jnp.exp(gc)
        chunks.append(dict(
            r=r, gl=gl,
            low=jnp.where(strict, qk[n:] * beta * decay, 0.0),
            rhs=jnp.concatenate([v_st * beta, k_st * (beta * egc)], axis=1),
            scores=(qk[:n] * decay).astype(BF16),
            q_in=(q_st * egc).astype(BF16),
            k_out=(k_st * jnp.exp(gl - gc)).astype(BF16)))
    t_offs = _unit_lower_inverses(
        [ch["low"] for ch in chunks],
        interleaved=[functools.partial(retention_head, h) for h in range(N_HEADS)])
    sols = [ch["rhs"] + _dot(t_off.astype(BF16), ch["rhs"].astype(BF16))
            for ch, t_off in zip(chunks, t_offs)]

    for ch, sol in zip(chunks, sols):
        r = ch["r"]
        w_b16 = sol[:, hd:].astype(BF16)
        v_new = []
        o_inter = []
        for h in range(N_HEADS):
            hr = slice(h * CHUNK, (h + 1) * CHUNK)
            lhs = jnp.concatenate([w_b16[hr], ch["q_in"][hr]], axis=0)
            prod = _dot(lhs, sgdn[h].astype(BF16))
            v_new.append(sol[hr, :hd] - prod[:CHUNK])
            o_inter.append(prod[CHUNK:])
        vn_b16 = jnp.concatenate(v_new, axis=0).astype(BF16)
        o_st = jnp.concatenate(o_inter, axis=0) + _dot(ch["scores"], vn_b16)
        for h in range(N_HEADS):
            hr = slice(h * CHUNK, (h + 1) * CHUNK)
            s_dec = jnp.exp(jnp.broadcast_to(ch["gl"][hr], (CHUNK, hd)))
            s_dec = jnp.concatenate([s_dec, s_dec], axis=0)
            sgdn[h] = sgdn[h] * s_dec + _dot_tn(ch["k_out"][hr], vn_b16[hr])
            o_h = o_st[hr]
            go = o_h * lax.rsqrt(jnp.mean(o_h * o_h, axis=-1, keepdims=True) + EPS) * gnn_ref[...]
            gz = pm_ref[r, 3584 + h * hd:3584 + (h + 1) * hd]
            mix_ref[r, 512 + h * hd:512 + (h + 1) * hd] = (go * _silu(gz)).astype(BF16)

    @pl.when(t == n_tiles - 1)
    def _():
        rs_ref[0] = sret[...]
        gs_ref[0] = sgdn[...]


def _hyb_core_prompt(pm, pt, cos, sin, cw_t, alog, dtb, gnw, gnn, batch, seq):
    n_tiles = seq // SEQ_TILE
    row = lambda b, t: (b * n_tiles + t, 0)
    state = pl.BlockSpec((1, N_HEADS, HEAD_DIM, HEAD_DIM), lambda b, t: (b, 0, 0, 0))
    return pl.pallas_call(
        functools.partial(_hyb_core_kernel, n_tiles=n_tiles),
        grid=(batch, n_tiles),
        in_specs=[pl.BlockSpec((SEQ_TILE, HYB_MAIN), row),
                  pl.BlockSpec((SEQ_TILE, LANES), row),
                  pl.BlockSpec((SEQ_TILE, LANES), lambda b, t: (t, 0)),
                  pl.BlockSpec((SEQ_TILE, LANES), lambda b, t: (t, 0)),
                  _resident((CONV_W, GDN_CONV_CH)), _resident((1, LANES)), _resident((1, LANES)),
                  _resident((1, 512)), _resident((1, LANES))],
        out_specs=[pl.BlockSpec((SEQ_TILE, D_MODEL), row), state, state,
                   pl.BlockSpec((1, SUBLANES, GDN_CONV_CH), lambda b, t: (b, 0, 0))],
        out_shape=[jax.ShapeDtypeStruct((batch * seq, D_MODEL), BF16),
                   jax.ShapeDtypeStruct((batch, N_HEADS, HEAD_DIM, HEAD_DIM), F32),
                   jax.ShapeDtypeStruct((batch, N_HEADS, HEAD_DIM, HEAD_DIM), F32),
                   jax.ShapeDtypeStruct((batch, SUBLANES, GDN_CONV_CH), F32)],
        scratch_shapes=[pltpu.VMEM((N_HEADS, HEAD_DIM, HEAD_DIM), F32),
                        pltpu.VMEM((N_HEADS, HEAD_DIM, HEAD_DIM), F32),
                        pltpu.VMEM((SEQ_TILE + SUBLANES, GDN_CONV_CH), F32),
                        pltpu.VMEM((SEQ_TILE, GDN_CONV_CH), F32),
                        pltpu.VMEM((N_HEADS, SEQ_TILE, SEQ_TILE), F32)],
        compiler_params=_params(2),
        name="hyb_core_prompt",
    )(pm, pt, cos, sin, cw_t, alog, dtb, gnw, gnn)


def _group_rmsnorm_gate(y, z, nw_ref, o_ref, rows):
    for g in range(SSM_GROUPS):
        sl = slice(g * SSM_GROUP_W, (g + 1) * SSM_GROUP_W)
        yg = y[:, sl] * _silu(z[:, sl])
        yg = yg * lax.rsqrt(jnp.mean(yg * yg, axis=-1, keepdims=True) + EPS)
        o_ref[rows, sl] = (yg * nw_ref[:, sl]).astype(o_ref.dtype)


def _ssm_core_kernel(pm_ref, pt_ref, cw_ref, cbias_ref, alog_ref, dtb_ref, dskip_ref, nw_ref,
                     y_ref, st_ref, cb_ref, s_t, xx, xbc, xdt, xdec, eacs, acs_e, y_scr, *, n_tiles):
    t = pl.program_id(1)
    tl = SEQ_TILE

    @pl.when(t == 0)
    def _():
        s_t[...] = jnp.zeros_like(s_t)
        xx[0:SUBLANES, :] = jnp.zeros((SUBLANES, SSM_CONV_CH), F32)

    n_chunks = tl // CHUNK
    expand = _head_expand_matrix()
    dt = _softplus(pt_ref[...] + dtb_ref[...])
    a = dt * (-jnp.exp(alog_ref[...]))
    prefix, total = _chunk_sum_matrices(tl)
    acs = _dot_exact(prefix, a)
    atot = _dot_exact(total, a)
    dt_parts = _split3(dt)
    dec_parts = _split3(jnp.exp(atot - acs))
    eacs_parts = _split3(jnp.exp(acs))
    acs_parts = _split3(acs)
    chunk_rows = [atot[c * CHUNK:c * CHUNK + 1] for c in range(n_chunks)]
    chunk_rows.append(jnp.zeros((SUBLANES - n_chunks, LANES), F32))
    cdec_parts = _split3(jnp.exp(jnp.concatenate(chunk_rows, axis=0)))

    def expand_group(parts, g):
        e = expand[:, g * SSM_GROUP_W:(g + 1) * SSM_GROUP_W]
        return _dot(parts[0], e) + _dot(parts[1], e) + _dot(parts[2], e)

    def conv_block(blk):
        cs = slice(blk * LANES, (blk + 1) * LANES)
        u = pm_ref[:, SSM_DINNER + blk * LANES:SSM_DINNER + (blk + 1) * LANES]
        xx[SUBLANES:SUBLANES + tl, cs] = u
        conv = xx[SUBLANES - 3:SUBLANES - 3 + tl, cs] * cw_ref[0:1, cs]
        conv = conv + xx[SUBLANES - 2:SUBLANES - 2 + tl, cs] * cw_ref[1:2, cs]
        conv = conv + xx[SUBLANES - 1:SUBLANES - 1 + tl, cs] * cw_ref[2:3, cs]
        conv = conv + u * cw_ref[3:4, cs]
        xbc[:, cs] = _silu(conv + cbias_ref[:, cs])

    named_parts = dict(dt=dt_parts, dec=dec_parts, eacs=eacs_parts, acs=acs_parts, cdec=cdec_parts)
    expanded = {}

    def expansion_jobs(g):
        def job(name):
            expanded[name, g] = expand_group(named_parts[name], g)
        return [functools.partial(job, name) for name in named_parts]

    def conv_jobs(g):
        blocks = list(range(4 * g, 4 * g + 4)) + [16 + g, 20 + g]
        return [functools.partial(conv_block, blk) for blk in blocks]

    for job in expansion_jobs(0):
        job()
    for g in range(SSM_GROUPS):
        gs = slice(g * SSM_GROUP_W, (g + 1) * SSM_GROUP_W)
        ahead = expansion_jobs(g + 1) if g + 1 < SSM_GROUPS else []
        for k, job in enumerate(conv_jobs(g)):
            job()
            if k < len(ahead):
                ahead[k]()
        x_dt = xbc[:, gs] * expanded["dt", g]
        xdt[:, gs] = x_dt
        xdec[:, gs] = (x_dt * expanded["dec", g]).astype(BF16)
        eacs[:, gs] = expanded["eacs", g]
        acs_e[:, gs] = expanded["acs", g]
    chunk_dec = [expanded["cdec", g] for g in range(SSM_GROUPS)]
    xx[0:SUBLANES, :] = xx[tl:tl + SUBLANES, :]

    @pl.when(t == n_tiles - 1)
    def _():
        cb_ref[0] = xx[tl:tl + SUBLANES, :]

    lane = lax.broadcasted_iota(jnp.int32, (CHUNK, LANES), 1)
    row_i = lax.broadcasted_iota(jnp.int32, (CHUNK, LANES), 0)
    first_half = lane < SSM_HEADDIM
    causal2 = row_i >= (lane & (CHUNK - 1))
    lane2 = lax.broadcasted_iota(jnp.int32, (2 * CHUNK, LANES), 1)
    row2 = lax.broadcasted_iota(jnp.int32, (2 * CHUNK, LANES), 0)
    pair_diag = (row2 < CHUNK) == (lane2 < SSM_HEADDIM)

    def bc_cc(r, g):
        bc = xbc[r, SSM_DINNER + g * LANES:SSM_DINNER + (g + 1) * LANES]
        cc = xbc[r, SSM_DINNER + 512 + g * LANES:SSM_DINNER + 512 + (g + 1) * LANES]
        return bc.astype(BF16), cc.astype(BF16)

    for c in range(n_chunks):
        r = slice(c * CHUNK, (c + 1) * CHUNK)
        acs_c = acs[r]
        acs_t = jnp.concatenate([acs_c, acs_c], axis=0).T
        cb2 = []
        for g in range(SSM_GROUPS):
            bc_b16, cc_b16 = bc_cc(r, g)
            cb2.append(_dot_nt(cc_b16, jnp.concatenate([bc_b16, bc_b16], axis=0)))
        pairs = range(SSM_HEADS // 2)
        ms = []
        for p in pairs:
            ha = 2 * p
            rowv = jnp.where(first_half, acs_t[ha:ha + 1, :], acs_t[ha + 1:ha + 2, :])
            seg = acs_e[r, p * LANES:(p + 1) * LANES] - rowv
            lmat = jnp.where(causal2, jnp.exp(jnp.where(causal2, seg, 0.0)), 0.0)
            ms.append((cb2[p // 4] * lmat).astype(BF16))
        x_bds = []
        for p in pairs:
            xblk = xdt[r, p * LANES:(p + 1) * LANES]
            xblk2 = jnp.concatenate([xblk, xblk], axis=0)
            x_bds.append(jnp.where(pair_diag, xblk2, 0.0).astype(BF16))
        for p in pairs:
            y_scr[r, p * LANES:(p + 1) * LANES] = _dot(ms[p], x_bds[p])

    for c in range(n_chunks):
        r = slice(c * CHUNK, (c + 1) * CHUNK)
        groups = range(SSM_GROUPS)
        gsl = [slice(g * SSM_GROUP_W, (g + 1) * SSM_GROUP_W) for g in groups]
        bcs, ccs = zip(*[bc_cc(r, g) for g in groups])
        s_old = [s_t[:, gsl[g]] for g in groups]
        y_inter = [_dot(ccs[g], s_old[g].astype(BF16)) for g in groups]
        s_add = [_dot_tn(bcs[g], xdec[r, gsl[g]]) for g in groups]
        for g in groups:
            s_t[:, gsl[g]] = s_old[g] * chunk_dec[g][c:c + 1, :] + s_add[g]
            y_scr[r, gsl[g]] = y_scr[r, gsl[g]] + y_inter[g] * eacs[r, gsl[g]]

    for g in range(SSM_GROUPS):
        gs = slice(g * SSM_GROUP_W, (g + 1) * SSM_GROUP_W)
        yg = (y_scr[:, gs] + xbc[:, gs] * dskip_ref[:, gs]) * _silu(pm_ref[:, gs])
        yg = yg * lax.rsqrt(jnp.mean(yg * yg, axis=-1, keepdims=True) + EPS)
        y_ref[:, gs] = (yg * nw_ref[:, gs]).astype(y_ref.dtype)

    @pl.when(t == n_tiles - 1)
    def _():
        for kblk in range(SSM_DINNER // LANES):
            ks = slice(kblk * LANES, (kblk + 1) * LANES)
            st_ref[0, ks, :] = s_t[:, ks].T


def _ssm_core_prompt(pm, pt, cw_t, cbias, alog, dtb, dskip, nw, batch, seq):
    n_tiles = seq // SEQ_TILE
    row = lambda b, t: (b * n_tiles + t, 0)
    return pl.pallas_call(
        functools.partial(_ssm_core_kernel, n_tiles=n_tiles),
        grid=(batch, n_tiles),
        in_specs=[pl.BlockSpec((SEQ_TILE, SSM_MAIN), row),
                  pl.BlockSpec((SEQ_TILE, LANES), row),
                  _resident((CONV_W, SSM_CONV_CH)), _resident((1, SSM_CONV_CH)),
                  _resident((1, LANES)), _resident((1, LANES)),
                  _resident((1, SSM_DINNER)), _resident((1, SSM_DINNER))],
        out_specs=[pl.BlockSpec((SEQ_TILE, SSM_DINNER), row),
                   pl.BlockSpec((1, SSM_DINNER, SSM_DSTATE), lambda b, t: (b, 0, 0)),
                   pl.BlockSpec((1, SUBLANES, SSM_CONV_CH), lambda b, t: (b, 0, 0))],
        out_shape=[jax.ShapeDtypeStruct((batch * seq, SSM_DINNER), BF16),
                   jax.ShapeDtypeStruct((batch, SSM_DINNER, SSM_DSTATE), F32),
                   jax.ShapeDtypeStruct((batch, SUBLANES, SSM_CONV_CH), F32)],
        scratch_shapes=[pltpu.VMEM((SSM_DSTATE, SSM_DINNER), F32),
                        pltpu.VMEM((SEQ_TILE + SUBLANES, SSM_CONV_CH), F32),
                        pltpu.VMEM((SEQ_TILE, SSM_CONV_CH), F32),
                        pltpu.VMEM((SEQ_TILE, SSM_DINNER), F32),
                        pltpu.VMEM((SEQ_TILE, SSM_DINNER), BF16),
                        pltpu.VMEM((SEQ_TILE, SSM_DINNER), F32),
                        pltpu.VMEM((SEQ_TILE, SSM_DINNER), F32),
                        pltpu.VMEM((SEQ_TILE, SSM_DINNER), F32)],
        compiler_params=_params(2),
        name="ssm_core_prompt",
    )(pm, pt, cw_t, cbias, alog, dtb, dskip, nw)


def _hyb_step_kernel(pm_ref, pt_ref, conv_ref, rs_in, gs_in, cos_ref, sin_ref, cw_ref, alog_ref,
                     dtb_ref, gnw_ref, gnn_ref, mix_ref, conv_out, rs_out, gs_out, o_scr):
    bb = STEP_BATCH
    hd = HEAD_DIM
    ch = GDN_CONV_CH
    cos = cos_ref[...]
    sin = sin_ref[...]
    u = pm_ref[:, 2048:2048 + ch]
    c0 = conv_ref[:, 0:ch]
    c1 = conv_ref[:, ch:2 * ch]
    c2 = conv_ref[:, 2 * ch:3 * ch]
    conv = c0 * cw_ref[0:1, :]
    conv = conv + c1 * cw_ref[1:2, :]
    conv = conv + c2 * cw_ref[2:3, :]
    conv = conv + u * cw_ref[3:4, :]
    qkv = _silu(conv)
    conv_out[:, 0:ch] = c1
    conv_out[:, ch:2 * ch] = c2
    conv_out[:, 2 * ch:3 * ch] = u
    tail = pt_ref[...]
    g_all = -jnp.exp(alog_ref[...]) * _softplus(tail + dtb_ref[...])
    beta_all = _sigmoid(tail)

    for h in range(N_HEADS):
        q = _rotary(pm_ref[:, h * hd:(h + 1) * hd], cos, sin)
        k = _rotary(pm_ref[:, 512 + h * hd:512 + (h + 1) * hd], cos, sin) * (hd ** -0.5)
        v = pm_ref[:, 1024 + h * hd:1024 + (h + 1) * hd]
        k_cols = _columns(k)
        gamma = math.exp(LOG_GAMMA[h])
        for b in range(bb):
            s = rs_in[b, h] * gamma + k_cols[:, b:b + 1] * v[b:b + 1, :]
            rs_out[b, h] = s
            o_scr[b:b + 1, h * hd:(h + 1) * hd] = _dot(q, s)[b:b + 1, :]
        gq = _l2norm(qkv[:, h * hd:(h + 1) * hd]) * (hd ** -0.5)
        gk = _l2norm(qkv[:, 512 + h * hd:512 + (h + 1) * hd])
        gv = qkv[:, 1024 + h * hd:1024 + (h + 1) * hd]
        k_cols = _columns(gk)
        decay = jnp.broadcast_to(jnp.exp(g_all[:, h:h + 1]), (bb, hd))
        beta = jnp.broadcast_to(beta_all[:, 4 + h:5 + h], (bb, hd))
        for b in range(bb):
            s = gs_in[b, h] * decay[b:b + 1, :]
            delta = (gv[b:b + 1, :] - _dot(gk, s)[b:b + 1, :]) * beta[b:b + 1, :]
            s = s + k_cols[:, b:b + 1] * delta
            gs_out[b, h] = s
            o_scr[b:b + 1, 512 + h * hd:512 + (h + 1) * hd] = _dot(gq, s)[b:b + 1, :]

    for h in range(N_HEADS):
        o = o_scr[:, h * hd:(h + 1) * hd]
        mu = jnp.mean(o, axis=-1, keepdims=True)
        var = jnp.mean(jnp.square(o - mu), axis=-1, keepdims=True)
        ro = (o - mu) * lax.rsqrt(var + EPS) * gnw_ref[:, h * hd:(h + 1) * hd]
        mix_ref[:, h * hd:(h + 1) * hd] = _silu(pm_ref[:, 1536 + h * hd:1536 + (h + 1) * hd]) * ro
        go = o_scr[:, 512 + h * hd:512 + (h + 1) * hd]
        go = go * lax.rsqrt(jnp.mean(go * go, axis=-1, keepdims=True) + EPS) * gnn_ref[...]
        mix_ref[:, 512 + h * hd:512 + (h + 1) * hd] = go * _silu(pm_ref[:, 3584 + h * hd:3584 + (h + 1) * hd])


def _layer_chained_outputs(state_arrays, layer, partial):
    shapes = [jax.ShapeDtypeStruct(s.shape, s.dtype) for s in state_arrays]
    if layer == 0:
        return [], [], shapes
    return list(partial), [pl.BlockSpec(memory_space=pl.ANY)] * len(partial), shapes


def _hyb_core_step(pm, pt, conv_state, ret_all, gdn_all, layer, partial,
                   cos, sin, cw_t, alog, dtb, gnw, gnn):
    batch = pm.shape[0]
    bb = STEP_BATCH
    row = lambda i: (i, 0)
    state = pl.BlockSpec((None, bb, N_HEADS, HEAD_DIM, HEAD_DIM), lambda i: (layer, i, 0, 0, 0))
    extra, extra_specs, state_shapes = _layer_chained_outputs((ret_all, gdn_all), layer, partial)
    n_in = 12
    aliases = {n_in + k: 2 + k for k in range(len(extra))}

    def body(*refs):
        _hyb_step_kernel(*refs[:n_in], *refs[n_in + len(extra):])

    return pl.pallas_call(
        body,
        grid=(batch // bb,),
        in_specs=[pl.BlockSpec((bb, HYB_MAIN), row), pl.BlockSpec((bb, LANES), row),
                  pl.BlockSpec((bb, 3 * GDN_CONV_CH), row), state, state,
                  _resident((1, LANES)), _resident((1, LANES)),
                  _resident((CONV_W, GDN_CONV_CH)), _resident((1, LANES)), _resident((1, LANES)),
                  _resident((1, 512)), _resident((1, LANES))] + extra_specs,
        out_specs=[pl.BlockSpec((bb, D_MODEL), row), pl.BlockSpec((bb, 3 * GDN_CONV_CH), row),
                   state, state],
        out_shape=[jax.ShapeDtypeStruct((batch, D_MODEL), F32),
                   jax.ShapeDtypeStruct((batch, 3 * GDN_CONV_CH), F32)] + state_shapes,
        scratch_shapes=[pltpu.VMEM((bb, D_MODEL), F32)],
        input_output_aliases=aliases,
        compiler_params=_params(1),
        name="hyb_core_step",
    )(pm, pt, conv_state, ret_all, gdn_all, cos, sin, cw_t, alog, dtb, gnw, gnn, *extra)


def _ssm_step_kernel(pm_ref, pt_ref, conv_ref, s_in, cw_ref, cbias_ref, alog_ref, dtb_ref,
                     dskip_ref, nw_ref, y_ref, conv_out, s_out, y_scr):
    bb = STEP_BATCH
    ch = SSM_CONV_CH
    u = pm_ref[:, SSM_DINNER:SSM_MAIN]
    c0 = conv_ref[:, 0:ch]
    c1 = conv_ref[:, ch:2 * ch]
    c2 = conv_ref[:, 2 * ch:3 * ch]
    conv = c0 * cw_ref[0:1, :]
    conv = conv + c1 * cw_ref[1:2, :]
    conv = conv + c2 * cw_ref[2:3, :]
    conv = conv + u * cw_ref[3:4, :]
    xbc = _silu(conv + cbias_ref[...])
    conv_out[:, 0:ch] = c1
    conv_out[:, ch:2 * ch] = c2
    conv_out[:, 2 * ch:3 * ch] = u
    xs = xbc[:, 0:SSM_DINNER]
    expand = _head_expand_matrix()
    dt = _softplus(pt_ref[...] + dtb_ref[...])
    a = dt * (-jnp.exp(alog_ref[...]))
    x_dt = xs * _expand_heads(dt, expand)
    decay = _expand_heads(jnp.exp(a), expand)
    for kblk in range(SSM_DINNER // LANES):
        ks = slice(kblk * LANES, (kblk + 1) * LANES)
        g = kblk // 4
        x_cols = _columns(x_dt[:, ks])
        d_cols = _columns(decay[:, ks])
        bm = xbc[:, SSM_DINNER + g * LANES:SSM_DINNER + (g + 1) * LANES]
        cm = xbc[:, SSM_DINNER + 512 + g * LANES:SSM_DINNER + 512 + (g + 1) * LANES]
        for b in range(bb):
            s = s_in[b, ks, :] * d_cols[:, b:b + 1] + x_cols[:, b:b + 1] * bm[b:b + 1, :]
            s_out[b, ks, :] = s
            y_scr[b:b + 1, ks] = _dot_nt(cm, s)[b:b + 1, :]
    y = y_scr[...] + xs * dskip_ref[...]
    _group_rmsnorm_gate(y, pm_ref[:, 0:SSM_DINNER], nw_ref, y_ref, slice(None))


def _ssm_core_step(pm, pt, conv_state, ssm_all, layer, partial, cw_t, cbias, alog, dtb, dskip, nw):
    batch = pm.shape[0]
    bb = STEP_BATCH
    row = lambda i: (i, 0)
    state = pl.BlockSpec((None, bb, SSM_DINNER, SSM_DSTATE), lambda i: (layer, i, 0, 0))
    extra, extra_specs, state_shapes = _layer_chained_outputs((ssm_all,), layer, partial)
    n_in = 10
    aliases = {n_in + k: 2 + k for k in range(len(extra))}

    def body(*refs):
        _ssm_step_kernel(*refs[:n_in], *refs[n_in + len(extra):])

    return pl.pallas_call(
        body,
        grid=(batch // bb,),
        in_specs=[pl.BlockSpec((bb, SSM_MAIN), row), pl.BlockSpec((bb, LANES), row),
                  pl.BlockSpec((bb, 3 * SSM_CONV_CH), row), state,
                  _resident((CONV_W, SSM_CONV_CH)), _resident((1, SSM_CONV_CH)),
                  _resident((1, LANES)), _resident((1, LANES)),
                  _resident((1, SSM_DINNER)), _resident((1, SSM_DINNER))] + extra_specs,
        out_specs=[pl.BlockSpec((bb, SSM_DINNER), row), pl.BlockSpec((bb, 3 * SSM_CONV_CH), row),
                   state],
        out_shape=[jax.ShapeDtypeStruct((batch, SSM_DINNER), F32),
                   jax.ShapeDtypeStruct((batch, 3 * SSM_CONV_CH), F32)] + state_shapes,
        scratch_shapes=[pltpu.VMEM((bb, SSM_DINNER), F32)],
        input_output_aliases=aliases,
        compiler_params=_params(1),
        name="ssm_core_step",
    )(pm, pt, conv_state, ssm_all, cw_t, cbias, alog, dtb, dskip, nw, *extra)


def _rope_tables(pos):
    inv_freq = 1.0 / (ROPE_BASE ** jnp.linspace(0.0, 1.0, HEAD_DIM // 2, dtype=F32))
    ang = pos.astype(F32)[:, None] * inv_freq[None, :]
    cos = jnp.repeat(jnp.cos(ang), 2, axis=-1)
    sign = jnp.tile(jnp.array([-1.0, 1.0], F32), HEAD_DIM // 2)
    sin = jnp.repeat(jnp.sin(ang), 2, axis=-1) * sign
    return cos, sin


def _lane_row(v):
    return jnp.pad(v.astype(F32), (0, LANES - v.shape[0])).reshape(1, LANES)


def kernel(x_prompt, x_sample, state_ret, state_gdn, state_gdn_conv, state_ssm, state_ssm_conv,
           norm_mix_w, norm_ffn_w, norm_final_w, hyb_w_in, hyb_w_out, ret_gn_w, gdn_conv_w,
           gdn_a_log, gdn_dt_bias, gdn_norm_w, ssm_w_in, ssm_conv_w, ssm_conv_b, ssm_dt_bias,
           ssm_a_log, ssm_d, ssm_norm_w, ssm_w_out, ffn_w_gate, ffn_w_up, ffn_w_down):
    depth = norm_mix_w.shape[0]
    bp, seq, d = x_prompt.shape
    bs = x_sample.shape[0]
    past_len = 16384

    hyb_in = hyb_w_in.astype(BF16)
    hyb_tail = jnp.pad(hyb_w_in[:, :, HYB_MAIN:], ((0, 0), (0, 0), (0, LANES - 2 * N_HEADS))).astype(BF16)
    ssm_in = ssm_w_in.astype(BF16)
    ssm_tail = jnp.pad(ssm_w_in[:, :, SSM_MAIN:], ((0, 0), (0, 0), (0, LANES - SSM_HEADS))).astype(BF16)
    hyb_out = hyb_w_out.astype(BF16)
    ssm_out = ssm_w_out.astype(BF16)
    w_gate = ffn_w_gate.astype(BF16)
    w_up = ffn_w_up.astype(BF16)
    w_down = ffn_w_down.astype(BF16)

    gdn_cw_t = jnp.swapaxes(gdn_conv_w, 1, 2)
    ssm_cw_t = jnp.swapaxes(ssm_conv_w, 1, 2)
    dskip = jnp.repeat(ssm_d, SSM_HEADDIM, axis=-1)
    cos_p, sin_p = _rope_tables(jnp.arange(seq))
    cos_s, sin_s = _rope_tables(jnp.arange(past_len, past_len + 1))

    ssm_state_rows = state_ssm.reshape(state_ssm.shape[0], bs, SSM_DINNER, SSM_DSTATE)

    def trunk(x, prompt):
        t = x.shape[0]
        ret_l, gdn_l, gconv_l, ssm_l, sconv_l = [], [], [], [], []
        hyb_states, ssm_states = (), ()
        for i in range(depth):
            j = i // 2
            if i % 2 == 0:
                pm, pt = _norm_proj(x, norm_mix_w[i], hyb_in, j, HYB_MAIN, hyb_tail[j])
                args = (gdn_cw_t[j], _lane_row(gdn_a_log[j]), _lane_row(gdn_dt_bias[j]),
                        ret_gn_w[j].reshape(1, 512), gdn_norm_w[j].reshape(1, LANES))
                if prompt:
                    mixed, rs, gs, cb = _hyb_core_prompt(pm, pt, cos_p, sin_p, *args, bp, seq)
                    cb = cb[:, SUBLANES - (CONV_W - 1):, :]
                    ret_l.append(rs)
                    gdn_l.append(gs)
                else:
                    mixed, cb, *hyb_states = _hyb_core_step(
                        pm, pt, state_gdn_conv[j].reshape(t, -1), state_ret, state_gdn, j,
                        hyb_states, cos_s, sin_s, *args)
                    cb = cb.reshape(t, CONV_W - 1, GDN_CONV_CH)
                gconv_l.append(cb)
                x = _out_proj(x, mixed, hyb_out, j)
            else:
                pm, pt = _norm_proj(x, norm_mix_w[i], ssm_in, j, SSM_MAIN, ssm_tail[j])
                args = (ssm_cw_t[j], ssm_conv_b[j].reshape(1, -1), _lane_row(ssm_a_log[j]),
                        _lane_row(ssm_dt_bias[j]), dskip[j].reshape(1, -1),
                        ssm_norm_w[j].reshape(1, -1))
                if prompt:
                    y, ss, cb = _ssm_core_prompt(pm, pt, *args, bp, seq)
                    cb = cb[:, SUBLANES - (CONV_W - 1):, :]
                    ssm_l.append(ss.reshape(-1, SSM_HEADS, SSM_HEADDIM, SSM_DSTATE))
                else:
                    y, cb, *ssm_states = _ssm_core_step(
                        pm, pt, state_ssm_conv[j].reshape(t, -1), ssm_state_rows, j, ssm_states, *args)
                    cb = cb.reshape(t, CONV_W - 1, SSM_CONV_CH)
                sconv_l.append(cb)
                x = _out_proj(x, y, ssm_out, j)
            x = _ffn(x, norm_ffn_w[i], w_gate, w_up, w_down, i, norm_final_w,
                     final_norm=(i == depth - 1))
        if prompt:
            ret_new, gdn_new, ssm_new = jnp.stack(ret_l), jnp.stack(gdn_l), jnp.stack(ssm_l)
        else:
            ret_new, gdn_new = hyb_states
            ssm_new = ssm_states[0].reshape(state_ssm.shape)
        return x, ret_new, gdn_new, jnp.stack(gconv_l), ssm_new, jnp.stack(sconv_l)

    yp, ret_p, gdn_p, gconv_p, ssm_p, sconv_p = trunk(x_prompt.reshape(bp * seq, d), True)
    ys, ret_s, gdn_s, gconv_s, ssm_s, sconv_s = trunk(x_sample.reshape(bs, d), False)
    return (yp.reshape(bp, seq, d), ys.reshape(bs, 1, d),
            ret_p, gdn_p, gconv_p, ssm_p, sconv_p,
            ret_s, gdn_s, gconv_s, ssm_s, sconv_s)
```

```python
import functools
import math

import jax
import jax.numpy as jnp
from jax import lax
from jax.experimental import pallas as pl
from jax.experimental.pallas import tpu as pltpu

F32 = jnp.float32
BF16 = jnp.bfloat16

D_MODEL = 1024
N_HEADS = 4
HEAD_DIM = 128
HYB_MAIN = 4096
GDN_CONV_CH = 1536
SSM_DINNER = 2048
SSM_HEADS = 32
SSM_HEADDIM = 64
SSM_GROUPS = 4
SSM_DSTATE = 128
SSM_GROUP_W = SSM_DINNER // SSM_GROUPS
SSM_CONV_CH = 3072
SSM_MAIN = 5120
D_FF = 2816
FF_CHUNK = 256
CONV_W = 4
EPS = 1e-6
ROPE_BASE = 10000.0

LANES = 128
SUBLANES = 8
VMEM_LIMIT = 56 * 1024 * 1024

PROJ_PIECE_LANES = 256
SEQ_TILE = 256
CHUNK = 64
STEP_BATCH = 8
LOG_GAMMA = tuple(math.log1p(-2.0 ** (-5.0 - h)) for h in range(N_HEADS))

_NT = (((1,), (1,)), ((), ()))
_TN = (((0,), (0,)), ((), ()))


def _dot(a, b):
    return jnp.dot(a, b, preferred_element_type=F32)


def _dot_nt(a, b):
    return lax.dot_general(a, b, _NT, preferred_element_type=F32)


def _dot_tn(a, b):
    return lax.dot_general(a, b, _TN, preferred_element_type=F32)


def _dot_exact(a, b):
    return jnp.dot(a, b, preferred_element_type=F32, precision=lax.Precision.HIGHEST)


def _silu(x):
    return x / (1.0 + jnp.exp(-x))


def _sigmoid(x):
    return 1.0 / (1.0 + jnp.exp(-x))


def _softplus(x):
    return jnp.maximum(x, 0.0) + jnp.log1p(jnp.exp(-jnp.abs(x)))


def _rmsnorm(x, w):
    return x * lax.rsqrt(jnp.mean(x * x, axis=-1, keepdims=True) + EPS) * w


def _resident(shape):
    nd = len(shape)
    return pl.BlockSpec(shape, lambda *_: (0,) * nd, pipeline_mode=pl.Buffered(1))


def _resident_layer(shape, layer):
    nd = len(shape)
    return pl.BlockSpec((None,) + tuple(shape), lambda *_: (layer,) + (0,) * nd,
                        pipeline_mode=pl.Buffered(1))


def _params(n_axes):
    return pltpu.CompilerParams(dimension_semantics=("arbitrary",) * n_axes,
                                vmem_limit_bytes=VMEM_LIMIT)


def _row_tile(t):
    return 512 if t % 512 == 0 else t


def _norm_proj_kernel(x_ref, nw_ref, w_ref, wt_ref, o_ref, ot_ref, *, n_chunk):
    h = _rmsnorm(x_ref[...], nw_ref[...]).astype(BF16)
    n = w_ref.shape[1]
    for c in range(n // n_chunk):
        sl = slice(c * n_chunk, (c + 1) * n_chunk)
        o_ref[:, sl] = _dot(h, w_ref[:, sl])
    ot_ref[...] = _dot(h, wt_ref[...])


def _norm_proj(x, nw, w_all, layer, n, w_tail):
    t, d = x.shape
    tm = _row_tile(t)
    return pl.pallas_call(
        functools.partial(_norm_proj_kernel, n_chunk=512),
        grid=(t // tm,),
        in_specs=[pl.BlockSpec((tm, d), lambda i: (i, 0)),
                  _resident((1, d)), _resident_layer((d, n), layer), _resident((d, LANES))],
        out_specs=[pl.BlockSpec((tm, n), lambda i: (i, 0)),
                   pl.BlockSpec((tm, LANES), lambda i: (i, 0))],
        out_shape=[jax.ShapeDtypeStruct((t, n), F32), jax.ShapeDtypeStruct((t, LANES), F32)],
        compiler_params=_params(1),
        name="norm_proj",
    )(x, nw.reshape(1, d), w_all, w_tail)


def _causal_conv_block(u, xx, cw_ref, cs):
    n = u.shape[0]
    xx[SUBLANES:SUBLANES + n, cs] = u
    ext = xx[:, cs]
    acc = ext * cw_ref[0:1, cs]
    for tap in range(1, CONV_W):
        acc = pltpu.roll(acc, 1, 0) + ext * cw_ref[tap:tap + 1, cs]
    return acc[SUBLANES:, :]


def _prompt_proj_body(x_ref, nw_ref, w_ref, wt_ref, o_ref, ot_ref, cb_ref, xx, acc_scr, tiles_per_seq,
                      epilogue):
    t = lax.rem(pl.program_id(0), tiles_per_seq)
    tm = x_ref.shape[0]
    piece = PROJ_PIECE_LANES
    per_piece = piece // LANES

    @pl.when(t == 0)
    def _():
        xx[0:SUBLANES, :] = jnp.zeros((SUBLANES, xx.shape[1]), F32)

    h = _rmsnorm(x_ref[...], nw_ref[...]).astype(BF16)
    n_pieces = w_ref.shape[1] // piece
    acc_scr[0] = _dot(h, w_ref[:, 0:piece])
    for p in range(n_pieces):
        if p + 1 < n_pieces:
            acc_scr[(p + 1) % 2] = _dot(h, w_ref[:, (p + 1) * piece:(p + 2) * piece])
        for j in range(per_piece):
            blk = p * per_piece + j
            o_ref[:, blk * LANES:(blk + 1) * LANES] = epilogue(
                blk // 4, blk % 4, acc_scr[p % 2, :, j * LANES:(j + 1) * LANES])
    ot_ref[...] = _dot(h, wt_ref[...])
    xx[0:SUBLANES, :] = xx[tm:tm + SUBLANES, :]

    @pl.when(t == tiles_per_seq - 1)
    def _():
        cb_ref[0] = xx[tm:tm + SUBLANES, :]


def _hyb_proj_kernel(x_ref, nw_ref, w_ref, wt_ref, cos_ref, sin_ref, cw_ref, o_ref, ot_ref, cb_ref,
                     xx, acc_scr, *, tiles_per_seq):
    cos = cos_ref[...]
    sin = sin_ref[...]
    scale = HEAD_DIM ** -0.5

    def epilogue(c, j, blk):
        if c == 0:
            return _rotary(blk, cos, sin)
        if c == 1:
            return _rotary(blk, cos, sin) * scale
        if c == 2:
            return blk
        if c in (3, 7):
            return _silu(blk)
        ch = (c - 4) * 4 + j
        act = _silu(_causal_conv_block(blk, xx, cw_ref, slice(ch * LANES, (ch + 1) * LANES)))
        if c == 4:
            return _l2norm(act) * scale
        if c == 5:
            return _l2norm(act)
        return act

    _prompt_proj_body(x_ref, nw_ref, w_ref, wt_ref, o_ref, ot_ref, cb_ref, xx, acc_scr, tiles_per_seq,
                      epilogue)


def _ssm_proj_kernel(x_ref, nw_ref, w_ref, wt_ref, cw_ref, cbias_ref, o_ref, ot_ref, cb_ref, xx,
                     acc_scr, *, tiles_per_seq):
    def epilogue(c, j, blk):
        if c < 4:
            return _silu(blk)
        ch = (c - 4) * 4 + j
        cs = slice(ch * LANES, (ch + 1) * LANES)
        return _silu(_causal_conv_block(blk, xx, cw_ref, cs) + cbias_ref[:, cs])

    _prompt_proj_body(x_ref, nw_ref, w_ref, wt_ref, o_ref, ot_ref, cb_ref, xx, acc_scr, tiles_per_seq,
                      epilogue)


def _prompt_proj(kernel_fn, x, nw, w_all, layer, n, w_tail, conv_ch, seq, extra, extra_specs):
    t, d = x.shape
    tm = _row_tile(t)
    tiles_per_seq = seq // tm
    return pl.pallas_call(
        functools.partial(kernel_fn, tiles_per_seq=tiles_per_seq),
        grid=(t // tm,),
        in_specs=[pl.BlockSpec((tm, d), lambda i: (i, 0)),
                  _resident((1, d)), _resident_layer((d, n), layer), _resident((d, LANES))] + extra_specs,
        out_specs=[pl.BlockSpec((tm, n), lambda i: (i, 0)),
                   pl.BlockSpec((tm, LANES), lambda i: (i, 0)),
                   pl.BlockSpec((1, SUBLANES, conv_ch), lambda i: (i // tiles_per_seq, 0, 0))],
        out_shape=[jax.ShapeDtypeStruct((t, n), F32), jax.ShapeDtypeStruct((t, LANES), F32),
                   jax.ShapeDtypeStruct((t // seq, SUBLANES, conv_ch), F32)],
        scratch_shapes=[pltpu.VMEM((tm + SUBLANES, conv_ch), F32),
                        pltpu.VMEM((2, tm, PROJ_PIECE_LANES), F32)],
        compiler_params=_params(1),
        name=kernel_fn.__name__.strip("_"),
    )(x, nw.reshape(1, d), w_all, w_tail, *extra)


def _ffn_kernel(x_ref, a_ref, wo_ref, nw_ref, wg_ref, wu_ref, wd_ref, fw_ref, o_ref, act_ref, *,
                final_norm):
    x = x_ref[...] + _dot(a_ref[...].astype(BF16), wo_ref[...])
    h = _rmsnorm(x, nw_ref[...]).astype(BF16)
    for c in range(D_FF // FF_CHUNK):
        sl = slice(c * FF_CHUNK, (c + 1) * FF_CHUNK)
        gate = _dot(h, wg_ref[:, sl])
        up = _dot(h, wu_ref[:, sl])
        act_ref[:, sl] = (_silu(gate) * up).astype(BF16)
    y = x + _dot(act_ref[...], wd_ref[...])
    if final_norm:
        y = _rmsnorm(y, fw_ref[...])
    o_ref[...] = y


def _mix_ffn(x, a, w_out, out_layer, nw, wg, wu, wd, layer, final_w, final_norm):
    t, d = x.shape
    k = a.shape[1]
    tm = _row_tile(t)
    return pl.pallas_call(
        functools.partial(_ffn_kernel, final_norm=final_norm),
        grid=(t // tm,),
        in_specs=[pl.BlockSpec((tm, d), lambda i: (i, 0)),
                  pl.BlockSpec((tm, k), lambda i: (i, 0)),
                  _resident_layer((k, d), out_layer),
                  _resident((1, d)), _resident_layer((d, D_FF), layer),
                  _resident_layer((d, D_FF), layer), _resident_layer((D_FF, d), layer),
                  _resident((1, d))],
        out_specs=pl.BlockSpec((tm, d), lambda i: (i, 0)),
        out_shape=jax.ShapeDtypeStruct((t, d), F32),
        scratch_shapes=[pltpu.VMEM((tm, D_FF), BF16)],
        compiler_params=_params(1),
        name="mix_ffn",
    )(x, a, w_out, nw.reshape(1, d), wg, wu, wd, final_w.reshape(1, d))


def _rotary(x, cos, sin_signed):
    lane = lax.broadcasted_iota(jnp.int32, x.shape, 1)
    partner = jnp.where((lane & 1) == 0, pltpu.roll(x, LANES - 1, 1), pltpu.roll(x, 1, 1))
    return x * cos + partner * sin_signed


def _l2norm(x):
    return x * lax.rsqrt(jnp.sum(x * x, axis=-1, keepdims=True) + EPS)


def _split3(x):
    hi = x.astype(BF16)
    r1 = x - hi.astype(F32)
    mid = r1.astype(BF16)
    lo = (r1 - mid.astype(F32)).astype(BF16)
    return hi, mid, lo


def _split3_lanes(x):
    return jnp.concatenate(_split3(x), axis=1)


def _expand_heads(x, expand_mat):
    return _dot(_split3_lanes(x), expand_mat)


def _head_expand_matrix():
    row = lax.broadcasted_iota(jnp.int32, (3 * LANES, SSM_DINNER), 0)
    col = lax.broadcasted_iota(jnp.int32, (3 * LANES, SSM_DINNER), 1)
    return jnp.where((row & (LANES - 1)) == (col >> 6), 1.0, 0.0).astype(BF16)


def _chunk_sum_matrices(n):
    ii = lax.broadcasted_iota(jnp.int32, (n, n), 0)
    jj = lax.broadcasted_iota(jnp.int32, (n, n), 1)
    same = (ii >> 6) == (jj >> 6)
    prefix = jnp.where(same & (ii >= jj), 1.0, 0.0).astype(F32)
    total = jnp.where(same, 1.0, 0.0).astype(F32)
    return prefix, total


def _columns(rows):
    pad = jnp.zeros((LANES - rows.shape[0], LANES), F32)
    return jnp.concatenate([rows, pad], axis=0).T


def _unit_lower_inverses(lows, interleaved=()):
    interleaved = list(interleaved)
    n = lows[0].shape[0]
    ii = lax.broadcasted_iota(jnp.int32, (n, n), 0)
    jj = lax.broadcasted_iota(jnp.int32, (n, n), 1)
    eye = (ii == jj).astype(F32)
    pair = (ii >> 1) == (jj >> 1)
    ts = [eye - jnp.where(pair, low, 0.0) for low in lows]
    lows_b = [low.astype(BF16) for low in lows]
    for k in range(1, 6):
        join = ((ii >> (k + 1)) == (jj >> (k + 1))) & ((ii >> k) != (jj >> k))
        join_b = join.astype(F32).astype(BF16)
        ts_b = [t.astype(BF16) for t in ts]
        txs = [_dot(t_b, low_b * join_b).astype(BF16) for t_b, low_b in zip(ts_b, lows_b)]
        ts = [t - _dot(tx, t_b) for t, tx, t_b in zip(ts, txs, ts_b)]
        if interleaved:
            interleaved.pop(0)()
    for fn in interleaved:
        fn()
    return [t - eye for t in ts]


def _hyb_core_kernel(pm_ref, pt_ref, alog_ref, dtb_ref, gnw_ref, gnn_ref,
                     mix_ref, rs_ref, gs_ref, sret, sgdn, intra_scr, *, n_tiles):
    t = pl.program_id(1)
    tl = SEQ_TILE
    hd = HEAD_DIM

    @pl.when(t == 0)
    def _():
        sret[...] = jnp.zeros_like(sret)
        sgdn[...] = jnp.zeros_like(sgdn)

    @pl.when((t == 0) & (pl.program_id(0) == 0))
    def _():
        ii = lax.broadcasted_iota(jnp.int32, (tl, tl), 0)
        jj = lax.broadcasted_iota(jnp.int32, (tl, tl), 1)
        diff = (ii - jj).astype(F32)
        for h in range(N_HEADS):
            intra_scr[h] = jnp.where(diff >= 0.0, jnp.exp(LOG_GAMMA[h] * jnp.maximum(diff, 0.0)), 0.0)

    idx = lax.broadcasted_iota(jnp.int32, (tl, 1), 0).astype(F32)

    def retention_head(h):
        lg = LOG_GAMMA[h]
        q = pm_ref[:, h * hd:(h + 1) * hd]
        k = pm_ref[:, 512 + h * hd:512 + (h + 1) * hd]
        v = pm_ref[:, 1024 + h * hd:1024 + (h + 1) * hd]
        gate_act = pm_ref[:, 1536 + h * hd:1536 + (h + 1) * hd]
        q_dec = jnp.exp(lg * (idx + 1.0))
        k_dec = jnp.exp(lg * (tl - 1.0 - idx))
        s = sret[h]
        qb = q.astype(BF16)
        vb = v.astype(BF16)
        scores = _dot_nt(qb, k.astype(BF16)) * intra_scr[h]
        o = _dot(scores.astype(BF16), vb) + _dot(qb, s.astype(BF16)) * q_dec
        sret[h] = s * math.exp(lg * tl) + _dot_tn((k * k_dec).astype(BF16), vb)
        mu = jnp.mean(o, axis=-1, keepdims=True)
        var = jnp.mean(jnp.square(o - mu), axis=-1, keepdims=True)
        ro = (o - mu) * lax.rsqrt(var + EPS) * gnw_ref[:, h * hd:(h + 1) * hd]
        mix_ref[:, h * hd:(h + 1) * hd] = (gate_act * ro).astype(BF16)

    tail = pt_ref[...]
    g_all = -jnp.exp(alog_ref[...]) * _softplus(tail + dtb_ref[...])
    beta_all = _sigmoid(tail)
    prefix, total = _chunk_sum_matrices(tl)
    gc_all = _dot_exact(prefix, g_all)
    gl_all = _dot_exact(total, g_all)

    n = N_HEADS * CHUNK
    si = lax.broadcasted_iota(jnp.int32, (n, n), 0)
    sj = lax.broadcasted_iota(jnp.int32, (n, n), 1)
    same_head = (si >> 6) == (sj >> 6)
    causal = same_head & (si >= sj)
    strict = same_head & (si > sj)

    def stack(fn):
        return jnp.concatenate([fn(h) for h in range(N_HEADS)], axis=0)

    chunks = []
    for c in range(tl // CHUNK):
        r = slice(c * CHUNK, (c + 1) * CHUNK)
        q_st = stack(lambda h: pm_ref[r, 2048 + h * hd:2048 + (h + 1) * hd])
        k_st = stack(lambda h: pm_ref[r, 2560 + h * hd:2560 + (h + 1) * hd])
        v_st = stack(lambda h: pm_ref[r, 3072 + h * hd:3072 + (h + 1) * hd])
        beta = stack(lambda h: beta_all[r, 4 + h:5 + h])
        gc = stack(lambda h: gc_all[r, h:h + 1])
        gl = stack(lambda h: gl_all[r, h:h + 1])
        gc_col = jnp.broadcast_to(gc, (n, n))
        seg = gc_col - gc_col.T
        decay = jnp.where(causal, jnp.exp(jnp.where(causal, seg, 0.0)), 0.0)
        k_b16 = k_st.astype(BF16)
        qk = _dot_nt(jnp.concatenate([q_st.astype(BF16), k_b16], axis=0), k_b16)
        egc = jnp.exp(gc)
        chunks.append(dict(
            r=r, gl=gl,
            low=jnp.where(strict, qk[n:] * beta * decay, 0.0),
            rhs=jnp.concatenate([v_st * beta, k_st * (beta * egc)], axis=1),
            scores=(qk[:n] * decay).astype(BF16),
            q_in=(q_st * egc).astype(BF16),
            k_out=(k_st * jnp.exp(gl - gc)).astype(BF16)))
    t_offs = _unit_lower_inverses(
        [ch["low"] for ch in chunks],
        interleaved=[functools.partial(retention_head, h) for h in range(N_HEADS)])
    sols = [ch["rhs"] + _dot(t_off.astype(BF16), ch["rhs"].astype(BF16))
            for ch, t_off in zip(chunks, t_offs)]

    for ch, sol in zip(chunks, sols):
        r = ch["r"]
        w_b16 = sol[:, hd:].astype(BF16)
        v_new = []
        o_inter = []
        for h in range(N_HEADS):
            hr = slice(h * CHUNK, (h + 1) * CHUNK)
            lhs = jnp.concatenate([w_b16[hr], ch["q_in"][hr]], axis=0)
            prod = _dot(lhs, sgdn[h].astype(BF16))
            v_new.append(sol[hr, :hd] - prod[:CHUNK])
            o_inter.append(prod[CHUNK:])
        vn_b16 = jnp.concatenate(v_new, axis=0).astype(BF16)
        o_st = jnp.concatenate(o_inter, axis=0) + _dot(ch["scores"], vn_b16)
        for h in range(N_HEADS):
            hr = slice(h * CHUNK, (h + 1) * CHUNK)
            s_dec = jnp.exp(jnp.broadcast_to(ch["gl"][hr], (CHUNK, hd)))
            s_dec = jnp.concatenate([s_dec, s_dec], axis=0)
            sgdn[h] = sgdn[h] * s_dec + _dot_tn(ch["k_out"][hr], vn_b16[hr])
            o_h = o_st[hr]
            go = o_h * lax.rsqrt(jnp.mean(o_h * o_h, axis=-1, keepdims=True) + EPS) * gnn_ref[...]
            gz_act = pm_ref[r, 3584 + h * hd:3584 + (h + 1) * hd]
            mix_ref[r, 512 + h * hd:512 + (h + 1) * hd] = (go * gz_act).astype(BF16)

    @pl.when(t == n_tiles - 1)
    def _():
        rs_ref[0] = sret[...]
        gs_ref[0] = sgdn[...]


def _hyb_core_prompt(pm, pt, alog, dtb, gnw, gnn, batch, seq):
    n_tiles = seq // SEQ_TILE
    row = lambda b, t: (b * n_tiles + t, 0)
    state = pl.BlockSpec((1, N_HEADS, HEAD_DIM, HEAD_DIM), lambda b, t: (b, 0, 0, 0))
    return pl.pallas_call(
        functools.partial(_hyb_core_kernel, n_tiles=n_tiles),
        grid=(batch, n_tiles),
        in_specs=[pl.BlockSpec((SEQ_TILE, HYB_MAIN), row),
                  pl.BlockSpec((SEQ_TILE, LANES), row),
                  _resident((1, LANES)), _resident((1, LANES)),
                  _resident((1, 512)), _resident((1, LANES))],
        out_specs=[pl.BlockSpec((SEQ_TILE, D_MODEL), row), state, state],
        out_shape=[jax.ShapeDtypeStruct((batch * seq, D_MODEL), BF16),
                   jax.ShapeDtypeStruct((batch, N_HEADS, HEAD_DIM, HEAD_DIM), F32),
                   jax.ShapeDtypeStruct((batch, N_HEADS, HEAD_DIM, HEAD_DIM), F32)],
        scratch_shapes=[pltpu.VMEM((N_HEADS, HEAD_DIM, HEAD_DIM), F32),
                        pltpu.VMEM((N_HEADS, HEAD_DIM, HEAD_DIM), F32),
                        pltpu.VMEM((N_HEADS, SEQ_TILE, SEQ_TILE), F32)],
        compiler_params=_params(2),
        name="hyb_core_prompt",
    )(pm, pt, alog, dtb, gnw, gnn)


def _group_rmsnorm_gate(y, z, nw_ref, o_ref, rows):
    for g in range(SSM_GROUPS):
        sl = slice(g * SSM_GROUP_W, (g + 1) * SSM_GROUP_W)
        yg = y[:, sl] * _silu(z[:, sl])
        yg = yg * lax.rsqrt(jnp.mean(yg * yg, axis=-1, keepdims=True) + EPS)
        o_ref[rows, sl] = (yg * nw_ref[:, sl]).astype(o_ref.dtype)


def _ssm_core_kernel(pm_ref, pt_ref, alog_ref, dtb_ref, dskip_ref, nw_ref,
                     y_ref, st_ref, s_t, xdt, xdec, eacs, acs_e, y_scr, *, n_tiles):
    t = pl.program_id(1)
    tl = SEQ_TILE
    xbc = pm_ref.at[:, SSM_DINNER:SSM_MAIN]

    @pl.when(t == 0)
    def _():
        s_t[...] = jnp.zeros_like(s_t)

    n_chunks = tl // CHUNK
    expand = _head_expand_matrix()
    dt = _softplus(pt_ref[...] + dtb_ref[...])
    a = dt * (-jnp.exp(alog_ref[...]))
    prefix, total = _chunk_sum_matrices(tl)
    acs = _dot_exact(prefix, a)
    atot = _dot_exact(total, a)
    dt_parts = _split3_lanes(dt)
    dec_parts = _split3_lanes(jnp.exp(atot - acs))
    eacs_parts = _split3_lanes(jnp.exp(acs))
    acs_parts = _split3_lanes(acs)
    chunk_rows = [atot[c * CHUNK:c * CHUNK + 1] for c in range(n_chunks)]
    chunk_rows.append(jnp.zeros((SUBLANES - n_chunks, LANES), F32))
    cdec_parts = _split3_lanes(jnp.exp(jnp.concatenate(chunk_rows, axis=0)))

    def expand_group(parts, g):
        return _dot(parts, expand[:, g * SSM_GROUP_W:(g + 1) * SSM_GROUP_W])

    chunk_dec = []
    for g in range(SSM_GROUPS):
        gs = slice(g * SSM_GROUP_W, (g + 1) * SSM_GROUP_W)
        x_dt = xbc[:, gs] * expand_group(dt_parts, g)
        xdt[:, gs] = x_dt
        xdec[:, gs] = (x_dt * expand_group(dec_parts, g)).astype(BF16)
        eacs[:, gs] = expand_group(eacs_parts, g)
        acs_e[:, gs] = expand_group(acs_parts, g)
        chunk_dec.append(expand_group(cdec_parts, g))

    lane = lax.broadcasted_iota(jnp.int32, (CHUNK, LANES), 1)
    row_i = lax.broadcasted_iota(jnp.int32, (CHUNK, LANES), 0)
    first_half = lane < SSM_HEADDIM
    causal2 = row_i >= (lane & (CHUNK - 1))
    lane2 = lax.broadcasted_iota(jnp.int32, (2 * CHUNK, LANES), 1)
    row2 = lax.broadcasted_iota(jnp.int32, (2 * CHUNK, LANES), 0)
    pair_diag = (row2 < CHUNK) == (lane2 < SSM_HEADDIM)

    def bc_cc(r, g):
        bc = xbc[r, SSM_DINNER + g * LANES:SSM_DINNER + (g + 1) * LANES]
        cc = xbc[r, SSM_DINNER + 512 + g * LANES:SSM_DINNER + 512 + (g + 1) * LANES]
        return bc.astype(BF16), cc.astype(BF16)

    for c in range(n_chunks):
        r = slice(c * CHUNK, (c + 1) * CHUNK)
        acs_c = acs[r]
        acs_t = jnp.concatenate([acs_c, acs_c], axis=0).T
        cb2 = []
        for g in range(SSM_GROUPS):
            bc_b16, cc_b16 = bc_cc(r, g)
            cb2.append(_dot_nt(cc_b16, jnp.concatenate([bc_b16, bc_b16], axis=0)))
        pairs = range(SSM_HEADS // 2)
        ms = []
        for p in pairs:
            ha = 2 * p
            rowv = jnp.where(first_half, acs_t[ha:ha + 1, :], acs_t[ha + 1:ha + 2, :])
            seg = acs_e[r, p * LANES:(p + 1) * LANES] - rowv
            lmat = jnp.where(causal2, jnp.exp(jnp.where(causal2, seg, 0.0)), 0.0)
            ms.append((cb2[p // 4] * lmat).astype(BF16))
        x_bds = []
        for p in pairs:
            xblk = xdt[r, p * LANES:(p + 1) * LANES]
            xblk2 = jnp.concatenate([xblk, xblk], axis=0)
            x_bds.append(jnp.where(pair_diag, xblk2, 0.0).astype(BF16))
        for p in pairs:
            y_scr[r, p * LANES:(p + 1) * LANES] = _dot(ms[p], x_bds[p])

    for c in range(n_chunks):
        r = slice(c * CHUNK, (c + 1) * CHUNK)
        groups = range(SSM_GROUPS)
        gsl = [slice(g * SSM_GROUP_W, (g + 1) * SSM_GROUP_W) for g in groups]
        bcs, ccs = zip(*[bc_cc(r, g) for g in groups])
        s_old = [s_t[:, gsl[g]] for g in groups]
        y_inter = [_dot(ccs[g], s_old[g].astype(BF16)) for g in groups]
        s_add = [_dot_tn(bcs[g], xdec[r, gsl[g]]) for g in groups]
        for g in groups:
            s_t[:, gsl[g]] = s_old[g] * chunk_dec[g][c:c + 1, :] + s_add[g]
            y_scr[r, gsl[g]] = y_scr[r, gsl[g]] + y_inter[g] * eacs[r, gsl[g]]

    for g in range(SSM_GROUPS):
        gs = slice(g * SSM_GROUP_W, (g + 1) * SSM_GROUP_W)
        yg = (y_scr[:, gs] + xbc[:, gs] * dskip_ref[:, gs]) * pm_ref[:, gs]
        yg = yg * lax.rsqrt(jnp.mean(yg * yg, axis=-1, keepdims=True) + EPS)
        y_ref[:, gs] = (yg * nw_ref[:, gs]).astype(y_ref.dtype)

    @pl.when(t == n_tiles - 1)
    def _():
        for kblk in range(SSM_DINNER // LANES):
            ks = slice(kblk * LANES, (kblk + 1) * LANES)
            st_ref[0, ks, :] = s_t[:, ks].T


def _ssm_core_prompt(pm, pt, alog, dtb, dskip, nw, batch, seq):
    n_tiles = seq // SEQ_TILE
    row = lambda b, t: (b * n_tiles + t, 0)
    return pl.pallas_call(
        functools.partial(_ssm_core_kernel, n_tiles=n_tiles),
        grid=(batch, n_tiles),
        in_specs=[pl.BlockSpec((SEQ_TILE, SSM_MAIN), row),
                  pl.BlockSpec((SEQ_TILE, LANES), row),
                  _resident((1, LANES)), _resident((1, LANES)),
                  _resident((1, SSM_DINNER)), _resident((1, SSM_DINNER))],
        out_specs=[pl.BlockSpec((SEQ_TILE, SSM_DINNER), row),
                   pl.BlockSpec((1, SSM_DINNER, SSM_DSTATE), lambda b, t: (b, 0, 0))],
        out_shape=[jax.ShapeDtypeStruct((batch * seq, SSM_DINNER), BF16),
                   jax.ShapeDtypeStruct((batch, SSM_DINNER, SSM_DSTATE), F32)],
        scratch_shapes=[pltpu.VMEM((SSM_DSTATE, SSM_DINNER), F32),
                        pltpu.VMEM((SEQ_TILE, SSM_DINNER), F32),
                        pltpu.VMEM((SEQ_TILE, SSM_DINNER), BF16),
                        pltpu.VMEM((SEQ_TILE, SSM_DINNER), F32),
                        pltpu.VMEM((SEQ_TILE, SSM_DINNER), F32),
                        pltpu.VMEM((SEQ_TILE, SSM_DINNER), F32)],
        compiler_params=_params(2),
        name="ssm_core_prompt",
    )(pm, pt, alog, dtb, dskip, nw)


def _hyb_step_kernel(pm_ref, pt_ref, conv_ref, rs_in, gs_in, cos_ref, sin_ref, cw_ref, alog_ref,
                     dtb_ref, gnw_ref, gnn_ref, mix_ref, conv_out, rs_out, gs_out, o_scr):
    bb = STEP_BATCH
    hd = HEAD_DIM
    ch = GDN_CONV_CH
    cos = cos_ref[...]
    sin = sin_ref[...]
    u = pm_ref[:, 2048:2048 + ch]
    c0 = conv_ref[:, 0:ch]
    c1 = conv_ref[:, ch:2 * ch]
    c2 = conv_ref[:, 2 * ch:3 * ch]
    conv = c0 * cw_ref[0:1, :]
    conv = conv + c1 * cw_ref[1:2, :]
    conv = conv + c2 * cw_ref[2:3, :]
    conv = conv + u * cw_ref[3:4, :]
    qkv = _silu(conv)
    conv_out[:, 0:ch] = c1
    conv_out[:, ch:2 * ch] = c2
    conv_out[:, 2 * ch:3 * ch] = u
    tail = pt_ref[...]
    g_all = -jnp.exp(alog_ref[...]) * _softplus(tail + dtb_ref[...])
    beta_all = _sigmoid(tail)

    for h in range(N_HEADS):
        q = _rotary(pm_ref[:, h * hd:(h + 1) * hd], cos, sin)
        k = _rotary(pm_ref[:, 512 + h * hd:512 + (h + 1) * hd], cos, sin) * (hd ** -0.5)
        v = pm_ref[:, 1024 + h * hd:1024 + (h + 1) * hd]
        k_cols = _columns(k)
        gamma = math.exp(LOG_GAMMA[h])
        for b in range(bb):
            s = rs_in[b, h] * gamma + k_cols[:, b:b + 1] * v[b:b + 1, :]
            rs_out[b, h] = s
            o_scr[b:b + 1, h * hd:(h + 1) * hd] = _dot(q, s)[b:b + 1, :]
        gq = _l2norm(qkv[:, h * hd:(h + 1) * hd]) * (hd ** -0.5)
        gk = _l2norm(qkv[:, 512 + h * hd:512 + (h + 1) * hd])
        gv = qkv[:, 1024 + h * hd:1024 + (h + 1) * hd]
        k_cols = _columns(gk)
        decay = jnp.broadcast_to(jnp.exp(g_all[:, h:h + 1]), (bb, hd))
        beta = jnp.broadcast_to(beta_all[:, 4 + h:5 + h], (bb, hd))
        for b in range(bb):
            s = gs_in[b, h] * decay[b:b + 1, :]
            delta = (gv[b:b + 1, :] - _dot(gk, s)[b:b + 1, :]) * beta[b:b + 1, :]
            s = s + k_cols[:, b:b + 1] * delta
            gs_out[b, h] = s
            o_scr[b:b + 1, 512 + h * hd:512 + (h + 1) * hd] = _dot(gq, s)[b:b + 1, :]

    for h in range(N_HEADS):
        o = o_scr[:, h * hd:(h + 1) * hd]
        mu = jnp.mean(o, axis=-1, keepdims=True)
        var = jnp.mean(jnp.square(o - mu), axis=-1, keepdims=True)
        ro = (o - mu) * lax.rsqrt(var + EPS) * gnw_ref[:, h * hd:(h + 1) * hd]
        mix_ref[:, h * hd:(h + 1) * hd] = _silu(pm_ref[:, 1536 + h * hd:1536 + (h + 1) * hd]) * ro
        go = o_scr[:, 512 + h * hd:512 + (h + 1) * hd]
        go = go * lax.rsqrt(jnp.mean(go * go, axis=-1, keepdims=True) + EPS) * gnn_ref[...]
        mix_ref[:, 512 + h * hd:512 + (h + 1) * hd] = go * _silu(pm_ref[:, 3584 + h * hd:3584 + (h + 1) * hd])


def _layer_chained_outputs(state_arrays, layer, partial):
    shapes = [jax.ShapeDtypeStruct(s.shape, s.dtype) for s in state_arrays]
    if layer == 0:
        return [], [], shapes
    return list(partial), [pl.BlockSpec(memory_space=pl.ANY)] * len(partial), shapes


def _hyb_core_step(pm, pt, conv_state, ret_all, gdn_all, layer, partial,
                   cos, sin, cw_t, alog, dtb, gnw, gnn):
    batch = pm.shape[0]
    bb = STEP_BATCH
    row = lambda i: (i, 0)
    state = pl.BlockSpec((None, bb, N_HEADS, HEAD_DIM, HEAD_DIM), lambda i: (layer, i, 0, 0, 0))
    extra, extra_specs, state_shapes = _layer_chained_outputs((ret_all, gdn_all), layer, partial)
    n_in = 12
    aliases = {n_in + k: 2 + k for k in range(len(extra))}

    def body(*refs):
        _hyb_step_kernel(*refs[:n_in], *refs[n_in + len(extra):])

    return pl.pallas_call(
        body,
        grid=(batch // bb,),
        in_specs=[pl.BlockSpec((bb, HYB_MAIN), row), pl.BlockSpec((bb, LANES), row),
                  pl.BlockSpec((bb, 3 * GDN_CONV_CH), row), state, state,
                  _resident((1, LANES)), _resident((1, LANES)),
                  _resident((CONV_W, GDN_CONV_CH)), _resident((1, LANES)), _resident((1, LANES)),
                  _resident((1, 512)), _resident((1, LANES))] + extra_specs,
        out_specs=[pl.BlockSpec((bb, D_MODEL), row), pl.BlockSpec((bb, 3 * GDN_CONV_CH), row),
                   state, state],
        out_shape=[jax.ShapeDtypeStruct((batch, D_MODEL), F32),
                   jax.ShapeDtypeStruct((batch, 3 * GDN_CONV_CH), F32)] + state_shapes,
        scratch_shapes=[pltpu.VMEM((bb, D_MODEL), F32)],
        input_output_aliases=aliases,
        compiler_params=_params(1),
        name="hyb_core_step",
    )(pm, pt, conv_state, ret_all, gdn_all, cos, sin, cw_t, alog, dtb, gnw, gnn, *extra)


def _ssm_step_kernel(pm_ref, pt_ref, conv_ref, s_in, cw_ref, cbias_ref, alog_ref, dtb_ref,
                     dskip_ref, nw_ref, y_ref, conv_out, s_out, y_scr):
    bb = STEP_BATCH
    ch = SSM_CONV_CH
    u = pm_ref[:, SSM_DINNER:SSM_MAIN]
    c0 = conv_ref[:, 0:ch]
    c1 = conv_ref[:, ch:2 * ch]
    c2 = conv_ref[:, 2 * ch:3 * ch]
    conv = c0 * cw_ref[0:1, :]
    conv = conv + c1 * cw_ref[1:2, :]
    conv = conv + c2 * cw_ref[2:3, :]
    conv = conv + u * cw_ref[3:4, :]
    xbc = _silu(conv + cbias_ref[...])
    conv_out[:, 0:ch] = c1
    conv_out[:, ch:2 * ch] = c2
    conv_out[:, 2 * ch:3 * ch] = u
    xs = xbc[:, 0:SSM_DINNER]
    expand = _head_expand_matrix()
    dt = _softplus(pt_ref[...] + dtb_ref[...])
    a = dt * (-jnp.exp(alog_ref[...]))
    x_dt = xs * _expand_heads(dt, expand)
    decay = _expand_heads(jnp.exp(a), expand)
    for kblk in range(SSM_DINNER // LANES):
        ks = slice(kblk * LANES, (kblk + 1) * LANES)
        g = kblk // 4
        x_cols = _columns(x_dt[:, ks])
        d_cols = _columns(decay[:, ks])
        bm = xbc[:, SSM_DINNER + g * LANES:SSM_DINNER + (g + 1) * LANES]
        cm = xbc[:, SSM_DINNER + 512 + g * LANES:SSM_DINNER + 512 + (g + 1) * LANES]
        for b in range(bb):
            s = s_in[b, ks, :] * d_cols[:, b:b + 1] + x_cols[:, b:b + 1] * bm[b:b + 1, :]
            s_out[b, ks, :] = s
            y_scr[b:b + 1, ks] = _dot_nt(cm, s)[b:b + 1, :]
    y = y_scr[...] + xs * dskip_ref[...]
    _group_rmsnorm_gate(y, pm_ref[:, 0:SSM_DINNER], nw_ref, y_ref, slice(None))


def _ssm_core_step(pm, pt, conv_state, ssm_all, layer, partial, cw_t, cbias, alog, dtb, dskip, nw):
    batch = pm.shape[0]
    bb = STEP_BATCH
    row = lambda i: (i, 0)
    state = pl.BlockSpec((None, bb, SSM_DINNER, SSM_DSTATE), lambda i: (layer, i, 0, 0))
    extra, extra_specs, state_shapes = _layer_chained_outputs((ssm_all,), layer, partial)
    n_in = 10
    aliases = {n_in + k: 2 + k for k in range(len(extra))}

    def body(*refs):
        _ssm_step_kernel(*refs[:n_in], *refs[n_in + len(extra):])

    return pl.pallas_call(
        body,
        grid=(batch // bb,),
        in_specs=[pl.BlockSpec((bb, SSM_MAIN), row), pl.BlockSpec((bb, LANES), row),
                  pl.BlockSpec((bb, 3 * SSM_CONV_CH), row), state,
                  _resident((CONV_W, SSM_CONV_CH)), _resident((1, SSM_CONV_CH)),
                  _resident((1, LANES)), _resident((1, LANES)),
                  _resident((1, SSM_DINNER)), _resident((1, SSM_DINNER))] + extra_specs,
        out_specs=[pl.BlockSpec((bb, SSM_DINNER), row), pl.BlockSpec((bb, 3 * SSM_CONV_CH), row),
                   state],
        out_shape=[jax.ShapeDtypeStruct((batch, SSM_DINNER), F32),
                   jax.ShapeDtypeStruct((batch, 3 * SSM_CONV_CH), F32)] + state_shapes,
        scratch_shapes=[pltpu.VMEM((bb, SSM_DINNER), F32)],
        input_output_aliases=aliases,
        compiler_params=_params(1),
        name="ssm_core_step",
    )(pm, pt, conv_state, ssm_all, cw_t, cbias, alog, dtb, dskip, nw, *extra)


def _rope_tables(pos):
    inv_freq = 1.0 / (ROPE_BASE ** jnp.linspace(0.0, 1.0, HEAD_DIM // 2, dtype=F32))
    ang = pos.astype(F32)[:, None] * inv_freq[None, :]
    cos = jnp.repeat(jnp.cos(ang), 2, axis=-1)
    sign = jnp.tile(jnp.array([-1.0, 1.0], F32), HEAD_DIM // 2)
    sin = jnp.repeat(jnp.sin(ang), 2, axis=-1) * sign
    return cos, sin


def _lane_row(v):
    return jnp.pad(v.astype(F32), (0, LANES - v.shape[0])).reshape(1, LANES)


def kernel(x_prompt, x_sample, state_ret, state_gdn, state_gdn_conv, state_ssm, state_ssm_conv,
           norm_mix_w, norm_ffn_w, norm_final_w, hyb_w_in, hyb_w_out, ret_gn_w, gdn_conv_w,
           gdn_a_log, gdn_dt_bias, gdn_norm_w, ssm_w_in, ssm_conv_w, ssm_conv_b, ssm_dt_bias,
           ssm_a_log, ssm_d, ssm_norm_w, ssm_w_out, ffn_w_gate, ffn_w_up, ffn_w_down):
    depth = norm_mix_w.shape[0]
    bp, seq, d = x_prompt.shape
    bs = x_sample.shape[0]
    past_len = 16384

    hyb_in = hyb_w_in.astype(BF16)
    hyb_tail = jnp.pad(hyb_w_in[:, :, HYB_MAIN:], ((0, 0), (0, 0), (0, LANES - 2 * N_HEADS))).astype(BF16)
    ssm_in = ssm_w_in.astype(BF16)
    ssm_tail = jnp.pad(ssm_w_in[:, :, SSM_MAIN:], ((0, 0), (0, 0), (0, LANES - SSM_HEADS))).astype(BF16)
    hyb_out = hyb_w_out.astype(BF16)
    ssm_out = ssm_w_out.astype(BF16)
    w_gate = ffn_w_gate.astype(BF16)
    w_up = ffn_w_up.astype(BF16)
    w_down = ffn_w_down.astype(BF16)

    gdn_cw_t = jnp.swapaxes(gdn_conv_w, 1, 2)
    ssm_cw_t = jnp.swapaxes(ssm_conv_w, 1, 2)
    dskip = jnp.repeat(ssm_d, SSM_HEADDIM, axis=-1)
    cos_p, sin_p = _rope_tables(jnp.arange(seq))
    cos_s, sin_s = _rope_tables(jnp.arange(past_len, past_len + 1))

    ssm_state_rows = state_ssm.reshape(state_ssm.shape[0], bs, SSM_DINNER, SSM_DSTATE)

    def trunk(x, prompt):
        t = x.shape[0]
        ret_l, gdn_l, gconv_l, ssm_l, sconv_l = [], [], [], [], []
        hyb_states, ssm_states = (), ()
        for i in range(depth):
            j = i // 2
            if i % 2 == 0:
                args = (_lane_row(gdn_a_log[j]), _lane_row(gdn_dt_bias[j]),
                        ret_gn_w[j].reshape(1, 512), gdn_norm_w[j].reshape(1, LANES))
                if prompt:
                    rope = pl.BlockSpec((_row_tile(t), LANES), lambda r: (r % (seq // _row_tile(t)), 0))
                    pm, pt, cb = _prompt_proj(
                        _hyb_proj_kernel, x, norm_mix_w[i], hyb_in, j, HYB_MAIN, hyb_tail[j],
                        GDN_CONV_CH, seq, (cos_p, sin_p, gdn_cw_t[j]),
                        [rope, rope, _resident((CONV_W, GDN_CONV_CH))])
                    mixed, rs, gs = _hyb_core_prompt(pm, pt, *args, bp, seq)
                    cb = cb[:, SUBLANES - (CONV_W - 1):, :]
                    ret_l.append(rs)
                    gdn_l.append(gs)
                else:
                    pm, pt = _norm_proj(x, norm_mix_w[i], hyb_in, j, HYB_MAIN, hyb_tail[j])
                    mixed, cb, *hyb_states = _hyb_core_step(
                        pm, pt, state_gdn_conv[j].reshape(t, -1), state_ret, state_gdn, j,
                        hyb_states, cos_s, sin_s, gdn_cw_t[j], *args)
                    cb = cb.reshape(t, CONV_W - 1, GDN_CONV_CH)
                gconv_l.append(cb)
                w_mix = hyb_out
            else:
                args = (_lane_row(ssm_a_log[j]), _lane_row(ssm_dt_bias[j]), dskip[j].reshape(1, -1),
                        ssm_norm_w[j].reshape(1, -1))
                conv_args = (ssm_cw_t[j], ssm_conv_b[j].reshape(1, -1))
                if prompt:
                    pm, pt, cb = _prompt_proj(
                        _ssm_proj_kernel, x, norm_mix_w[i], ssm_in, j, SSM_MAIN, ssm_tail[j],
                        SSM_CONV_CH, seq, conv_args,
                        [_resident((CONV_W, SSM_CONV_CH)), _resident((1, SSM_CONV_CH))])
                    y, ss = _ssm_core_prompt(pm, pt, *args, bp, seq)
                    cb = cb[:, SUBLANES - (CONV_W - 1):, :]
                    ssm_l.append(ss.reshape(-1, SSM_HEADS, SSM_HEADDIM, SSM_DSTATE))
                else:
                    pm, pt = _norm_proj(x, norm_mix_w[i], ssm_in, j, SSM_MAIN, ssm_tail[j])
                    y, cb, *ssm_states = _ssm_core_step(
                        pm, pt, state_ssm_conv[j].reshape(t, -1), ssm_state_rows, j, ssm_states,
                        *conv_args, *args)
                    cb = cb.reshape(t, CONV_W - 1, SSM_CONV_CH)
                sconv_l.append(cb)
                mixed, w_mix = y, ssm_out
            x = _mix_ffn(x, mixed, w_mix, j, norm_ffn_w[i], w_gate, w_up, w_down, i, norm_final_w,
                         final_norm=(i == depth - 1))
        if prompt:
            ret_new, gdn_new, ssm_new = jnp.stack(ret_l), jnp.stack(gdn_l), jnp.stack(ssm_l)
        else:
            ret_new, gdn_new = hyb_states
            ssm_new = ssm_states[0].reshape(state_ssm.shape)
        return x, ret_new, gdn_new, jnp.stack(gconv_l), ssm_new, jnp.stack(sconv_l)

    yp, ret_p, gdn_p, gconv_p, ssm_p, sconv_p = trunk(x_prompt.reshape(bp * seq, d), True)
    ys, ret_s, gdn_s, gconv_s, ssm_s, sconv_s = trunk(x_sample.reshape(bs, d), False)
    return (yp.reshape(bp, seq, d), ys.reshape(bs, 1, d),
            ret_p, gdn_p, gconv_p, ssm_p, sconv_p,
            ret_s, gdn_s, gconv_s, ssm_s, sconv_s)
```

```python
import functools
import math

import jax
import jax.numpy as jnp
from jax import lax
from jax.experimental import pallas as pl
from jax.experimental.pallas import tpu as pltpu

F32 = jnp.float32
BF16 = jnp.bfloat16

D_MODEL = 1024
N_HEADS = 4
HEAD_DIM = 128
HYB_MAIN = 4096
GDN_CONV_CH = 1536
SSM_DINNER = 2048
SSM_HEADS = 32
SSM_HEADDIM = 64
SSM_GROUPS = 4
SSM_DSTATE = 128
SSM_GROUP_W = SSM_DINNER // SSM_GROUPS
SSM_CONV_CH = 3072
SSM_MAIN = 5120
D_FF = 2816
FF_CHUNK = 256
CONV_W = 4
EPS = 1e-6
ROPE_BASE = 10000.0

LANES = 128
SUBLANES = 8
VMEM_LIMIT = 56 * 1024 * 1024

PROJ_PIECE_LANES = 256
SEQ_TILE = 256
SEQS_PER_STEP = 2
CHUNK = 64
STEP_BATCH = 8
LOG_GAMMA = tuple(math.log1p(-2.0 ** (-5.0 - h)) for h in range(N_HEADS))

_NT = (((1,), (1,)), ((), ()))
_TN = (((0,), (0,)), ((), ()))


def _dot(a, b):
    return jnp.dot(a, b, preferred_element_type=F32)


def _dot_nt(a, b):
    return lax.dot_general(a, b, _NT, preferred_element_type=F32)


def _dot_tn(a, b):
    return lax.dot_general(a, b, _TN, preferred_element_type=F32)


def _dot_exact(a, b):
    return jnp.dot(a, b, preferred_element_type=F32, precision=lax.Precision.HIGHEST)


def _silu(x):
    return x / (1.0 + jnp.exp(-x))


def _sigmoid(x):
    return 1.0 / (1.0 + jnp.exp(-x))


def _softplus(x):
    return jnp.maximum(x, 0.0) + jnp.log1p(jnp.exp(-jnp.abs(x)))


def _rmsnorm(x, w):
    return x * lax.rsqrt(jnp.mean(x * x, axis=-1, keepdims=True) + EPS) * w


def _resident(shape):
    nd = len(shape)
    return pl.BlockSpec(shape, lambda *_: (0,) * nd, pipeline_mode=pl.Buffered(1))


def _resident_layer(shape, layer):
    nd = len(shape)
    return pl.BlockSpec((None,) + tuple(shape), lambda *_: (layer,) + (0,) * nd,
                        pipeline_mode=pl.Buffered(1))


def _params(n_axes):
    return pltpu.CompilerParams(dimension_semantics=("arbitrary",) * n_axes,
                                vmem_limit_bytes=VMEM_LIMIT)


def _row_tile(t):
    return 512 if t % 512 == 0 else t


def _norm_proj_kernel(x_ref, nw_ref, w_ref, wt_ref, o_ref, ot_ref, *, n_chunk):
    h = _rmsnorm(x_ref[...], nw_ref[...]).astype(BF16)
    n = w_ref.shape[1]
    for c in range(n // n_chunk):
        sl = slice(c * n_chunk, (c + 1) * n_chunk)
        o_ref[:, sl] = _dot(h, w_ref[:, sl])
    ot_ref[...] = _dot(h, wt_ref[...])


def _norm_proj(x, nw, w_all, layer, n, w_tail):
    t, d = x.shape
    tm = _row_tile(t)
    return pl.pallas_call(
        functools.partial(_norm_proj_kernel, n_chunk=512),
        grid=(t // tm,),
        in_specs=[pl.BlockSpec((tm, d), lambda i: (i, 0)),
                  _resident((1, d)), _resident_layer((d, n), layer), _resident((d, LANES))],
        out_specs=[pl.BlockSpec((tm, n), lambda i: (i, 0)),
                   pl.BlockSpec((tm, LANES), lambda i: (i, 0))],
        out_shape=[jax.ShapeDtypeStruct((t, n), F32), jax.ShapeDtypeStruct((t, LANES), F32)],
        compiler_params=_params(1),
        name="norm_proj",
    )(x, nw.reshape(1, d), w_all, w_tail)


def _causal_conv_block(u, xx, cw_ref, cs):
    n = u.shape[0]
    xx[SUBLANES:SUBLANES + n, cs] = u
    ext = xx[:, cs]
    acc = ext * cw_ref[0:1, cs]
    for tap in range(1, CONV_W):
        acc = pltpu.roll(acc, 1, 0) + ext * cw_ref[tap:tap + 1, cs]
    return acc[SUBLANES:, :]


def _prompt_proj_body(x_ref, nw_ref, w_ref, wt_ref, o_ref, ot_ref, cb_ref, xx, acc_scr, tiles_per_seq,
                      epilogue):
    t = lax.rem(pl.program_id(0), tiles_per_seq)
    tm = x_ref.shape[0]
    piece = PROJ_PIECE_LANES
    per_piece = piece // LANES

    @pl.when(t == 0)
    def _():
        xx[0:SUBLANES, :] = jnp.zeros((SUBLANES, xx.shape[1]), F32)

    h = _rmsnorm(x_ref[...], nw_ref[...]).astype(BF16)
    n_pieces = w_ref.shape[1] // piece
    for p in range(n_pieces):
        acc = _dot(h, w_ref[:, p * piece:(p + 1) * piece])
        for j in range(per_piece):
            blk = p * per_piece + j
            o_ref[:, blk * LANES:(blk + 1) * LANES] = epilogue(
                blk // 4, blk % 4, acc[:, j * LANES:(j + 1) * LANES])
    ot_ref[...] = _dot(h, wt_ref[...])
    xx[0:SUBLANES, :] = xx[tm:tm + SUBLANES, :]

    @pl.when(t == tiles_per_seq - 1)
    def _():
        cb_ref[0] = xx[tm:tm + SUBLANES, :]


def _hyb_proj_kernel(x_ref, nw_ref, w_ref, wt_ref, cos_ref, sin_ref, cw_ref, o_ref, ot_ref, cb_ref,
                     xx, acc_scr, *, tiles_per_seq):
    cos = cos_ref[...]
    sin = sin_ref[...]
    scale = HEAD_DIM ** -0.5

    def epilogue(c, j, blk):
        if c == 0:
            return _rotary(blk, cos, sin)
        if c == 1:
            return _rotary(blk, cos, sin) * scale
        if c == 2:
            return blk
        if c in (3, 7):
            return _silu(blk)
        ch = (c - 4) * 4 + j
        act = _silu(_causal_conv_block(blk, xx, cw_ref, slice(ch * LANES, (ch + 1) * LANES)))
        if c == 4:
            return _l2norm(act) * scale
        if c == 5:
            return _l2norm(act)
        return act

    _prompt_proj_body(x_ref, nw_ref, w_ref, wt_ref, o_ref, ot_ref, cb_ref, xx, acc_scr, tiles_per_seq,
                      epilogue)


def _ssm_proj_kernel(x_ref, nw_ref, w_ref, wt_ref, cw_ref, cbias_ref, o_ref, ot_ref, cb_ref, xx,
                     acc_scr, *, tiles_per_seq):
    def epilogue(c, j, blk):
        if c < 4:
            return _silu(blk)
        ch = (c - 4) * 4 + j
        cs = slice(ch * LANES, (ch + 1) * LANES)
        return _silu(_causal_conv_block(blk, xx, cw_ref, cs) + cbias_ref[:, cs])

    _prompt_proj_body(x_ref, nw_ref, w_ref, wt_ref, o_ref, ot_ref, cb_ref, xx, acc_scr, tiles_per_seq,
                      epilogue)


def _prompt_proj(kernel_fn, x, nw, w_all, layer, n, w_tail, conv_ch, seq, extra, extra_specs):
    t, d = x.shape
    tm = _row_tile(t)
    tiles_per_seq = seq // tm
    return pl.pallas_call(
        functools.partial(kernel_fn, tiles_per_seq=tiles_per_seq),
        grid=(t // tm,),
        in_specs=[pl.BlockSpec((tm, d), lambda i: (i, 0)),
                  _resident((1, d)), _resident_layer((d, n), layer), _resident((d, LANES))] + extra_specs,
        out_specs=[pl.BlockSpec((tm, n), lambda i: (i, 0)),
                   pl.BlockSpec((tm, LANES), lambda i: (i, 0)),
                   pl.BlockSpec((1, SUBLANES, conv_ch), lambda i: (i // tiles_per_seq, 0, 0))],
        out_shape=[jax.ShapeDtypeStruct((t, n), F32), jax.ShapeDtypeStruct((t, LANES), F32),
                   jax.ShapeDtypeStruct((t // seq, SUBLANES, conv_ch), F32)],
        scratch_shapes=[pltpu.VMEM((tm + SUBLANES, conv_ch), F32),
                        pltpu.VMEM((2, tm, PROJ_PIECE_LANES), F32)],
        compiler_params=_params(1),
        name=kernel_fn.__name__.strip("_"),
    )(x, nw.reshape(1, d), w_all, w_tail, *extra)


def _ffn_kernel(x_ref, a_ref, wo_ref, nw_ref, wg_ref, wu_ref, wd_ref, fw_ref, o_ref, act_ref, *,
                final_norm):
    x = x_ref[...] + _dot(a_ref[...].astype(BF16), wo_ref[...])
    h = _rmsnorm(x, nw_ref[...]).astype(BF16)
    for c in range(D_FF // FF_CHUNK):
        sl = slice(c * FF_CHUNK, (c + 1) * FF_CHUNK)
        gate = _dot(h, wg_ref[:, sl])
        up = _dot(h, wu_ref[:, sl])
        act_ref[:, sl] = (_silu(gate) * up).astype(BF16)
    y = x + _dot(act_ref[...], wd_ref[...])
    if final_norm:
        y = _rmsnorm(y, fw_ref[...])
    o_ref[...] = y


def _mix_ffn(x, a, w_out, out_layer, nw, wg, wu, wd, layer, final_w, final_norm):
    t, d = x.shape
    k = a.shape[1]
    tm = _row_tile(t)
    return pl.pallas_call(
        functools.partial(_ffn_kernel, final_norm=final_norm),
        grid=(t // tm,),
        in_specs=[pl.BlockSpec((tm, d), lambda i: (i, 0)),
                  pl.BlockSpec((tm, k), lambda i: (i, 0)),
                  _resident_layer((k, d), out_layer),
                  _resident((1, d)), _resident_layer((d, D_FF), layer),
                  _resident_layer((d, D_FF), layer), _resident_layer((D_FF, d), layer),
                  _resident((1, d))],
        out_specs=pl.BlockSpec((tm, d), lambda i: (i, 0)),
        out_shape=jax.ShapeDtypeStruct((t, d), F32),
        scratch_shapes=[pltpu.VMEM((tm, D_FF), BF16)],
        compiler_params=_params(1),
        name="mix_ffn",
    )(x, a, w_out, nw.reshape(1, d), wg, wu, wd, final_w.reshape(1, d))


def _rotary(x, cos, sin_signed):
    lane = lax.broadcasted_iota(jnp.int32, x.shape, 1)
    partner = jnp.where((lane & 1) == 0, pltpu.roll(x, LANES - 1, 1), pltpu.roll(x, 1, 1))
    return x * cos + partner * sin_signed


def _l2norm(x):
    return x * lax.rsqrt(jnp.sum(x * x, axis=-1, keepdims=True) + EPS)


def _split3(x):
    hi = x.astype(BF16)
    r1 = x - hi.astype(F32)
    mid = r1.astype(BF16)
    lo = (r1 - mid.astype(F32)).astype(BF16)
    return hi, mid, lo


def _pack3(x):
    lane = lax.broadcasted_iota(jnp.int32, x.shape, 1)
    x = jnp.where(lane < SSM_HEADS, x, 0.0)
    hi = x.astype(BF16).astype(F32)
    r1 = x - hi
    mid = r1.astype(BF16).astype(F32)
    lo = r1 - mid
    return (hi + pltpu.roll(mid, SSM_HEADS, 1) + pltpu.roll(lo, 2 * SSM_HEADS, 1)).astype(BF16)


def _unpack3(y):
    return y + pltpu.roll(y, LANES - SSM_HEADS, 1) + pltpu.roll(y, LANES - 2 * SSM_HEADS, 1)


def _expand_heads(x, expand_mat):
    return _dot(_pack3(x), expand_mat)


def _head_expand_matrix():
    row = lax.broadcasted_iota(jnp.int32, (LANES, SSM_DINNER), 0)
    col = lax.broadcasted_iota(jnp.int32, (LANES, SSM_DINNER), 1)
    hit = ((row & (SSM_HEADS - 1)) == (col >> 6)) & (row < 3 * SSM_HEADS)
    return jnp.where(hit, 1.0, 0.0).astype(BF16)


def _chunk_sum_matrices(n):
    ii = lax.broadcasted_iota(jnp.int32, (n, n), 0)
    jj = lax.broadcasted_iota(jnp.int32, (n, n), 1)
    same = (ii >> 6) == (jj >> 6)
    prefix = jnp.where(same & (ii >= jj), 1.0, 0.0).astype(F32)
    total = jnp.where(same, 1.0, 0.0).astype(F32)
    return prefix, total


def _columns(rows):
    pad = jnp.zeros((LANES - rows.shape[0], LANES), F32)
    return jnp.concatenate([rows, pad], axis=0).T


def _unit_lower_inverses(lows, interleaved=()):
    interleaved = list(interleaved)
    n = lows[0].shape[0]
    ii = lax.broadcasted_iota(jnp.int32, (n, n), 0)
    jj = lax.broadcasted_iota(jnp.int32, (n, n), 1)
    eye = (ii == jj).astype(F32)
    pair = (ii >> 1) == (jj >> 1)
    ts = [eye - jnp.where(pair, low, 0.0) for low in lows]
    lows_b = [low.astype(BF16) for low in lows]
    for k in range(1, 6):
        join = ((ii >> (k + 1)) == (jj >> (k + 1))) & ((ii >> k) != (jj >> k))
        join_b = join.astype(F32).astype(BF16)
        ts_b = [t.astype(BF16) for t in ts]
        txs = [_dot(t_b, low_b * join_b).astype(BF16) for t_b, low_b in zip(ts_b, lows_b)]
        ts = [t - _dot(tx, t_b) for t, tx, t_b in zip(ts, txs, ts_b)]
        if interleaved:
            interleaved.pop(0)()
        yield
    for fn in interleaved:
        fn()
    return [t - eye for t in ts]


def _hyb_core_kernel(pm_ref, pt_ref, alog_ref, dtb_ref, gnw_ref, gnn_ref,
                     mix_ref, rs_ref, gs_ref, sret, sgdn, intra_scr, *, n_tiles):
    t = pl.program_id(1)
    tl = SEQ_TILE
    hd = HEAD_DIM

    @pl.when(t == 0)
    def _():
        sret[...] = jnp.zeros_like(sret)
        sgdn[...] = jnp.zeros_like(sgdn)

    @pl.when((t == 0) & (pl.program_id(0) == 0))
    def _():
        ii = lax.broadcasted_iota(jnp.int32, (tl, tl), 0)
        jj = lax.broadcasted_iota(jnp.int32, (tl, tl), 1)
        diff = (ii - jj).astype(F32)
        for h in range(N_HEADS):
            intra_scr[h] = jnp.where(diff >= 0.0, jnp.exp(LOG_GAMMA[h] * jnp.maximum(diff, 0.0)), 0.0)

    idx = lax.broadcasted_iota(jnp.int32, (tl, 1), 0).astype(F32)

    def retention_head(h):
        lg = LOG_GAMMA[h]
        q = pm_ref[:, h * hd:(h + 1) * hd]
        k = pm_ref[:, 512 + h * hd:512 + (h + 1) * hd]
        v = pm_ref[:, 1024 + h * hd:1024 + (h + 1) * hd]
        gate_act = pm_ref[:, 1536 + h * hd:1536 + (h + 1) * hd]
        q_dec = jnp.exp(lg * (idx + 1.0))
        k_dec = jnp.exp(lg * (tl - 1.0 - idx))
        s = sret[h]
        qb = q.astype(BF16)
        vb = v.astype(BF16)
        scores = _dot_nt(qb, k.astype(BF16)) * intra_scr[h]
        o = _dot(scores.astype(BF16), vb) + _dot(qb, s.astype(BF16)) * q_dec
        sret[h] = s * math.exp(lg * tl) + _dot_tn((k * k_dec).astype(BF16), vb)
        mu = jnp.mean(o, axis=-1, keepdims=True)
        var = jnp.mean(jnp.square(o - mu), axis=-1, keepdims=True)
        ro = (o - mu) * lax.rsqrt(var + EPS) * gnw_ref[:, h * hd:(h + 1) * hd]
        mix_ref[:, h * hd:(h + 1) * hd] = (gate_act * ro).astype(BF16)

    tail = pt_ref[...]
    g_all = -jnp.exp(alog_ref[...]) * _softplus(tail + dtb_ref[...])
    beta_all = _sigmoid(tail)
    prefix, total = _chunk_sum_matrices(tl)
    g_parts = _pack3(g_all)
    gc_all = _unpack3(_dot(prefix.astype(BF16), g_parts))
    gl_all = _unpack3(_dot(total.astype(BF16), g_parts))
    yield

    n = N_HEADS * CHUNK
    si = lax.broadcasted_iota(jnp.int32, (n, n), 0)
    sj = lax.broadcasted_iota(jnp.int32, (n, n), 1)
    same_head = (si >> 6) == (sj >> 6)
    causal = same_head & (si >= sj)
    strict = same_head & (si > sj)

    def stack(fn):
        return jnp.concatenate([fn(h) for h in range(N_HEADS)], axis=0)

    chunks = []
    for c in range(tl // CHUNK):
        r = slice(c * CHUNK, (c + 1) * CHUNK)
        q_st = stack(lambda h: pm_ref[r, 2048 + h * hd:2048 + (h + 1) * hd])
        k_st = stack(lambda h: pm_ref[r, 2560 + h * hd:2560 + (h + 1) * hd])
        v_st = stack(lambda h: pm_ref[r, 3072 + h * hd:3072 + (h + 1) * hd])
        beta = stack(lambda h: beta_all[r, 4 + h:5 + h])
        gc = stack(lambda h: gc_all[r, h:h + 1])
        gl = stack(lambda h: gl_all[r, h:h + 1])
        gc_col = jnp.broadcast_to(gc, (n, n))
        seg = gc_col - gc_col.T
        decay = jnp.where(causal, jnp.exp(jnp.where(causal, seg, 0.0)), 0.0)
        k_b16 = k_st.astype(BF16)
        qk = _dot_nt(jnp.concatenate([q_st.astype(BF16), k_b16], axis=0), k_b16)
        egc = jnp.exp(gc)
        chunks.append(dict(
            r=r, gl=gl,
            low=jnp.where(strict, qk[n:] * beta * decay, 0.0),
            rhs=jnp.concatenate([v_st * beta, k_st * (beta * egc)], axis=1),
            scores=(qk[:n] * decay).astype(BF16),
            q_in=(q_st * egc).astype(BF16),
            k_out=(k_st * jnp.exp(gl - gc)).astype(BF16)))
        yield
    t_offs = yield from _unit_lower_inverses(
        [ch["low"] for ch in chunks],
        interleaved=[functools.partial(retention_head, h) for h in range(N_HEADS)])
    sols = [ch["rhs"] + _dot(t_off.astype(BF16), ch["rhs"].astype(BF16))
            for ch, t_off in zip(chunks, t_offs)]
    yield

    for ch, sol in zip(chunks, sols):
        r = ch["r"]
        w_b16 = sol[:, hd:].astype(BF16)
        v_new = []
        o_inter = []
        for h in range(N_HEADS):
            hr = slice(h * CHUNK, (h + 1) * CHUNK)
            lhs = jnp.concatenate([w_b16[hr], ch["q_in"][hr]], axis=0)
            prod = _dot(lhs, sgdn[h].astype(BF16))
            v_new.append(sol[hr, :hd] - prod[:CHUNK])
            o_inter.append(prod[CHUNK:])
        yield
        vn_b16 = jnp.concatenate(v_new, axis=0).astype(BF16)
        o_st = jnp.concatenate(o_inter, axis=0) + _dot(ch["scores"], vn_b16)
        yield
        for h in range(N_HEADS):
            hr = slice(h * CHUNK, (h + 1) * CHUNK)
            s_dec = jnp.exp(jnp.broadcast_to(ch["gl"][hr], (CHUNK, hd)))
            s_dec = jnp.concatenate([s_dec, s_dec], axis=0)
            sgdn[h] = sgdn[h] * s_dec + _dot_tn(ch["k_out"][hr], vn_b16[hr])
            o_h = o_st[hr]
            go = o_h * lax.rsqrt(jnp.mean(o_h * o_h, axis=-1, keepdims=True) + EPS) * gnn_ref[...]
            gz_act = pm_ref[r, 3584 + h * hd:3584 + (h + 1) * hd]
            mix_ref[r, 512 + h * hd:512 + (h + 1) * hd] = (go * gz_act).astype(BF16)
        yield

    @pl.when(t == n_tiles - 1)
    def _():
        rs_ref[...] = sret[...]
        gs_ref[...] = sgdn[...]


def _hyb_core_prompt(pm, pt, alog, dtb, gnw, gnn, batch, seq):
    n_tiles = seq // SEQ_TILE
    ns = SEQS_PER_STEP
    tile = lambda b, t: (b, t, 0)
    state = pl.BlockSpec((ns, N_HEADS, HEAD_DIM, HEAD_DIM), lambda b, t: (b, 0, 0, 0))
    per_seq = lambda shape: pltpu.VMEM((ns,) + shape, F32)
    mixed, rs, gs = pl.pallas_call(
        functools.partial(_per_sequence(_hyb_core_kernel, ns, 2, 4), n_tiles=n_tiles),
        grid=(batch // ns, n_tiles),
        in_specs=[pl.BlockSpec((ns, SEQ_TILE, HYB_MAIN), tile),
                  pl.BlockSpec((ns, SEQ_TILE, LANES), tile),
                  _resident((1, LANES)), _resident((1, LANES)),
                  _resident((1, 512)), _resident((1, LANES))],
        out_specs=[pl.BlockSpec((ns, SEQ_TILE, D_MODEL), tile), state, state],
        out_shape=[jax.ShapeDtypeStruct((batch, seq, D_MODEL), BF16),
                   jax.ShapeDtypeStruct((batch, N_HEADS, HEAD_DIM, HEAD_DIM), F32),
                   jax.ShapeDtypeStruct((batch, N_HEADS, HEAD_DIM, HEAD_DIM), F32)],
        scratch_shapes=[per_seq((N_HEADS, HEAD_DIM, HEAD_DIM)),
                        per_seq((N_HEADS, HEAD_DIM, HEAD_DIM)),
                        per_seq((N_HEADS, SEQ_TILE, SEQ_TILE))],
        compiler_params=_params(2),
        name="hyb_core_prompt",
    )(pm.reshape(batch, seq, HYB_MAIN), pt.reshape(batch, seq, LANES), alog, dtb, gnw, gnn)
    return mixed.reshape(batch * seq, D_MODEL), rs, gs


def _group_rmsnorm_gate(y, z, nw_ref, o_ref, rows):
    for g in range(SSM_GROUPS):
        sl = slice(g * SSM_GROUP_W, (g + 1) * SSM_GROUP_W)
        yg = y[:, sl] * _silu(z[:, sl])
        yg = yg * lax.rsqrt(jnp.mean(yg * yg, axis=-1, keepdims=True) + EPS)
        o_ref[rows, sl] = (yg * nw_ref[:, sl]).astype(o_ref.dtype)


def _ssm_core_kernel(pm_ref, pt_ref, alog_ref, dtb_ref, dskip_ref, nw_ref,
                     y_ref, st_ref, s_t, xdt, xdec, eacs, acs_e, y_scr, *, n_tiles):
    t = pl.program_id(1)
    tl = SEQ_TILE
    xbc = pm_ref.at[:, SSM_DINNER:SSM_MAIN]

    @pl.when(t == 0)
    def _():
        s_t[...] = jnp.zeros_like(s_t)

    n_chunks = tl // CHUNK
    expand = _head_expand_matrix()
    dt = _softplus(pt_ref[...] + dtb_ref[...])
    a = dt * (-jnp.exp(alog_ref[...]))
    prefix, total = _chunk_sum_matrices(tl)
    a_parts = _pack3(a)
    acs = _unpack3(_dot(prefix.astype(BF16), a_parts))
    atot = _unpack3(_dot(total.astype(BF16), a_parts))
    dt_parts = _pack3(dt)
    dec_parts = _pack3(jnp.exp(atot - acs))
    eacs_parts = _pack3(jnp.exp(acs))
    acs_parts = _pack3(acs)
    chunk_rows = [atot[c * CHUNK:c * CHUNK + 1] for c in range(n_chunks)]
    chunk_rows.append(jnp.zeros((SUBLANES - n_chunks, LANES), F32))
    cdec_parts = _pack3(jnp.exp(jnp.concatenate(chunk_rows, axis=0)))

    def expand_group(parts, g):
        return _dot(parts, expand[:, g * SSM_GROUP_W:(g + 1) * SSM_GROUP_W])

    yield
    chunk_dec = []
    for g in range(SSM_GROUPS):
        gs = slice(g * SSM_GROUP_W, (g + 1) * SSM_GROUP_W)
        x_dt = xbc[:, gs] * expand_group(dt_parts, g)
        xdt[:, gs] = x_dt
        yield
        xdec[:, gs] = (x_dt * expand_group(dec_parts, g)).astype(BF16)
        yield
        eacs[:, gs] = expand_group(eacs_parts, g)
        yield
        acs_e[:, gs] = expand_group(acs_parts, g)
        chunk_dec.append(expand_group(cdec_parts, g))
        yield

    lane = lax.broadcasted_iota(jnp.int32, (CHUNK, LANES), 1)
    row_i = lax.broadcasted_iota(jnp.int32, (CHUNK, LANES), 0)
    first_half = lane < SSM_HEADDIM
    causal2 = row_i >= (lane & (CHUNK - 1))
    lane2 = lax.broadcasted_iota(jnp.int32, (2 * CHUNK, LANES), 1)
    row2 = lax.broadcasted_iota(jnp.int32, (2 * CHUNK, LANES), 0)
    pair_diag = (row2 < CHUNK) == (lane2 < SSM_HEADDIM)

    def bc_cc(r, g):
        bc = xbc[r, SSM_DINNER + g * LANES:SSM_DINNER + (g + 1) * LANES]
        cc = xbc[r, SSM_DINNER + 512 + g * LANES:SSM_DINNER + 512 + (g + 1) * LANES]
        return bc.astype(BF16), cc.astype(BF16)

    for c in range(n_chunks):
        r = slice(c * CHUNK, (c + 1) * CHUNK)
        acs_c = acs[r]
        acs_t = jnp.concatenate([acs_c, acs_c], axis=0).T
        cb2 = []
        for g in range(SSM_GROUPS):
            bc_b16, cc_b16 = bc_cc(r, g)
            cb2.append(_dot_nt(cc_b16, jnp.concatenate([bc_b16, bc_b16], axis=0)))
        yield
        for p in range(SSM_HEADS // 2):
            ha = 2 * p
            rowv = jnp.where(first_half, acs_t[ha:ha + 1, :], acs_t[ha + 1:ha + 2, :])
            seg = acs_e[r, p * LANES:(p + 1) * LANES] - rowv
            lmat = jnp.where(causal2, jnp.exp(jnp.where(causal2, seg, 0.0)), 0.0)
            m = (cb2[p // 4] * lmat).astype(BF16)
            xblk = xdt[r, p * LANES:(p + 1) * LANES]
            xblk2 = jnp.concatenate([xblk, xblk], axis=0)
            x_bd = jnp.where(pair_diag, xblk2, 0.0).astype(BF16)
            y_scr[r, p * LANES:(p + 1) * LANES] = _dot(m, x_bd)
            yield

    for c in range(n_chunks):
        r = slice(c * CHUNK, (c + 1) * CHUNK)
        groups = range(SSM_GROUPS)
        gsl = [slice(g * SSM_GROUP_W, (g + 1) * SSM_GROUP_W) for g in groups]
        bcs, ccs = zip(*[bc_cc(r, g) for g in groups])
        s_old = [s_t[:, gsl[g]] for g in groups]
        y_inter = [_dot(ccs[g], s_old[g].astype(BF16)) for g in groups]
        yield
        s_add = [_dot_tn(bcs[g], xdec[r, gsl[g]]) for g in groups]
        yield
        for g in groups:
            s_t[:, gsl[g]] = s_old[g] * chunk_dec[g][c:c + 1, :] + s_add[g]
            y_scr[r, gsl[g]] = y_scr[r, gsl[g]] + y_inter[g] * eacs[r, gsl[g]]
        yield

    for g in range(SSM_GROUPS):
        gs = slice(g * SSM_GROUP_W, (g + 1) * SSM_GROUP_W)
        yg = (y_scr[:, gs] + xbc[:, gs] * dskip_ref[:, gs]) * pm_ref[:, gs]
        yg = yg * lax.rsqrt(jnp.mean(yg * yg, axis=-1, keepdims=True) + EPS)
        y_ref[:, gs] = (yg * nw_ref[:, gs]).astype(y_ref.dtype)
        yield

    @pl.when(t == n_tiles - 1)
    def _():
        for kblk in range(SSM_DINNER // LANES):
            ks = slice(kblk * LANES, (kblk + 1) * LANES)
            st_ref[ks, :] = s_t[:, ks].T


def _per_sequence(body, n_seq, n_tiled_in, n_shared_in):
    def kernel(*refs, **kw):
        tiled = refs[:n_tiled_in]
        shared = refs[n_tiled_in:n_tiled_in + n_shared_in]
        rest = refs[n_tiled_in + n_shared_in:]
        running = [body(*[r.at[s] for r in tiled], *shared, *[r.at[s] for r in rest], **kw)
                   for s in range(n_seq)]
        while running:
            for gen in list(running):
                if next(gen, StopIteration) is StopIteration:
                    running.remove(gen)
    return kernel


def _ssm_core_prompt(pm, pt, alog, dtb, dskip, nw, batch, seq):
    n_tiles = seq // SEQ_TILE
    ns = SEQS_PER_STEP
    tile = lambda b, t: (b, t, 0)
    per_seq = lambda shape, dtype: pltpu.VMEM((ns,) + shape, dtype)
    y, st = pl.pallas_call(
        functools.partial(_per_sequence(_ssm_core_kernel, ns, 2, 4), n_tiles=n_tiles),
        grid=(batch // ns, n_tiles),
        in_specs=[pl.BlockSpec((ns, SEQ_TILE, SSM_MAIN), tile),
                  pl.BlockSpec((ns, SEQ_TILE, LANES), tile),
                  _resident((1, LANES)), _resident((1, LANES)),
                  _resident((1, SSM_DINNER)), _resident((1, SSM_DINNER))],
        out_specs=[pl.BlockSpec((ns, SEQ_TILE, SSM_DINNER), tile),
                   pl.BlockSpec((ns, SSM_DINNER, SSM_DSTATE), lambda b, t: (b, 0, 0))],
        out_shape=[jax.ShapeDtypeStruct((batch, seq, SSM_DINNER), BF16),
                   jax.ShapeDtypeStruct((batch, SSM_DINNER, SSM_DSTATE), F32)],
        scratch_shapes=[per_seq((SSM_DSTATE, SSM_DINNER), F32),
                        per_seq((SEQ_TILE, SSM_DINNER), F32),
                        per_seq((SEQ_TILE, SSM_DINNER), BF16),
                        per_seq((SEQ_TILE, SSM_DINNER), F32),
                        per_seq((SEQ_TILE, SSM_DINNER), F32),
                        per_seq((SEQ_TILE, SSM_DINNER), F32)],
        compiler_params=_params(2),
        name="ssm_core_prompt",
    )(pm.reshape(batch, seq, SSM_MAIN), pt.reshape(batch, seq, LANES), alog, dtb, dskip, nw)
    return y.reshape(batch * seq, SSM_DINNER), st


def _hyb_step_kernel(pm_ref, pt_ref, conv_ref, rs_in, gs_in, cos_ref, sin_ref, cw_ref, alog_ref,
                     dtb_ref, gnw_ref, gnn_ref, mix_ref, conv_out, rs_out, gs_out, o_scr):
    bb = STEP_BATCH
    hd = HEAD_DIM
    ch = GDN_CONV_CH
    cos = cos_ref[...]
    sin = sin_ref[...]
    u = pm_ref[:, 2048:2048 + ch]
    c0 = conv_ref[:, 0:ch]
    c1 = conv_ref[:, ch:2 * ch]
    c2 = conv_ref[:, 2 * ch:3 * ch]
    conv = c0 * cw_ref[0:1, :]
    conv = conv + c1 * cw_ref[1:2, :]
    conv = conv + c2 * cw_ref[2:3, :]
    conv = conv + u * cw_ref[3:4, :]
    qkv = _silu(conv)
    conv_out[:, 0:ch] = c1
    conv_out[:, ch:2 * ch] = c2
    conv_out[:, 2 * ch:3 * ch] = u
    tail = pt_ref[...]
    g_all = -jnp.exp(alog_ref[...]) * _softplus(tail + dtb_ref[...])
    beta_all = _sigmoid(tail)

    for h in range(N_HEADS):
        q = _rotary(pm_ref[:, h * hd:(h + 1) * hd], cos, sin)
        k = _rotary(pm_ref[:, 512 + h * hd:512 + (h + 1) * hd], cos, sin) * (hd ** -0.5)
        v = pm_ref[:, 1024 + h * hd:1024 + (h + 1) * hd]
        k_cols = _columns(k)
        gamma = math.exp(LOG_GAMMA[h])
        for b in range(bb):
            s = rs_in[b, h] * gamma + k_cols[:, b:b + 1] * v[b:b + 1, :]
            rs_out[b, h] = s
            o_scr[b:b + 1, h * hd:(h + 1) * hd] = _dot(q, s)[b:b + 1, :]
        gq = _l2norm(qkv[:, h * hd:(h + 1) * hd]) * (hd ** -0.5)
        gk = _l2norm(qkv[:, 512 + h * hd:512 + (h + 1) * hd])
        gv = qkv[:, 1024 + h * hd:1024 + (h + 1) * hd]
        k_cols = _columns(gk)
        decay = jnp.broadcast_to(jnp.exp(g_all[:, h:h + 1]), (bb, hd))
        beta = jnp.broadcast_to(beta_all[:, 4 + h:5 + h], (bb, hd))
        for b in range(bb):
            s = gs_in[b, h] * decay[b:b + 1, :]
            delta = (gv[b:b + 1, :] - _dot(gk, s)[b:b + 1, :]) * beta[b:b + 1, :]
            s = s + k_cols[:, b:b + 1] * delta
            gs_out[b, h] = s
            o_scr[b:b + 1, 512 + h * hd:512 + (h + 1) * hd] = _dot(gq, s)[b:b + 1, :]

    for h in range(N_HEADS):
        o = o_scr[:, h * hd:(h + 1) * hd]
        mu = jnp.mean(o, axis=-1, keepdims=True)
        var = jnp.mean(jnp.square(o - mu), axis=-1, keepdims=True)
        ro = (o - mu) * lax.rsqrt(var + EPS) * gnw_ref[:, h * hd:(h + 1) * hd]
        mix_ref[:, h * hd:(h + 1) * hd] = _silu(pm_ref[:, 1536 + h * hd:1536 + (h + 1) * hd]) * ro
        go = o_scr[:, 512 + h * hd:512 + (h + 1) * hd]
        go = go * lax.rsqrt(jnp.mean(go * go, axis=-1, keepdims=True) + EPS) * gnn_ref[...]
        mix_ref[:, 512 + h * hd:512 + (h + 1) * hd] = go * _silu(pm_ref[:, 3584 + h * hd:3584 + (h + 1) * hd])


def _layer_chained_outputs(state_arrays, layer, partial):
    shapes = [jax.ShapeDtypeStruct(s.shape, s.dtype) for s in state_arrays]
    if layer == 0:
        return [], [], shapes
    return list(partial), [pl.BlockSpec(memory_space=pl.ANY)] * len(partial), shapes


def _hyb_core_step(pm, pt, conv_state, ret_all, gdn_all, layer, partial,
                   cos, sin, cw_t, alog, dtb, gnw, gnn):
    batch = pm.shape[0]
    bb = STEP_BATCH
    row = lambda i: (i, 0)
    state = pl.BlockSpec((None, bb, N_HEADS, HEAD_DIM, HEAD_DIM), lambda i: (layer, i, 0, 0, 0))
    extra, extra_specs, state_shapes = _layer_chained_outputs((ret_all, gdn_all), layer, partial)
    n_in = 12
    aliases = {n_in + k: 2 + k for k in range(len(extra))}

    def body(*refs):
        _hyb_step_kernel(*refs[:n_in], *refs[n_in + len(extra):])

    return pl.pallas_call(
        body,
        grid=(batch // bb,),
        in_specs=[pl.BlockSpec((bb, HYB_MAIN), row), pl.BlockSpec((bb, LANES), row),
                  pl.BlockSpec((bb, 3 * GDN_CONV_CH), row), state, state,
                  _resident((1, LANES)), _resident((1, LANES)),
                  _resident((CONV_W, GDN_CONV_CH)), _resident((1, LANES)), _resident((1, LANES)),
                  _resident((1, 512)), _resident((1, LANES))] + extra_specs,
        out_specs=[pl.BlockSpec((bb, D_MODEL), row), pl.BlockSpec((bb, 3 * GDN_CONV_CH), row),
                   state, state],
        out_shape=[jax.ShapeDtypeStruct((batch, D_MODEL), F32),
                   jax.ShapeDtypeStruct((batch, 3 * GDN_CONV_CH), F32)] + state_shapes,
        scratch_shapes=[pltpu.VMEM((bb, D_MODEL), F32)],
        input_output_aliases=aliases,
        compiler_params=_params(1),
        name="hyb_core_step",
    )(pm, pt, conv_state, ret_all, gdn_all, cos, sin, cw_t, alog, dtb, gnw, gnn, *extra)


def _ssm_step_kernel(pm_ref, pt_ref, conv_ref, s_in, cw_ref, cbias_ref, alog_ref, dtb_ref,
                     dskip_ref, nw_ref, y_ref, conv_out, s_out, y_scr):
    bb = STEP_BATCH
    ch = SSM_CONV_CH
    u = pm_ref[:, SSM_DINNER:SSM_MAIN]
    c0 = conv_ref[:, 0:ch]
    c1 = conv_ref[:, ch:2 * ch]
    c2 = conv_ref[:, 2 * ch:3 * ch]
    conv = c0 * cw_ref[0:1, :]
    conv = conv + c1 * cw_ref[1:2, :]
    conv = conv + c2 * cw_ref[2:3, :]
    conv = conv + u * cw_ref[3:4, :]
    xbc = _silu(conv + cbias_ref[...])
    conv_out[:, 0:ch] = c1
    conv_out[:, ch:2 * ch] = c2
    conv_out[:, 2 * ch:3 * ch] = u
    xs = xbc[:, 0:SSM_DINNER]
    expand = _head_expand_matrix()
    dt = _softplus(pt_ref[...] + dtb_ref[...])
    a = dt * (-jnp.exp(alog_ref[...]))
    x_dt = xs * _expand_heads(dt, expand)
    decay = _expand_heads(jnp.exp(a), expand)
    for kblk in range(SSM_DINNER // LANES):
        ks = slice(kblk * LANES, (kblk + 1) * LANES)
        g = kblk // 4
        x_cols = _columns(x_dt[:, ks])
        d_cols = _columns(decay[:, ks])
        bm = xbc[:, SSM_DINNER + g * LANES:SSM_DINNER + (g + 1) * LANES]
        cm = xbc[:, SSM_DINNER + 512 + g * LANES:SSM_DINNER + 512 + (g + 1) * LANES]
        for b in range(bb):
            s = s_in[b, ks, :] * d_cols[:, b:b + 1] + x_cols[:, b:b + 1] * bm[b:b + 1, :]
            s_out[b, ks, :] = s
            y_scr[b:b + 1, ks] = _dot_nt(cm, s)[b:b + 1, :]
    y = y_scr[...] + xs * dskip_ref[...]
    _group_rmsnorm_gate(y, pm_ref[:, 0:SSM_DINNER], nw_ref, y_ref, slice(None))


def _ssm_core_step(pm, pt, conv_state, ssm_all, layer, partial, cw_t, cbias, alog, dtb, dskip, nw):
    batch = pm.shape[0]
    bb = STEP_BATCH
    row = lambda i: (i, 0)
    state = pl.BlockSpec((None, bb, SSM_DINNER, SSM_DSTATE), lambda i: (layer, i, 0, 0))
    extra, extra_specs, state_shapes = _layer_chained_outputs((ssm_all,), layer, partial)
    n_in = 10
    aliases = {n_in + k: 2 + k for k in range(len(extra))}

    def body(*refs):
        _ssm_step_kernel(*refs[:n_in], *refs[n_in + len(extra):])

    return pl.pallas_call(
        body,
        grid=(batch // bb,),
        in_specs=[pl.BlockSpec((bb, SSM_MAIN), row), pl.BlockSpec((bb, LANES), row),
                  pl.BlockSpec((bb, 3 * SSM_CONV_CH), row), state,
                  _resident((CONV_W, SSM_CONV_CH)), _resident((1, SSM_CONV_CH)),
                  _resident((1, LANES)), _resident((1, LANES)),
                  _resident((1, SSM_DINNER)), _resident((1, SSM_DINNER))] + extra_specs,
        out_specs=[pl.BlockSpec((bb, SSM_DINNER), row), pl.BlockSpec((bb, 3 * SSM_CONV_CH), row),
                   state],
        out_shape=[jax.ShapeDtypeStruct((batch, SSM_DINNER), F32),
                   jax.ShapeDtypeStruct((batch, 3 * SSM_CONV_CH), F32)] + state_shapes,
        scratch_shapes=[pltpu.VMEM((bb, SSM_DINNER), F32)],
        input_output_aliases=aliases,
        compiler_params=_params(1),
        name="ssm_core_step",
    )(pm, pt, conv_state, ssm_all, cw_t, cbias, alog, dtb, dskip, nw, *extra)


def _rope_tables(pos):
    inv_freq = 1.0 / (ROPE_BASE ** jnp.linspace(0.0, 1.0, HEAD_DIM // 2, dtype=F32))
    ang = pos.astype(F32)[:, None] * inv_freq[None, :]
    cos = jnp.repeat(jnp.cos(ang), 2, axis=-1)
    sign = jnp.tile(jnp.array([-1.0, 1.0], F32), HEAD_DIM // 2)
    sin = jnp.repeat(jnp.sin(ang), 2, axis=-1) * sign
    return cos, sin


def _lane_row(v):
    return jnp.pad(v.astype(F32), (0, LANES - v.shape[0])).reshape(1, LANES)


def kernel(x_prompt, x_sample, state_ret, state_gdn, state_gdn_conv, state_ssm, state_ssm_conv,
           norm_mix_w, norm_ffn_w, norm_final_w, hyb_w_in, hyb_w_out, ret_gn_w, gdn_conv_w,
           gdn_a_log, gdn_dt_bias, gdn_norm_w, ssm_w_in, ssm_conv_w, ssm_conv_b, ssm_dt_bias,
           ssm_a_log, ssm_d, ssm_norm_w, ssm_w_out, ffn_w_gate, ffn_w_up, ffn_w_down):
    depth = norm_mix_w.shape[0]
    bp, seq, d = x_prompt.shape
    bs = x_sample.shape[0]
    past_len = 16384

    hyb_in = hyb_w_in.astype(BF16)
    hyb_tail = jnp.pad(hyb_w_in[:, :, HYB_MAIN:], ((0, 0), (0, 0), (0, LANES - 2 * N_HEADS))).astype(BF16)
    ssm_in = ssm_w_in.astype(BF16)
    ssm_tail = jnp.pad(ssm_w_in[:, :, SSM_MAIN:], ((0, 0), (0, 0), (0, LANES - SSM_HEADS))).astype(BF16)
    hyb_out = hyb_w_out.astype(BF16)
    ssm_out = ssm_w_out.astype(BF16)
    w_gate = ffn_w_gate.astype(BF16)
    w_up = ffn_w_up.astype(BF16)
    w_down = ffn_w_down.astype(BF16)

    gdn_cw_t = jnp.swapaxes(gdn_conv_w, 1, 2)
    ssm_cw_t = jnp.swapaxes(ssm_conv_w, 1, 2)
    dskip = jnp.repeat(ssm_d, SSM_HEADDIM, axis=-1)
    cos_p, sin_p = _rope_tables(jnp.arange(seq))
    cos_s, sin_s = _rope_tables(jnp.arange(past_len, past_len + 1))

    ssm_state_rows = state_ssm.reshape(state_ssm.shape[0], bs, SSM_DINNER, SSM_DSTATE)

    def trunk(x, prompt):
        t = x.shape[0]
        ret_l, gdn_l, gconv_l, ssm_l, sconv_l = [], [], [], [], []
        hyb_states, ssm_states = (), ()
        for i in range(depth):
            j = i // 2
            if i % 2 == 0:
                args = (_lane_row(gdn_a_log[j]), _lane_row(gdn_dt_bias[j]),
                        ret_gn_w[j].reshape(1, 512), gdn_norm_w[j].reshape(1, LANES))
                if prompt:
                    rope = pl.BlockSpec((_row_tile(t), LANES), lambda r: (r % (seq // _row_tile(t)), 0))
                    pm, pt, cb = _prompt_proj(
                        _hyb_proj_kernel, x, norm_mix_w[i], hyb_in, j, HYB_MAIN, hyb_tail[j],
                        GDN_CONV_CH, seq, (cos_p, sin_p, gdn_cw_t[j]),
                        [rope, rope, _resident((CONV_W, GDN_CONV_CH))])
                    mixed, rs, gs = _hyb_core_prompt(pm, pt, *args, bp, seq)
                    cb = cb[:, SUBLANES - (CONV_W - 1):, :]
                    ret_l.append(rs)
                    gdn_l.append(gs)
                else:
                    pm, pt = _norm_proj(x, norm_mix_w[i], hyb_in, j, HYB_MAIN, hyb_tail[j])
                    mixed, cb, *hyb_states = _hyb_core_step(
                        pm, pt, state_gdn_conv[j].reshape(t, -1), state_ret, state_gdn, j,
                        hyb_states, cos_s, sin_s, gdn_cw_t[j], *args)
                    cb = cb.reshape(t, CONV_W - 1, GDN_CONV_CH)
                gconv_l.append(cb)
                w_mix = hyb_out
            else:
                args = (_lane_row(ssm_a_log[j]), _lane_row(ssm_dt_bias[j]), dskip[j].reshape(1, -1),
                        ssm_norm_w[j].reshape(1, -1))
                conv_args = (ssm_cw_t[j], ssm_conv_b[j].reshape(1, -1))
                if prompt:
                    pm, pt, cb = _prompt_proj(
                        _ssm_proj_kernel, x, norm_mix_w[i], ssm_in, j, SSM_MAIN, ssm_tail[j],
                        SSM_CONV_CH, seq, conv_args,
                        [_resident((CONV_W, SSM_CONV_CH)), _resident((1, SSM_CONV_CH))])
                    y, ss = _ssm_core_prompt(pm, pt, *args, bp, seq)
                    cb = cb[:, SUBLANES - (CONV_W - 1):, :]
                    ssm_l.append(ss.reshape(-1, SSM_HEADS, SSM_HEADDIM, SSM_DSTATE))
                else:
                    pm, pt = _norm_proj(x, norm_mix_w[i], ssm_in, j, SSM_MAIN, ssm_tail[j])
                    y, cb, *ssm_states = _ssm_core_step(
                        pm, pt, state_ssm_conv[j].reshape(t, -1), ssm_state_rows, j, ssm_states,
                        *conv_args, *args)
                    cb = cb.reshape(t, CONV_W - 1, SSM_CONV_CH)
                sconv_l.append(cb)
                mixed, w_mix = y, ssm_out
            x = _mix_ffn(x, mixed, w_mix, j, norm_ffn_w[i], w_gate, w_up, w_down, i, norm_final_w,
                         final_norm=(i == depth - 1))
        if prompt:
            ret_new, gdn_new, ssm_new = jnp.stack(ret_l), jnp.stack(gdn_l), jnp.stack(ssm_l)
        else:
            ret_new, gdn_new = hyb_states
            ssm_new = ssm_states[0].reshape(state_ssm.shape)
        return x, ret_new, gdn_new, jnp.stack(gconv_l), ssm_new, jnp.stack(sconv_l)

    yp, ret_p, gdn_p, gconv_p, ssm_p, sconv_p = trunk(x_prompt.reshape(bp * seq, d), True)
    ys, ret_s, gdn_s, gconv_s, ssm_s, sconv_s = trunk(x_sample.reshape(bs, d), False)
    return (yp.reshape(bp, seq, d), ys.reshape(bs, 1, d),
            ret_p, gdn_p, gconv_p, ssm_p, sconv_p,
            ret_s, gdn_s, gconv_s, ssm_s, sconv_s)
```

```python
import functools
import math

import jax
import jax.numpy as jnp
from jax import lax
from jax.experimental import pallas as pl
from jax.experimental.pallas import tpu as pltpu

F32 = jnp.float32
BF16 = jnp.bfloat16

D_MODEL = 1024
N_HEADS = 4
HEAD_DIM = 128
HYB_MAIN = 4096
GDN_CONV_CH = 1536
SSM_DINNER = 2048
SSM_HEADS = 32
SSM_HEADDIM = 64
SSM_GROUPS = 4
SSM_DSTATE = 128
SSM_GROUP_W = SSM_DINNER // SSM_GROUPS
SSM_CONV_CH = 3072
SSM_MAIN = 5120
D_FF = 2816
FF_CHUNK = 256
CONV_W = 4
EPS = 1e-6
ROPE_BASE = 10000.0

LANES = 128
SUBLANES = 8
VMEM_LIMIT = 56 * 1024 * 1024

PROJ_PIECE_LANES = 256
SEQ_TILE = 256
SEQS_PER_STEP = 2
CHUNK = 64
STEP_BATCH = 8
LOG_GAMMA = tuple(math.log1p(-2.0 ** (-5.0 - h)) for h in range(N_HEADS))

_NT = (((1,), (1,)), ((), ()))
_TN = (((0,), (0,)), ((), ()))


def _dot(a, b):
    return jnp.dot(a, b, preferred_element_type=F32)


def _dot_nt(a, b):
    return lax.dot_general(a, b, _NT, preferred_element_type=F32)


def _dot_tn(a, b):
    return lax.dot_general(a, b, _TN, preferred_element_type=F32)


def _dot_exact(a, b):
    return jnp.dot(a, b, preferred_element_type=F32, precision=lax.Precision.HIGHEST)


def _silu(x):
    return x / (1.0 + jnp.exp(-x))


def _sigmoid(x):
    return 1.0 / (1.0 + jnp.exp(-x))


def _softplus(x):
    return jnp.maximum(x, 0.0) + jnp.log1p(jnp.exp(-jnp.abs(x)))


def _rmsnorm(x, w):
    return x * lax.rsqrt(jnp.mean(x * x, axis=-1, keepdims=True) + EPS) * w


def _resident(shape):
    nd = len(shape)
    return pl.BlockSpec(shape, lambda *_: (0,) * nd, pipeline_mode=pl.Buffered(1))


def _resident_layer(shape, layer):
    nd = len(shape)
    return pl.BlockSpec((None,) + tuple(shape), lambda *_: (layer,) + (0,) * nd,
                        pipeline_mode=pl.Buffered(1))


def _params(n_axes):
    return pltpu.CompilerParams(dimension_semantics=("arbitrary",) * n_axes,
                                vmem_limit_bytes=VMEM_LIMIT)


def _row_tile(t):
    return 512 if t % 512 == 0 else t


def _norm_proj_kernel(x_ref, nw_ref, w_ref, wt_ref, o_ref, ot_ref, *, n_chunk):
    h = _rmsnorm(x_ref[...], nw_ref[...]).astype(BF16)
    n = w_ref.shape[1]
    for c in range(n // n_chunk):
        sl = slice(c * n_chunk, (c + 1) * n_chunk)
        o_ref[:, sl] = _dot(h, w_ref[:, sl])
    ot_ref[...] = _dot(h, wt_ref[...])


def _norm_proj(x, nw, w_all, layer, n, w_tail):
    t, d = x.shape
    tm = _row_tile(t)
    return pl.pallas_call(
        functools.partial(_norm_proj_kernel, n_chunk=512),
        grid=(t // tm,),
        in_specs=[pl.BlockSpec((tm, d), lambda i: (i, 0)),
                  _resident((1, d)), _resident_layer((d, n), layer), _resident((d, LANES))],
        out_specs=[pl.BlockSpec((tm, n), lambda i: (i, 0)),
                   pl.BlockSpec((tm, LANES), lambda i: (i, 0))],
        out_shape=[jax.ShapeDtypeStruct((t, n), F32), jax.ShapeDtypeStruct((t, LANES), F32)],
        compiler_params=_params(1),
        name="norm_proj",
    )(x, nw.reshape(1, d), w_all, w_tail)


def _causal_conv_block(u, xx, cw_ref, cs):
    n = u.shape[0]
    xx[SUBLANES:SUBLANES + n, cs] = u
    ext = xx[:, cs]
    acc = ext * cw_ref[0:1, cs]
    for tap in range(1, CONV_W):
        acc = pltpu.roll(acc, 1, 0) + ext * cw_ref[tap:tap + 1, cs]
    return acc[SUBLANES:, :]


def _prompt_proj_body(x_ref, nw_ref, w_ref, wt_ref, o_ref, ot_ref, cb_ref, xx, acc_scr, tiles_per_seq,
                      epilogue):
    t = lax.rem(pl.program_id(0), tiles_per_seq)
    tm = x_ref.shape[0]
    piece = PROJ_PIECE_LANES
    per_piece = piece // LANES

    @pl.when(t == 0)
    def _():
        xx[0:SUBLANES, :] = jnp.zeros((SUBLANES, xx.shape[1]), F32)

    h = _rmsnorm(x_ref[...], nw_ref[...]).astype(BF16)
    n_pieces = w_ref.shape[1] // piece
    for p in range(n_pieces):
        acc = _dot(h, w_ref[:, p * piece:(p + 1) * piece])
        for j in range(per_piece):
            blk = p * per_piece + j
            o_ref[:, blk * LANES:(blk + 1) * LANES] = epilogue(
                blk // 4, blk % 4, acc[:, j * LANES:(j + 1) * LANES])
    ot_ref[...] = _dot(h, wt_ref[...])
    xx[0:SUBLANES, :] = xx[tm:tm + SUBLANES, :]

    @pl.when(t == tiles_per_seq - 1)
    def _():
        cb_ref[0] = xx[tm:tm + SUBLANES, :]


def _hyb_proj_kernel(x_ref, nw_ref, w_ref, wt_ref, cos_ref, sin_ref, cw_ref, o_ref, ot_ref, cb_ref,
                     xx, acc_scr, *, tiles_per_seq):
    cos = cos_ref[...]
    sin = sin_ref[...]
    scale = HEAD_DIM ** -0.5

    def epilogue(c, j, blk):
        if c == 0:
            return _rotary(blk, cos, sin)
        if c == 1:
            return _rotary(blk, cos, sin) * scale
        if c == 2:
            return blk
        if c in (3, 7):
            return _silu(blk)
        ch = (c - 4) * 4 + j
        act = _silu(_causal_conv_block(blk, xx, cw_ref, slice(ch * LANES, (ch + 1) * LANES)))
        if c == 4:
            return _l2norm(act) * scale
        if c == 5:
            return _l2norm(act)
        return act

    _prompt_proj_body(x_ref, nw_ref, w_ref, wt_ref, o_ref, ot_ref, cb_ref, xx, acc_scr, tiles_per_seq,
                      epilogue)


def _ssm_proj_kernel(x_ref, nw_ref, w_ref, wt_ref, cw_ref, cbias_ref, o_ref, ot_ref, cb_ref, xx,
                     acc_scr, *, tiles_per_seq):
    def epilogue(c, j, blk):
        if c < 4:
            return _silu(blk)
        ch = (c - 4) * 4 + j
        cs = slice(ch * LANES, (ch + 1) * LANES)
        return _silu(_causal_conv_block(blk, xx, cw_ref, cs) + cbias_ref[:, cs])

    _prompt_proj_body(x_ref, nw_ref, w_ref, wt_ref, o_ref, ot_ref, cb_ref, xx, acc_scr, tiles_per_seq,
                      epilogue)


def _prompt_proj(kernel_fn, x, nw, w_all, layer, n, w_tail, conv_ch, seq, extra, extra_specs):
    t, d = x.shape
    tm = _row_tile(t)
    tiles_per_seq = seq // tm
    return pl.pallas_call(
        functools.partial(kernel_fn, tiles_per_seq=tiles_per_seq),
        grid=(t // tm,),
        in_specs=[pl.BlockSpec((tm, d), lambda i: (i, 0)),
                  _resident((1, d)), _resident_layer((d, n), layer), _resident((d, LANES))] + extra_specs,
        out_specs=[pl.BlockSpec((tm, n), lambda i: (i, 0)),
                   pl.BlockSpec((tm, LANES), lambda i: (i, 0)),
                   pl.BlockSpec((1, SUBLANES, conv_ch), lambda i: (i // tiles_per_seq, 0, 0))],
        out_shape=[jax.ShapeDtypeStruct((t, n), F32), jax.ShapeDtypeStruct((t, LANES), F32),
                   jax.ShapeDtypeStruct((t // seq, SUBLANES, conv_ch), F32)],
        scratch_shapes=[pltpu.VMEM((tm + SUBLANES, conv_ch), F32),
                        pltpu.VMEM((2, tm, PROJ_PIECE_LANES), F32)],
        compiler_params=_params(1),
        name=kernel_fn.__name__.strip("_"),
    )(x, nw.reshape(1, d), w_all, w_tail, *extra)


def _ffn_kernel(x_ref, a_ref, wo_ref, nw_ref, wg_ref, wu_ref, wd_ref, fw_ref, o_ref, act_ref, *,
                final_norm):
    x = x_ref[...] + _dot(a_ref[...].astype(BF16), wo_ref[...])
    h = _rmsnorm(x, nw_ref[...]).astype(BF16)
    for c in range(D_FF // FF_CHUNK):
        sl = slice(c * FF_CHUNK, (c + 1) * FF_CHUNK)
        gate = _dot(h, wg_ref[:, sl])
        up = _dot(h, wu_ref[:, sl])
        act_ref[:, sl] = (_silu(gate) * up).astype(BF16)
    y = x + _dot(act_ref[...], wd_ref[...])
    if final_norm:
        y = _rmsnorm(y, fw_ref[...])
    o_ref[...] = y


def _mix_ffn(x, a, w_out, out_layer, nw, wg, wu, wd, layer, final_w, final_norm):
    t, d = x.shape
    k = a.shape[1]
    tm = _row_tile(t)
    return pl.pallas_call(
        functools.partial(_ffn_kernel, final_norm=final_norm),
        grid=(t // tm,),
        in_specs=[pl.BlockSpec((tm, d), lambda i: (i, 0)),
                  pl.BlockSpec((tm, k), lambda i: (i, 0)),
                  _resident_layer((k, d), out_layer),
                  _resident((1, d)), _resident_layer((d, D_FF), layer),
                  _resident_layer((d, D_FF), layer), _resident_layer((D_FF, d), layer),
                  _resident((1, d))],
        out_specs=pl.BlockSpec((tm, d), lambda i: (i, 0)),
        out_shape=jax.ShapeDtypeStruct((t, d), F32),
        scratch_shapes=[pltpu.VMEM((tm, D_FF), BF16)],
        compiler_params=_params(1),
        name="mix_ffn",
    )(x, a, w_out, nw.reshape(1, d), wg, wu, wd, final_w.reshape(1, d))


def _rotary(x, cos, sin_signed):
    lane = lax.broadcasted_iota(jnp.int32, x.shape, 1)
    partner = jnp.where((lane & 1) == 0, pltpu.roll(x, LANES - 1, 1), pltpu.roll(x, 1, 1))
    return x * cos + partner * sin_signed


def _l2norm(x):
    return x * lax.rsqrt(jnp.sum(x * x, axis=-1, keepdims=True) + EPS)


def _split3(x):
    hi = x.astype(BF16)
    r1 = x - hi.astype(F32)
    mid = r1.astype(BF16)
    lo = (r1 - mid.astype(F32)).astype(BF16)
    return hi, mid, lo


def _pack3(x):
    lane = lax.broadcasted_iota(jnp.int32, x.shape, 1)
    x = jnp.where(lane < SSM_HEADS, x, 0.0)
    hi = x.astype(BF16).astype(F32)
    r1 = x - hi
    mid = r1.astype(BF16).astype(F32)
    lo = r1 - mid
    return (hi + pltpu.roll(mid, SSM_HEADS, 1) + pltpu.roll(lo, 2 * SSM_HEADS, 1)).astype(BF16)


def _unpack3(y):
    return y + pltpu.roll(y, LANES - SSM_HEADS, 1) + pltpu.roll(y, LANES - 2 * SSM_HEADS, 1)


def _expand_heads(x, expand_mat):
    return _dot(_pack3(x), expand_mat)


def _head_expand_matrix():
    row = lax.broadcasted_iota(jnp.int32, (LANES, SSM_DINNER), 0)
    col = lax.broadcasted_iota(jnp.int32, (LANES, SSM_DINNER), 1)
    hit = ((row & (SSM_HEADS - 1)) == (col >> 6)) & (row < 3 * SSM_HEADS)
    return jnp.where(hit, 1.0, 0.0).astype(BF16)


def _chunk_sum_matrices(n):
    ii = lax.broadcasted_iota(jnp.int32, (n, n), 0)
    jj = lax.broadcasted_iota(jnp.int32, (n, n), 1)
    same = (ii >> 6) == (jj >> 6)
    prefix = jnp.where(same & (ii >= jj), 1.0, 0.0).astype(F32)
    total = jnp.where(same, 1.0, 0.0).astype(F32)
    return prefix, total


def _columns(rows):
    pad = jnp.zeros((LANES - rows.shape[0], LANES), F32)
    return jnp.concatenate([rows, pad], axis=0).T


def _unit_lower_inverses(lows, interleaved=()):
    interleaved = list(interleaved)
    n = lows[0].shape[0]
    ii = lax.broadcasted_iota(jnp.int32, (n, n), 0)
    jj = lax.broadcasted_iota(jnp.int32, (n, n), 1)
    eye = (ii == jj).astype(F32)
    pair = (ii >> 1) == (jj >> 1)
    ts = [eye - jnp.where(pair, low, 0.0) for low in lows]
    lows_b = [low.astype(BF16) for low in lows]
    for k in range(1, 6):
        size = 1 << k
        join = ((ii >> (k + 1)) == (jj >> (k + 1))) & ((ii >> k) != (jj >> k))
        join_b = join.astype(F32).astype(BF16)
        ts_b = [t.astype(BF16) for t in ts]
        if size < SUBLANES:
            txs = [_dot(t_b, low_b * join_b).astype(BF16) for t_b, low_b in zip(ts_b, lows_b)]
            ts = [t - _dot(tx, t_b) for t, tx, t_b in zip(ts, txs, ts_b)]
        else:
            lower = [slice(b0 + size, b0 + 2 * size) for b0 in range(0, n, 2 * size)]
            upper = [slice(b0, b0 + size) for b0 in range(0, n, 2 * size)]
            t_low = [jnp.concatenate([t[r] for r in lower], axis=0) for t in ts]
            txs = [_dot(tl.astype(BF16), low_b * join_b).astype(BF16)
                   for tl, low_b in zip(t_low, lows_b)]
            new_lower = [tl - _dot(tx, t_b) for tl, tx, t_b in zip(t_low, txs, ts_b)]
            ts = [jnp.concatenate(
                      [piece for i, r in enumerate(upper)
                       for piece in (t[r], nl[i * size:(i + 1) * size])], axis=0)
                  for t, nl in zip(ts, new_lower)]
        if interleaved:
            interleaved.pop(0)()
        yield
    for fn in interleaved:
        fn()
    return [t - eye for t in ts]


def _hyb_core_kernel(pm_ref, pt_ref, alog_ref, dtb_ref, gnw_ref, gnn_ref,
                     mix_ref, rs_ref, gs_ref, sret, sgdn, intra_scr, *, n_tiles):
    t = pl.program_id(1)
    tl = SEQ_TILE
    hd = HEAD_DIM

    @pl.when(t == 0)
    def _():
        sret[...] = jnp.zeros_like(sret)
        sgdn[...] = jnp.zeros_like(sgdn)

    @pl.when((t == 0) & (pl.program_id(0) == 0))
    def _():
        ii = lax.broadcasted_iota(jnp.int32, (tl, tl), 0)
        jj = lax.broadcasted_iota(jnp.int32, (tl, tl), 1)
        diff = (ii - jj).astype(F32)
        for h in range(N_HEADS):
            intra_scr[h] = jnp.where(diff >= 0.0, jnp.exp(LOG_GAMMA[h] * jnp.maximum(diff, 0.0)), 0.0)

    idx = lax.broadcasted_iota(jnp.int32, (tl, 1), 0).astype(F32)

    def retention_head(h):
        lg = LOG_GAMMA[h]
        q = pm_ref[:, h * hd:(h + 1) * hd]
        k = pm_ref[:, 512 + h * hd:512 + (h + 1) * hd]
        v = pm_ref[:, 1024 + h * hd:1024 + (h + 1) * hd]
        gate_act = pm_ref[:, 1536 + h * hd:1536 + (h + 1) * hd]
        q_dec = jnp.exp(lg * (idx + 1.0))
        k_dec = jnp.exp(lg * (tl - 1.0 - idx))
        s = sret[h]
        qb = q.astype(BF16)
        vb = v.astype(BF16)
        scores = _dot_nt(qb, k.astype(BF16)) * intra_scr[h]
        o = _dot(scores.astype(BF16), vb) + _dot(qb, s.astype(BF16)) * q_dec
        sret[h] = s * math.exp(lg * tl) + _dot_tn((k * k_dec).astype(BF16), vb)
        mu = jnp.mean(o, axis=-1, keepdims=True)
        var = jnp.mean(jnp.square(o - mu), axis=-1, keepdims=True)
        ro = (o - mu) * lax.rsqrt(var + EPS) * gnw_ref[:, h * hd:(h + 1) * hd]
        mix_ref[:, h * hd:(h + 1) * hd] = (gate_act * ro).astype(BF16)

    tail = pt_ref[...]
    g_all = -jnp.exp(alog_ref[...]) * _softplus(tail + dtb_ref[...])
    beta_all = _sigmoid(tail)
    prefix, total = _chunk_sum_matrices(tl)
    g_parts = _pack3(g_all)
    gc_all = _unpack3(_dot(prefix.astype(BF16), g_parts))
    gl_all = _unpack3(_dot(total.astype(BF16), g_parts))
    yield

    n = N_HEADS * CHUNK
    si = lax.broadcasted_iota(jnp.int32, (n, n), 0)
    sj = lax.broadcasted_iota(jnp.int32, (n, n), 1)
    same_head = (si >> 6) == (sj >> 6)
    causal = same_head & (si >= sj)
    strict = same_head & (si > sj)

    def stack(fn):
        return jnp.concatenate([fn(h) for h in range(N_HEADS)], axis=0)

    chunks = []
    for c in range(tl // CHUNK):
        r = slice(c * CHUNK, (c + 1) * CHUNK)
        q_st = stack(lambda h: pm_ref[r, 2048 + h * hd:2048 + (h + 1) * hd])
        k_st = stack(lambda h: pm_ref[r, 2560 + h * hd:2560 + (h + 1) * hd])
        v_st = stack(lambda h: pm_ref[r, 3072 + h * hd:3072 + (h + 1) * hd])
        beta = stack(lambda h: beta_all[r, 4 + h:5 + h])
        gc = stack(lambda h: gc_all[r, h:h + 1])
        gl = stack(lambda h: gl_all[r, h:h + 1])
        gc_col = jnp.broadcast_to(gc, (n, n))
        seg = gc_col - gc_col.T
        decay = jnp.where(causal, jnp.exp(jnp.where(causal, seg, 0.0)), 0.0)
        k_b16 = k_st.astype(BF16)
        qk = _dot_nt(jnp.concatenate([q_st.astype(BF16), k_b16], axis=0), k_b16)
        egc = jnp.exp(gc)
        chunks.append(dict(
            r=r, gl=gl,
            low=jnp.where(strict, qk[n:] * beta * decay, 0.0),
            rhs=jnp.concatenate([v_st * beta, k_st * (beta * egc)], axis=1),
            scores=(qk[:n] * decay).astype(BF16),
            q_in=(q_st * egc).astype(BF16),
            k_out=(k_st * jnp.exp(gl - gc)).astype(BF16)))
        yield
    t_offs = yield from _unit_lower_inverses(
        [ch["low"] for ch in chunks],
        interleaved=[functools.partial(retention_head, h) for h in range(N_HEADS)])
    sols = [ch["rhs"] + _dot(t_off.astype(BF16), ch["rhs"].astype(BF16))
            for ch, t_off in zip(chunks, t_offs)]
    yield

    for ch, sol in zip(chunks, sols):
        r = ch["r"]
        w_b16 = sol[:, hd:].astype(BF16)
        v_new = []
        o_inter = []
        for h in range(N_HEADS):
            hr = slice(h * CHUNK, (h + 1) * CHUNK)
            lhs = jnp.concatenate([w_b16[hr], ch["q_in"][hr]], axis=0)
            prod = _dot(lhs, sgdn[h].astype(BF16))
            v_new.append(sol[hr, :hd] - prod[:CHUNK])
            o_inter.append(prod[CHUNK:])
        yield
        vn_b16 = jnp.concatenate(v_new, axis=0).astype(BF16)
        o_st = jnp.concatenate(o_inter, axis=0) + _dot(ch["scores"], vn_b16)
        yield
        for h in range(N_HEADS):
            hr = slice(h * CHUNK, (h + 1) * CHUNK)
            s_dec = jnp.exp(jnp.broadcast_to(ch["gl"][hr], (CHUNK, hd)))
            s_dec = jnp.concatenate([s_dec, s_dec], axis=0)
            sgdn[h] = sgdn[h] * s_dec + _dot_tn(ch["k_out"][hr], vn_b16[hr])
            o_h = o_st[hr]
            go = o_h * lax.rsqrt(jnp.mean(o_h * o_h, axis=-1, keepdims=True) + EPS) * gnn_ref[...]
            gz_act = pm_ref[r, 3584 + h * hd:3584 + (h + 1) * hd]
            mix_ref[r, 512 + h * hd:512 + (h + 1) * hd] = (go * gz_act).astype(BF16)
        yield

    @pl.when(t == n_tiles - 1)
    def _():
        rs_ref[...] = sret[...]
        gs_ref[...] = sgdn[...]


def _hyb_core_prompt(pm, pt, alog, dtb, gnw, gnn, batch, seq):
    n_tiles = seq // SEQ_TILE
    ns = SEQS_PER_STEP
    tile = lambda b, t: (b, t, 0)
    state = pl.BlockSpec((ns, N_HEADS, HEAD_DIM, HEAD_DIM), lambda b, t: (b, 0, 0, 0))
    per_seq = lambda shape: pltpu.VMEM((ns,) + shape, F32)
    mixed, rs, gs = pl.pallas_call(
        functools.partial(_per_sequence(_hyb_core_kernel, ns, 2, 4), n_tiles=n_tiles),
        grid=(batch // ns, n_tiles),
        in_specs=[pl.BlockSpec((ns, SEQ_TILE, HYB_MAIN), tile),
                  pl.BlockSpec((ns, SEQ_TILE, LANES), tile),
                  _resident((1, LANES)), _resident((1, LANES)),
                  _resident((1, 512)), _resident((1, LANES))],
        out_specs=[pl.BlockSpec((ns, SEQ_TILE, D_MODEL), tile), state, state],
        out_shape=[jax.ShapeDtypeStruct((batch, seq, D_MODEL), BF16),
                   jax.ShapeDtypeStruct((batch, N_HEADS, HEAD_DIM, HEAD_DIM), F32),
                   jax.ShapeDtypeStruct((batch, N_HEADS, HEAD_DIM, HEAD_DIM), F32)],
        scratch_shapes=[per_seq((N_HEADS, HEAD_DIM, HEAD_DIM)),
                        per_seq((N_HEADS, HEAD_DIM, HEAD_DIM)),
                        per_seq((N_HEADS, SEQ_TILE, SEQ_TILE))],
        compiler_params=_params(2),
        name="hyb_core_prompt",
    )(pm.reshape(batch, seq, HYB_MAIN), pt.reshape(batch, seq, LANES), alog, dtb, gnw, gnn)
    return mixed.reshape(batch * seq, D_MODEL), rs, gs


def _group_rmsnorm_gate(y, z, nw_ref, o_ref, rows):
    for g in range(SSM_GROUPS):
        sl = slice(g * SSM_GROUP_W, (g + 1) * SSM_GROUP_W)
        yg = y[:, sl] * _silu(z[:, sl])
        yg = yg * lax.rsqrt(jnp.mean(yg * yg, axis=-1, keepdims=True) + EPS)
        o_ref[rows, sl] = (yg * nw_ref[:, sl]).astype(o_ref.dtype)


def _ssm_core_kernel(pm_ref, pt_ref, alog_ref, dtb_ref, dskip_ref, nw_ref,
                     y_ref, st_ref, s_t, xdt, xdec, eacs, acs_e, y_scr, *, n_tiles):
    t = pl.program_id(1)
    tl = SEQ_TILE
    xbc = pm_ref.at[:, SSM_DINNER:SSM_MAIN]

    @pl.when(t == 0)
    def _():
        s_t[...] = jnp.zeros_like(s_t)

    n_chunks = tl // CHUNK
    expand = _head_expand_matrix()
    dt = _softplus(pt_ref[...] + dtb_ref[...])
    a = dt * (-jnp.exp(alog_ref[...]))
    prefix, total = _chunk_sum_matrices(tl)
    a_parts = _pack3(a)
    acs = _unpack3(_dot(prefix.astype(BF16), a_parts))
    atot = _unpack3(_dot(total.astype(BF16), a_parts))
    dt_parts = _pack3(dt)
    dec_parts = _pack3(jnp.exp(atot - acs))
    eacs_parts = _pack3(jnp.exp(acs))
    acs_parts = _pack3(acs)
    chunk_rows = [atot[c * CHUNK:c * CHUNK + 1] for c in range(n_chunks)]
    chunk_rows.append(jnp.zeros((SUBLANES - n_chunks, LANES), F32))
    cdec_parts = _pack3(jnp.exp(jnp.concatenate(chunk_rows, axis=0)))

    def expand_group(parts, g):
        return _dot(parts, expand[:, g * SSM_GROUP_W:(g + 1) * SSM_GROUP_W])

    yield
    chunk_dec = []
    for g in range(SSM_GROUPS):
        gs = slice(g * SSM_GROUP_W, (g + 1) * SSM_GROUP_W)
        x_dt = xbc[:, gs] * expand_group(dt_parts, g)
        xdt[:, gs] = x_dt
        yield
        xdec[:, gs] = (x_dt * expand_group(dec_parts, g)).astype(BF16)
        yield
        eacs[:, gs] = expand_group(eacs_parts, g)
        yield
        acs_e[:, gs] = expand_group(acs_parts, g)
        chunk_dec.append(expand_group(cdec_parts, g))
        yield

    lane = lax.broadcasted_iota(jnp.int32, (CHUNK, LANES), 1)
    row_i = lax.broadcasted_iota(jnp.int32, (CHUNK, LANES), 0)
    first_half = lane < SSM_HEADDIM
    causal2 = row_i >= (lane & (CHUNK - 1))
    lane2 = lax.broadcasted_iota(jnp.int32, (2 * CHUNK, LANES), 1)
    row2 = lax.broadcasted_iota(jnp.int32, (2 * CHUNK, LANES), 0)
    pair_diag = (row2 < CHUNK) == (lane2 < SSM_HEADDIM)

    def bc_cc(r, g):
        bc = xbc[r, SSM_DINNER + g * LANES:SSM_DINNER + (g + 1) * LANES]
        cc = xbc[r, SSM_DINNER + 512 + g * LANES:SSM_DINNER + 512 + (g + 1) * LANES]
        return bc.astype(BF16), cc.astype(BF16)

    for c in range(n_chunks):
        r = slice(c * CHUNK, (c + 1) * CHUNK)
        acs_c = acs[r]
        acs_t = jnp.concatenate([acs_c, acs_c], axis=0).T
        cb2 = []
        for g in range(SSM_GROUPS):
            bc_b16, cc_b16 = bc_cc(r, g)
            cb2.append(_dot_nt(cc_b16, jnp.concatenate([bc_b16, bc_b16], axis=0)))
        yield
        for p in range(SSM_HEADS // 2):
            ha = 2 * p
            rowv = jnp.where(first_half, acs_t[ha:ha + 1, :], acs_t[ha + 1:ha + 2, :])
            seg = acs_e[r, p * LANES:(p + 1) * LANES] - rowv
            lmat = jnp.where(causal2, jnp.exp(jnp.where(causal2, seg, 0.0)), 0.0)
            m = (cb2[p // 4] * lmat).astype(BF16)
            xblk = xdt[r, p * LANES:(p + 1) * LANES]
            xblk2 = jnp.concatenate([xblk, xblk], axis=0)
            x_bd = jnp.where(pair_diag, xblk2, 0.0).astype(BF16)
            y_scr[r, p * LANES:(p + 1) * LANES] = _dot(m, x_bd)
            yield

    for c in range(n_chunks):
        r = slice(c * CHUNK, (c + 1) * CHUNK)
        groups = range(SSM_GROUPS)
        gsl = [slice(g * SSM_GROUP_W, (g + 1) * SSM_GROUP_W) for g in groups]
        bcs, ccs = zip(*[bc_cc(r, g) for g in groups])
        s_old = [s_t[:, gsl[g]] for g in groups]
        y_inter = [_dot(ccs[g], s_old[g].astype(BF16)) for g in groups]
        yield
        s_add = [_dot_tn(bcs[g], xdec[r, gsl[g]]) for g in groups]
        yield
        for g in groups:
            s_t[:, gsl[g]] = s_old[g] * chunk_dec[g][c:c + 1, :] + s_add[g]
            y_scr[r, gsl[g]] = y_scr[r, gsl[g]] + y_inter[g] * eacs[r, gsl[g]]
        yield

    for g in range(SSM_GROUPS):
        gs = slice(g * SSM_GROUP_W, (g + 1) * SSM_GROUP_W)
        yg = (y_scr[:, gs] + xbc[:, gs] * dskip_ref[:, gs]) * pm_ref[:, gs]
        yg = yg * lax.rsqrt(jnp.mean(yg * yg, axis=-1, keepdims=True) + EPS)
        y_ref[:, gs] = (yg * nw_ref[:, gs]).astype(y_ref.dtype)
        yield

    @pl.when(t == n_tiles - 1)
    def _():
        for kblk in range(SSM_DINNER // LANES):
            ks = slice(kblk * LANES, (kblk + 1) * LANES)
            st_ref[ks, :] = s_t[:, ks].T


def _per_sequence(body, n_seq, n_tiled_in, n_shared_in):
    def kernel(*refs, **kw):
        tiled = refs[:n_tiled_in]
        shared = refs[n_tiled_in:n_tiled_in + n_shared_in]
        rest = refs[n_tiled_in + n_shared_in:]
        running = [body(*[r.at[s] for r in tiled], *shared, *[r.at[s] for r in rest], **kw)
                   for s in range(n_seq)]
        while running:
            for gen in list(running):
                if next(gen, StopIteration) is StopIteration:
                    running.remove(gen)
    return kernel


def _ssm_core_prompt(pm, pt, alog, dtb, dskip, nw, batch, seq):
    n_tiles = seq // SEQ_TILE
    ns = SEQS_PER_STEP
    tile = lambda b, t: (b, t, 0)
    per_seq = lambda shape, dtype: pltpu.VMEM((ns,) + shape, dtype)
    y, st = pl.pallas_call(
        functools.partial(_per_sequence(_ssm_core_kernel, ns, 2, 4), n_tiles=n_tiles),
        grid=(batch // ns, n_tiles),
        in_specs=[pl.BlockSpec((ns, SEQ_TILE, SSM_MAIN), tile),
                  pl.BlockSpec((ns, SEQ_TILE, LANES), tile),
                  _resident((1, LANES)), _resident((1, LANES)),
                  _resident((1, SSM_DINNER)), _resident((1, SSM_DINNER))],
        out_specs=[pl.BlockSpec((ns, SEQ_TILE, SSM_DINNER), tile),
                   pl.BlockSpec((ns, SSM_DINNER, SSM_DSTATE), lambda b, t: (b, 0, 0))],
        out_shape=[jax.ShapeDtypeStruct((batch, seq, SSM_DINNER), BF16),
                   jax.ShapeDtypeStruct((batch, SSM_DINNER, SSM_DSTATE), F32)],
        scratch_shapes=[per_seq((SSM_DSTATE, SSM_DINNER), F32),
                        per_seq((SEQ_TILE, SSM_DINNER), F32),
                        per_seq((SEQ_TILE, SSM_DINNER), BF16),
                        per_seq((SEQ_TILE, SSM_DINNER), F32),
                        per_seq((SEQ_TILE, SSM_DINNER), F32),
                        per_seq((SEQ_TILE, SSM_DINNER), F32)],
        compiler_params=_params(2),
        name="ssm_core_prompt",
    )(pm.reshape(batch, seq, SSM_MAIN), pt.reshape(batch, seq, LANES), alog, dtb, dskip, nw)
    return y.reshape(batch * seq, SSM_DINNER), st


def _hyb_step_front_kernel(pm_ref, conv_ref, cos_ref, sin_ref, cw_ref, o_ref, conv_out):
    hd = HEAD_DIM
    ch = GDN_CONV_CH
    cos = cos_ref[...]
    sin = sin_ref[...]
    for h in range(N_HEADS):
        o_ref[:, h * hd:(h + 1) * hd] = _rotary(pm_ref[:, h * hd:(h + 1) * hd], cos, sin)
        ks = slice(512 + h * hd, 512 + (h + 1) * hd)
        o_ref[:, ks] = _rotary(pm_ref[:, ks], cos, sin) * (hd ** -0.5)
    o_ref[:, 1024:1536] = pm_ref[:, 1024:1536]
    o_ref[:, 1536:2048] = _silu(pm_ref[:, 1536:2048])
    o_ref[:, 3584:4096] = _silu(pm_ref[:, 3584:4096])
    for blk in range(ch // LANES):
        cs = slice(blk * LANES, (blk + 1) * LANES)
        u = pm_ref[:, 2048 + blk * LANES:2048 + (blk + 1) * LANES]
        c0 = conv_ref[:, blk * LANES:(blk + 1) * LANES]
        c1 = conv_ref[:, ch + blk * LANES:ch + (blk + 1) * LANES]
        c2 = conv_ref[:, 2 * ch + blk * LANES:2 * ch + (blk + 1) * LANES]
        conv = c0 * cw_ref[0:1, cs]
        conv = conv + c1 * cw_ref[1:2, cs]
        conv = conv + c2 * cw_ref[2:3, cs]
        conv = conv + u * cw_ref[3:4, cs]
        act = _silu(conv)
        if blk < N_HEADS:
            act = _l2norm(act) * (hd ** -0.5)
        elif blk < 2 * N_HEADS:
            act = _l2norm(act)
        o_ref[:, 2048 + blk * LANES:2048 + (blk + 1) * LANES] = act
        conv_out[:, blk * LANES:(blk + 1) * LANES] = c1
        conv_out[:, ch + blk * LANES:ch + (blk + 1) * LANES] = c2
        conv_out[:, 2 * ch + blk * LANES:2 * ch + (blk + 1) * LANES] = u


def _hyb_step_front(pm, conv_state, cos, sin, cw_t):
    t = pm.shape[0]
    full = lambda a: pl.BlockSpec(a.shape, lambda i: (0,) * a.ndim)
    return pl.pallas_call(
        _hyb_step_front_kernel,
        grid=(1,),
        in_specs=[full(pm), full(conv_state), full(cos), full(sin), full(cw_t)],
        out_specs=[full(pm), full(conv_state)],
        out_shape=[jax.ShapeDtypeStruct(pm.shape, F32), jax.ShapeDtypeStruct(conv_state.shape, F32)],
        compiler_params=_params(1),
        name="hyb_step_front",
    )(pm, conv_state, cos, sin, cw_t)


def _hyb_step_kernel(pm_ref, pt_ref, rs_in, gs_in, alog_ref, dtb_ref, gnw_ref, gnn_ref,
                     mix_ref, rs_out, gs_out, o_scr):
    bb = STEP_BATCH
    hd = HEAD_DIM
    tail = pt_ref[...]
    g_all = -jnp.exp(alog_ref[...]) * _softplus(tail + dtb_ref[...])
    beta_all = _sigmoid(tail)

    for h in range(N_HEADS):
        q = pm_ref[:, h * hd:(h + 1) * hd]
        k = pm_ref[:, 512 + h * hd:512 + (h + 1) * hd]
        v = pm_ref[:, 1024 + h * hd:1024 + (h + 1) * hd]
        k_cols = _columns(k)
        gamma = math.exp(LOG_GAMMA[h])
        for b in range(bb):
            s = rs_in[b, h] * gamma + k_cols[:, b:b + 1] * v[b:b + 1, :]
            rs_out[b, h] = s
            o_scr[b:b + 1, h * hd:(h + 1) * hd] = _dot(q, s)[b:b + 1, :]
        gq = pm_ref[:, 2048 + h * hd:2048 + (h + 1) * hd]
        gk = pm_ref[:, 2560 + h * hd:2560 + (h + 1) * hd]
        gv = pm_ref[:, 3072 + h * hd:3072 + (h + 1) * hd]
        k_cols = _columns(gk)
        decay = jnp.broadcast_to(jnp.exp(g_all[:, h:h + 1]), (bb, hd))
        beta = jnp.broadcast_to(beta_all[:, 4 + h:5 + h], (bb, hd))
        for b in range(bb):
            s = gs_in[b, h] * decay[b:b + 1, :]
            delta = (gv[b:b + 1, :] - _dot(gk, s)[b:b + 1, :]) * beta[b:b + 1, :]
            s = s + k_cols[:, b:b + 1] * delta
            gs_out[b, h] = s
            o_scr[b:b + 1, 512 + h * hd:512 + (h + 1) * hd] = _dot(gq, s)[b:b + 1, :]

    for h in range(N_HEADS):
        o = o_scr[:, h * hd:(h + 1) * hd]
        mu = jnp.mean(o, axis=-1, keepdims=True)
        var = jnp.mean(jnp.square(o - mu), axis=-1, keepdims=True)
        ro = (o - mu) * lax.rsqrt(var + EPS) * gnw_ref[:, h * hd:(h + 1) * hd]
        mix_ref[:, h * hd:(h + 1) * hd] = pm_ref[:, 1536 + h * hd:1536 + (h + 1) * hd] * ro
        go = o_scr[:, 512 + h * hd:512 + (h + 1) * hd]
        go = go * lax.rsqrt(jnp.mean(go * go, axis=-1, keepdims=True) + EPS) * gnn_ref[...]
        mix_ref[:, 512 + h * hd:512 + (h + 1) * hd] = go * pm_ref[:, 3584 + h * hd:3584 + (h + 1) * hd]


def _layer_chained_outputs(state_arrays, layer, partial):
    shapes = [jax.ShapeDtypeStruct(s.shape, s.dtype) for s in state_arrays]
    if layer == 0:
        return [], [], shapes
    return list(partial), [pl.BlockSpec(memory_space=pl.ANY)] * len(partial), shapes


def _hyb_core_step(pm_act, pt, ret_all, gdn_all, layer, partial, alog, dtb, gnw, gnn):
    batch = pm_act.shape[0]
    bb = STEP_BATCH
    row = lambda i: (i, 0)
    state = pl.BlockSpec((None, bb, N_HEADS, HEAD_DIM, HEAD_DIM), lambda i: (layer, i, 0, 0, 0))
    extra, extra_specs, state_shapes = _layer_chained_outputs((ret_all, gdn_all), layer, partial)
    n_in = 8
    aliases = {n_in + k: 1 + k for k in range(len(extra))}

    def body(*refs):
        _hyb_step_kernel(*refs[:n_in], *refs[n_in + len(extra):])

    return pl.pallas_call(
        body,
        grid=(batch // bb,),
        in_specs=[pl.BlockSpec((bb, HYB_MAIN), row), pl.BlockSpec((bb, LANES), row), state, state,
                  _resident((1, LANES)), _resident((1, LANES)),
                  _resident((1, 512)), _resident((1, LANES))] + extra_specs,
        out_specs=[pl.BlockSpec((bb, D_MODEL), row), state, state],
        out_shape=[jax.ShapeDtypeStruct((batch, D_MODEL), F32)] + state_shapes,
        scratch_shapes=[pltpu.VMEM((bb, D_MODEL), F32)],
        input_output_aliases=aliases,
        compiler_params=_params(1),
        name="hyb_core_step",
    )(pm_act, pt, ret_all, gdn_all, alog, dtb, gnw, gnn, *extra)


def _ssm_step_kernel(pm_ref, pt_ref, conv_ref, s_in, cw_ref, cbias_ref, alog_ref, dtb_ref,
                     dskip_ref, nw_ref, y_ref, conv_out, s_out, y_scr):
    bb = STEP_BATCH
    ch = SSM_CONV_CH
    u = pm_ref[:, SSM_DINNER:SSM_MAIN]
    c0 = conv_ref[:, 0:ch]
    c1 = conv_ref[:, ch:2 * ch]
    c2 = conv_ref[:, 2 * ch:3 * ch]
    conv = c0 * cw_ref[0:1, :]
    conv = conv + c1 * cw_ref[1:2, :]
    conv = conv + c2 * cw_ref[2:3, :]
    conv = conv + u * cw_ref[3:4, :]
    xbc = _silu(conv + cbias_ref[...])
    conv_out[:, 0:ch] = c1
    conv_out[:, ch:2 * ch] = c2
    conv_out[:, 2 * ch:3 * ch] = u
    xs = xbc[:, 0:SSM_DINNER]
    expand = _head_expand_matrix()
    dt = _softplus(pt_ref[...] + dtb_ref[...])
    a = dt * (-jnp.exp(alog_ref[...]))
    x_dt = xs * _expand_heads(dt, expand)
    decay = jnp.exp(a)
    half = SSM_HEADDIM
    for kblk in range(SSM_DINNER // LANES):
        ks = slice(kblk * LANES, (kblk + 1) * LANES)
        g = kblk // 4
        x_cols = _columns(x_dt[:, ks])
        bm = xbc[:, SSM_DINNER + g * LANES:SSM_DINNER + (g + 1) * LANES]
        cm = xbc[:, SSM_DINNER + 512 + g * LANES:SSM_DINNER + 512 + (g + 1) * LANES]
        for b in range(bb):
            d0 = decay[b:b + 1, 2 * kblk:2 * kblk + 1]
            d1 = decay[b:b + 1, 2 * kblk + 1:2 * kblk + 2]
            s_dec = jnp.concatenate([s_in[b, kblk * LANES:kblk * LANES + half, :] * d0,
                                     s_in[b, kblk * LANES + half:(kblk + 1) * LANES, :] * d1], axis=0)
            s = s_dec + x_cols[:, b:b + 1] * bm[b:b + 1, :]
            s_out[b, ks, :] = s
            y_scr[b:b + 1, ks] = _dot_nt(cm, s)[b:b + 1, :]
    y = y_scr[...] + xs * dskip_ref[...]
    _group_rmsnorm_gate(y, pm_ref[:, 0:SSM_DINNER], nw_ref, y_ref, slice(None))


def _ssm_core_step(pm, pt, conv_state, ssm_all, layer, partial, cw_t, cbias, alog, dtb, dskip, nw):
    batch = pm.shape[0]
    bb = STEP_BATCH
    row = lambda i: (i, 0)
    state = pl.BlockSpec((None, bb, SSM_DINNER, SSM_DSTATE), lambda i: (layer, i, 0, 0))
    extra, extra_specs, state_shapes = _layer_chained_outputs((ssm_all,), layer, partial)
    n_in = 10
    aliases = {n_in + k: 2 + k for k in range(len(extra))}

    def body(*refs):
        _ssm_step_kernel(*refs[:n_in], *refs[n_in + len(extra):])

    return pl.pallas_call(
        body,
        grid=(batch // bb,),
        in_specs=[pl.BlockSpec((bb, SSM_MAIN), row), pl.BlockSpec((bb, LANES), row),
                  pl.BlockSpec((bb, 3 * SSM_CONV_CH), row), state,
                  _resident((CONV_W, SSM_CONV_CH)), _resident((1, SSM_CONV_CH)),
                  _resident((1, LANES)), _resident((1, LANES)),
                  _resident((1, SSM_DINNER)), _resident((1, SSM_DINNER))] + extra_specs,
        out_specs=[pl.BlockSpec((bb, SSM_DINNER), row), pl.BlockSpec((bb, 3 * SSM_CONV_CH), row),
                   state],
        out_shape=[jax.ShapeDtypeStruct((batch, SSM_DINNER), F32),
                   jax.ShapeDtypeStruct((batch, 3 * SSM_CONV_CH), F32)] + state_shapes,
        scratch_shapes=[pltpu.VMEM((bb, SSM_DINNER), F32)],
        input_output_aliases=aliases,
        compiler_params=_params(1),
        name="ssm_core_step",
    )(pm, pt, conv_state, ssm_all, cw_t, cbias, alog, dtb, dskip, nw, *extra)


def _rope_tables(pos):
    inv_freq = 1.0 / (ROPE_BASE ** jnp.linspace(0.0, 1.0, HEAD_DIM // 2, dtype=F32))
    ang = pos.astype(F32)[:, None] * inv_freq[None, :]
    cos = jnp.repeat(jnp.cos(ang), 2, axis=-1)
    sign = jnp.tile(jnp.array([-1.0, 1.0], F32), HEAD_DIM // 2)
    sin = jnp.repeat(jnp.sin(ang), 2, axis=-1) * sign
    return cos, sin


def _lane_row(v):
    return jnp.pad(v.astype(F32), (0, LANES - v.shape[0])).reshape(1, LANES)


def kernel(x_prompt, x_sample, state_ret, state_gdn, state_gdn_conv, state_ssm, state_ssm_conv,
           norm_mix_w, norm_ffn_w, norm_final_w, hyb_w_in, hyb_w_out, ret_gn_w, gdn_conv_w,
           gdn_a_log, gdn_dt_bias, gdn_norm_w, ssm_w_in, ssm_conv_w, ssm_conv_b, ssm_dt_bias,
           ssm_a_log, ssm_d, ssm_norm_w, ssm_w_out, ffn_w_gate, ffn_w_up, ffn_w_down):
    depth = norm_mix_w.shape[0]
    bp, seq, d = x_prompt.shape
    bs = x_sample.shape[0]
    past_len = 16384

    hyb_in = hyb_w_in[:, :, :HYB_MAIN].astype(BF16)
    hyb_tail = jnp.pad(hyb_w_in[:, :, HYB_MAIN:], ((0, 0), (0, 0), (0, LANES - 2 * N_HEADS))).astype(BF16)
    ssm_in = ssm_w_in[:, :, :SSM_MAIN].astype(BF16)
    ssm_tail = jnp.pad(ssm_w_in[:, :, SSM_MAIN:], ((0, 0), (0, 0), (0, LANES - SSM_HEADS))).astype(BF16)
    hyb_out = hyb_w_out.astype(BF16)
    ssm_out = ssm_w_out.astype(BF16)
    w_gate = ffn_w_gate.astype(BF16)
    w_up = ffn_w_up.astype(BF16)
    w_down = ffn_w_down.astype(BF16)

    gdn_cw_t = jnp.swapaxes(gdn_conv_w, 1, 2)
    ssm_cw_t = jnp.swapaxes(ssm_conv_w, 1, 2)
    dskip = jnp.repeat(ssm_d, SSM_HEADDIM, axis=-1)
    cos_p, sin_p = _rope_tables(jnp.arange(seq))
    cos_s, sin_s = _rope_tables(jnp.arange(past_len, past_len + 1))

    ssm_state_rows = state_ssm.reshape(state_ssm.shape[0], bs, SSM_DINNER, SSM_DSTATE)

    def trunk(x, prompt):
        t = x.shape[0]
        ret_l, gdn_l, gconv_l, ssm_l, sconv_l = [], [], [], [], []
        hyb_states, ssm_states = (), ()
        for i in range(depth):
            j = i // 2
            if i % 2 == 0:
                args = (_lane_row(gdn_a_log[j]), _lane_row(gdn_dt_bias[j]),
                        ret_gn_w[j].reshape(1, 512), gdn_norm_w[j].reshape(1, LANES))
                if prompt:
                    rope = pl.BlockSpec((_row_tile(t), LANES), lambda r: (r % (seq // _row_tile(t)), 0))
                    pm, pt, cb = _prompt_proj(
                        _hyb_proj_kernel, x, norm_mix_w[i], hyb_in, j, HYB_MAIN, hyb_tail[j],
                        GDN_CONV_CH, seq, (cos_p, sin_p, gdn_cw_t[j]),
                        [rope, rope, _resident((CONV_W, GDN_CONV_CH))])
                    mixed, rs, gs = _hyb_core_prompt(pm, pt, *args, bp, seq)
                    cb = cb[:, SUBLANES - (CONV_W - 1):, :]
                    ret_l.append(rs)
                    gdn_l.append(gs)
                else:
                    pm, pt = _norm_proj(x, norm_mix_w[i], hyb_in, j, HYB_MAIN, hyb_tail[j])
                    pm_act, cb = _hyb_step_front(pm, state_gdn_conv[j].reshape(t, -1), cos_s, sin_s,
                                                 gdn_cw_t[j])
                    mixed, *hyb_states = _hyb_core_step(pm_act, pt, state_ret, state_gdn, j,
                                                        hyb_states, *args)
                    cb = cb.reshape(t, CONV_W - 1, GDN_CONV_CH)
                gconv_l.append(cb)
                w_mix = hyb_out
            else:
                args = (_lane_row(ssm_a_log[j]), _lane_row(ssm_dt_bias[j]), dskip[j].reshape(1, -1),
                        ssm_norm_w[j].reshape(1, -1))
                conv_args = (ssm_cw_t[j], ssm_conv_b[j].reshape(1, -1))
                if prompt:
                    pm, pt, cb = _prompt_proj(
                        _ssm_proj_kernel, x, norm_mix_w[i], ssm_in, j, SSM_MAIN, ssm_tail[j],
                        SSM_CONV_CH, seq, conv_args,
                        [_resident((CONV_W, SSM_CONV_CH)), _resident((1, SSM_CONV_CH))])
                    y, ss = _ssm_core_prompt(pm, pt, *args, bp, seq)
                    cb = cb[:, SUBLANES - (CONV_W - 1):, :]
                    ssm_l.append(ss.reshape(-1, SSM_HEADS, SSM_HEADDIM, SSM_DSTATE))
                else:
                    pm, pt = _norm_proj(x, norm_mix_w[i], ssm_in, j, SSM_MAIN, ssm_tail[j])
                    y, cb, *ssm_states = _ssm_core_step(
                        pm, pt, state_ssm_conv[j].reshape(t, -1), ssm_state_rows, j, ssm_states,
                        *conv_args, *args)
                    cb = cb.reshape(t, CONV_W - 1, SSM_CONV_CH)
                sconv_l.append(cb)
                mixed, w_mix = y, ssm_out
            x = _mix_ffn(x, mixed, w_mix, j, norm_ffn_w[i], w_gate, w_up, w_down, i, norm_final_w,
                         final_norm=(i == depth - 1))
        if prompt:
            ret_new, gdn_new, ssm_new = jnp.stack(ret_l), jnp.stack(gdn_l), jnp.stack(ssm_l)
        else:
            ret_new, gdn_new = hyb_states
            ssm_new = ssm_states[0].reshape(state_ssm.shape)
        return x, ret_new, gdn_new, jnp.stack(gconv_l), ssm_new, jnp.stack(sconv_l)

    yp, ret_p, gdn_p, gconv_p, ssm_p, sconv_p = trunk(x_prompt.reshape(bp * seq, d), True)
    ys, ret_s, gdn_s, gconv_s, ssm_s, sconv_s = trunk(x_sample.reshape(bs, d), False)
    return (yp.reshape(bp, seq, d), ys.reshape(bs, 1, d),
            ret_p, gdn_p, gconv_p, ssm_p, sconv_p,
            ret_s, gdn_s, gconv_s, ssm_s, sconv_s)
```

```python
import functools
import math

import jax
import jax.numpy as jnp
from jax import lax
from jax.experimental import pallas as pl
from jax.experimental.pallas import tpu as pltpu

F32 = jnp.float32
BF16 = jnp.bfloat16

D_MODEL = 1024
N_HEADS = 4
HEAD_DIM = 128
HEADS_W = N_HEADS * HEAD_DIM
HYB_MAIN = 8 * HEADS_W
RET_Q, RET_K, RET_V, RET_GATE, GDN_Q, GDN_K, GDN_V, GDN_GATE = (i * HEADS_W for i in range(8))
GDN_CONV_CH = 1536
SSM_DINNER = 2048
SSM_HEADS = 32
SSM_HEADDIM = 64
SSM_GROUPS = 4
SSM_DSTATE = 128
SSM_GROUP_W = SSM_DINNER // SSM_GROUPS
SSM_CONV_CH = 3072
SSM_MAIN = 5120
D_FF = 2816
FF_CHUNK = 256
CONV_W = 4
EPS = 1e-6
ROPE_BASE = 10000.0

LANES = 128
SUBLANES = 8
VMEM_LIMIT = 56 * 1024 * 1024

PROJ_PIECE_LANES = 256
FFN_ROW_TILE = 512
SEQ_TILE = 256
SEQS_PER_STEP = 2
CHUNK = 64
STEP_BATCH = 8
LOG_GAMMA = tuple(math.log1p(-2.0 ** (-5.0 - h)) for h in range(N_HEADS))

_NT = (((1,), (1,)), ((), ()))
_TN = (((0,), (0,)), ((), ()))


def _dot(a, b):
    return jnp.dot(a, b, preferred_element_type=F32)


def _dot_nt(a, b):
    return lax.dot_general(a, b, _NT, preferred_element_type=F32)


def _dot_tn(a, b):
    return lax.dot_general(a, b, _TN, preferred_element_type=F32)


def _dot_exact(a, b):
    return jnp.dot(a, b, preferred_element_type=F32, precision=lax.Precision.HIGHEST)


def _silu(x):
    return x / (1.0 + jnp.exp(-x))


def _sigmoid(x):
    return 1.0 / (1.0 + jnp.exp(-x))


def _softplus(x):
    return jnp.maximum(x, 0.0) + jnp.log1p(jnp.exp(-jnp.abs(x)))


def _rmsnorm(x, w):
    return x * lax.rsqrt(jnp.mean(x * x, axis=-1, keepdims=True) + EPS) * w


def _resident(shape):
    nd = len(shape)
    return pl.BlockSpec(shape, lambda *_: (0,) * nd, pipeline_mode=pl.Buffered(1))


def _resident_layer(shape, layer):
    nd = len(shape)
    return pl.BlockSpec((None,) + tuple(shape), lambda *_: (layer,) + (0,) * nd,
                        pipeline_mode=pl.Buffered(1))


def _params(n_axes):
    return pltpu.CompilerParams(dimension_semantics=("arbitrary",) * n_axes,
                                vmem_limit_bytes=VMEM_LIMIT)


def _head_cols(base, h):
    return slice(base + h * HEAD_DIM, base + (h + 1) * HEAD_DIM)


def _row_tile(t):
    return 512 if t % 512 == 0 else t


def _norm_proj_kernel(x_ref, nw_ref, w_ref, wt_ref, o_ref, ot_ref, *, n_chunk):
    h = _rmsnorm(x_ref[...], nw_ref[...]).astype(BF16)
    n = w_ref.shape[1]
    for c in range(n // n_chunk):
        sl = slice(c * n_chunk, (c + 1) * n_chunk)
        o_ref[:, sl] = _dot(h, w_ref[:, sl])
    ot_ref[...] = _dot(h, wt_ref[...])


def _norm_proj(x, nw, w_all, layer, n, w_tail):
    t, d = x.shape
    tm = _row_tile(t)
    return pl.pallas_call(
        functools.partial(_norm_proj_kernel, n_chunk=512),
        grid=(t // tm,),
        in_specs=[pl.BlockSpec((tm, d), lambda i: (i, 0)),
                  _resident((1, d)), _resident_layer((d, n), layer), _resident((d, LANES))],
        out_specs=[pl.BlockSpec((tm, n), lambda i: (i, 0)),
                   pl.BlockSpec((tm, LANES), lambda i: (i, 0))],
        out_shape=[jax.ShapeDtypeStruct((t, n), F32), jax.ShapeDtypeStruct((t, LANES), F32)],
        compiler_params=_params(1),
        name="norm_proj",
    )(x, nw.reshape(1, d), w_all, w_tail)


def _causal_conv_block(u, xx, cw_ref, cs):
    n = u.shape[0]
    xx[SUBLANES:SUBLANES + n, cs] = u
    ext = xx[:, cs]
    prev = pltpu.roll(ext, 1, 0)
    newer = ext * cw_ref[3:4, cs] + prev * cw_ref[2:3, cs]
    older = ext * cw_ref[1:2, cs] + prev * cw_ref[0:1, cs]
    return (newer + pltpu.roll(older, 2, 0))[SUBLANES:, :]


def _prompt_proj_body(x_ref, nw_ref, w_ref, wt_ref, o_ref, ot_ref, cb_ref, xx, acc_scr, tiles_per_seq,
                      epilogue):
    t = lax.rem(pl.program_id(0), tiles_per_seq)
    tm = x_ref.shape[0]
    piece = PROJ_PIECE_LANES
    per_piece = piece // LANES

    @pl.when(t == 0)
    def _():
        xx[0:SUBLANES, :] = jnp.zeros((SUBLANES, xx.shape[1]), F32)

    h = _rmsnorm(x_ref[...], nw_ref[...]).astype(BF16)
    n_pieces = w_ref.shape[1] // piece
    for p in range(n_pieces):
        acc = _dot(h, w_ref[:, p * piece:(p + 1) * piece])
        for j in range(per_piece):
            blk = p * per_piece + j
            o_ref[:, blk * LANES:(blk + 1) * LANES] = epilogue(
                blk // 4, blk % 4, acc[:, j * LANES:(j + 1) * LANES])
    ot_ref[...] = _dot(h, wt_ref[...])
    xx[0:SUBLANES, :] = xx[tm:tm + SUBLANES, :]

    @pl.when(t == tiles_per_seq - 1)
    def _():
        cb_ref[0] = xx[tm:tm + SUBLANES, :]


def _hyb_proj_kernel(x_ref, nw_ref, w_ref, wt_ref, cos_ref, sin_ref, cw_ref, o_ref, ot_ref, cb_ref,
                     xx, acc_scr, *, tiles_per_seq):
    cos = cos_ref[...]
    sin = sin_ref[...]
    scale = HEAD_DIM ** -0.5

    def epilogue(c, j, blk):
        if c == 0:
            return _rotary(blk, cos, sin)
        if c == 1:
            return _rotary(blk, cos, sin) * scale
        if c == 2:
            return blk
        if c in (3, 7):
            return _silu(blk)
        ch = (c - 4) * 4 + j
        act = _silu(_causal_conv_block(blk, xx, cw_ref, slice(ch * LANES, (ch + 1) * LANES)))
        if c == 4:
            return _l2norm(act) * scale
        if c == 5:
            return _l2norm(act)
        return act

    _prompt_proj_body(x_ref, nw_ref, w_ref, wt_ref, o_ref, ot_ref, cb_ref, xx, acc_scr, tiles_per_seq,
                      epilogue)


def _ssm_proj_kernel(x_ref, nw_ref, w_ref, wt_ref, cw_ref, cbias_ref, o_ref, ot_ref, cb_ref, xx,
                     acc_scr, *, tiles_per_seq):
    def epilogue(c, j, blk):
        if c < 4:
            return _silu(blk)
        ch = (c - 4) * 4 + j
        cs = slice(ch * LANES, (ch + 1) * LANES)
        return _silu(_causal_conv_block(blk, xx, cw_ref, cs) + cbias_ref[:, cs])

    _prompt_proj_body(x_ref, nw_ref, w_ref, wt_ref, o_ref, ot_ref, cb_ref, xx, acc_scr, tiles_per_seq,
                      epilogue)


def _prompt_proj(kernel_fn, x, nw, w_all, layer, n, w_tail, conv_ch, seq, extra, extra_specs):
    t, d = x.shape
    tm = _row_tile(t)
    tiles_per_seq = seq // tm
    return pl.pallas_call(
        functools.partial(kernel_fn, tiles_per_seq=tiles_per_seq),
        grid=(t // tm,),
        in_specs=[pl.BlockSpec((tm, d), lambda i: (i, 0)),
                  _resident((1, d)), _resident_layer((d, n), layer), _resident((d, LANES))] + extra_specs,
        out_specs=[pl.BlockSpec((tm, n), lambda i: (i, 0)),
                   pl.BlockSpec((tm, LANES), lambda i: (i, 0)),
                   pl.BlockSpec((1, SUBLANES, conv_ch), lambda i: (i // tiles_per_seq, 0, 0))],
        out_shape=[jax.ShapeDtypeStruct((t, n), F32), jax.ShapeDtypeStruct((t, LANES), F32),
                   jax.ShapeDtypeStruct((t // seq, SUBLANES, conv_ch), F32)],
        scratch_shapes=[pltpu.VMEM((tm + SUBLANES, conv_ch), F32),
                        pltpu.VMEM((2, tm, PROJ_PIECE_LANES), F32)],
        compiler_params=_params(1),
        name=kernel_fn.__name__.strip("_"),
    )(x, nw.reshape(1, d), w_all, w_tail, *extra)


def _ffn_kernel(x_ref, a_ref, wo_ref, nw_ref, wg_ref, wu_ref, wd_ref, fw_ref, o_ref, act_ref, *,
                final_norm):
    x = x_ref[...] + _dot(a_ref[...].astype(BF16), wo_ref[...])
    h = _rmsnorm(x, nw_ref[...]).astype(BF16)
    for c in range(D_FF // FF_CHUNK):
        sl = slice(c * FF_CHUNK, (c + 1) * FF_CHUNK)
        gate = _dot(h, wg_ref[:, sl])
        up = _dot(h, wu_ref[:, sl])
        act_ref[:, sl] = (_silu(gate) * up).astype(BF16)
    y = x + _dot(act_ref[...], wd_ref[...])
    if final_norm:
        y = _rmsnorm(y, fw_ref[...])
    o_ref[...] = y


def _mix_ffn(x, a, w_out, out_layer, nw, wg, wu, wd, layer, final_w, final_norm):
    t, d = x.shape
    k = a.shape[1]
    tm = FFN_ROW_TILE if t % FFN_ROW_TILE == 0 else t
    return pl.pallas_call(
        functools.partial(_ffn_kernel, final_norm=final_norm),
        grid=(t // tm,),
        in_specs=[pl.BlockSpec((tm, d), lambda i: (i, 0)),
                  pl.BlockSpec((tm, k), lambda i: (i, 0)),
                  _resident_layer((k, d), out_layer),
                  _resident((1, d)), _resident_layer((d, D_FF), layer),
                  _resident_layer((d, D_FF), layer), _resident_layer((D_FF, d), layer),
                  _resident((1, d))],
        out_specs=pl.BlockSpec((tm, d), lambda i: (i, 0)),
        out_shape=jax.ShapeDtypeStruct((t, d), F32),
        scratch_shapes=[pltpu.VMEM((tm, D_FF), BF16)],
        compiler_params=_params(1),
        name="mix_ffn",
    )(x, a, w_out, nw.reshape(1, d), wg, wu, wd, final_w.reshape(1, d))


def _rotary(x, cos, sin_signed):
    lane = lax.broadcasted_iota(jnp.int32, x.shape, 1)
    partner = jnp.where((lane & 1) == 0, pltpu.roll(x, LANES - 1, 1), pltpu.roll(x, 1, 1))
    return x * cos + partner * sin_signed


def _l2norm(x):
    return x * lax.rsqrt(jnp.sum(x * x, axis=-1, keepdims=True) + EPS)


def _split3(x):
    hi = x.astype(BF16)
    r1 = x - hi.astype(F32)
    mid = r1.astype(BF16)
    lo = (r1 - mid.astype(F32)).astype(BF16)
    return hi, mid, lo


def _pack3(x):
    lane = lax.broadcasted_iota(jnp.int32, x.shape, 1)
    x = jnp.where(lane < SSM_HEADS, x, 0.0)
    hi = x.astype(BF16).astype(F32)
    r1 = x - hi
    mid = r1.astype(BF16).astype(F32)
    lo = r1 - mid
    return (hi + pltpu.roll(mid, SSM_HEADS, 1) + pltpu.roll(lo, 2 * SSM_HEADS, 1)).astype(BF16)


def _unpack3(y):
    return y + pltpu.roll(y, LANES - SSM_HEADS, 1) + pltpu.roll(y, LANES - 2 * SSM_HEADS, 1)


def _expand_heads(x, expand_mat):
    return _dot(_pack3(x), expand_mat)


def _head_expand_matrix():
    row = lax.broadcasted_iota(jnp.int32, (LANES, SSM_DINNER), 0)
    col = lax.broadcasted_iota(jnp.int32, (LANES, SSM_DINNER), 1)
    hit = ((row & (SSM_HEADS - 1)) == col // SSM_HEADDIM) & (row < 3 * SSM_HEADS)
    return jnp.where(hit, 1.0, 0.0).astype(BF16)


def _chunk_sum_matrices(n):
    ii = lax.broadcasted_iota(jnp.int32, (n, n), 0)
    jj = lax.broadcasted_iota(jnp.int32, (n, n), 1)
    same = ii // CHUNK == jj // CHUNK
    prefix = jnp.where(same & (ii >= jj), 1.0, 0.0).astype(F32)
    total = jnp.where(same, 1.0, 0.0).astype(F32)
    return prefix, total


def _columns(rows):
    pad = jnp.zeros((LANES - rows.shape[0], LANES), F32)
    return jnp.concatenate([rows, pad], axis=0).T


def _unit_lower_inverses(lows, interleaved=()):
    interleaved = list(interleaved)
    n = lows[0].shape[0]
    ii = lax.broadcasted_iota(jnp.int32, (n, n), 0)
    jj = lax.broadcasted_iota(jnp.int32, (n, n), 1)
    eye = (ii == jj).astype(F32)
    pair = (ii >> 1) == (jj >> 1)
    ts = [eye - jnp.where(pair, low, 0.0) for low in lows]
    lows_b = [low.astype(BF16) for low in lows]
    for k in range(1, 6):
        size = 1 << k
        join = ((ii >> (k + 1)) == (jj >> (k + 1))) & ((ii >> k) != (jj >> k))
        join_b = join.astype(F32).astype(BF16)
        ts_b = [t.astype(BF16) for t in ts]
        if size < SUBLANES:
            txs = [_dot(t_b, low_b * join_b).astype(BF16) for t_b, low_b in zip(ts_b, lows_b)]
            ts = [t - _dot(tx, t_b) for t, tx, t_b in zip(ts, txs, ts_b)]
        else:
            lower = [slice(b0 + size, b0 + 2 * size) for b0 in range(0, n, 2 * size)]
            upper = [slice(b0, b0 + size) for b0 in range(0, n, 2 * size)]
            t_low = [jnp.concatenate([t[r] for r in lower], axis=0) for t in ts]
            txs = [_dot(tl.astype(BF16), low_b * join_b).astype(BF16)
                   for tl, low_b in zip(t_low, lows_b)]
            new_lower = [tl - _dot(tx, t_b) for tl, tx, t_b in zip(t_low, txs, ts_b)]
            ts = [jnp.concatenate(
                      [piece for i, r in enumerate(upper)
                       for piece in (t[r], nl[i * size:(i + 1) * size])], axis=0)
                  for t, nl in zip(ts, new_lower)]
        if interleaved:
            interleaved.pop(0)()
        yield
    for fn in interleaved:
        fn()
    return [t - eye for t in ts]


def _hyb_core_kernel(pm_ref, pt_ref, alog_ref, dtb_ref, gnw_ref, gnn_ref,
                     mix_ref, rs_ref, gs_ref, sret, sgdn, intra_scr, *, n_tiles):
    t = pl.program_id(1)
    tl = SEQ_TILE
    hd = HEAD_DIM

    @pl.when(t == 0)
    def _():
        sret[...] = jnp.zeros_like(sret)
        sgdn[...] = jnp.zeros_like(sgdn)

    @pl.when((t == 0) & (pl.program_id(0) == 0))
    def _():
        ii = lax.broadcasted_iota(jnp.int32, (tl, tl), 0)
        jj = lax.broadcasted_iota(jnp.int32, (tl, tl), 1)
        diff = (ii - jj).astype(F32)
        for h in range(N_HEADS):
            intra_scr[h] = jnp.where(diff >= 0.0, jnp.exp(LOG_GAMMA[h] * jnp.maximum(diff, 0.0)), 0.0)

    idx = lax.broadcasted_iota(jnp.int32, (tl, 1), 0).astype(F32)

    def retention_head(h):
        lg = LOG_GAMMA[h]
        q = pm_ref[:, h * hd:(h + 1) * hd]
        k = pm_ref[:, _head_cols(RET_K, h)]
        v = pm_ref[:, _head_cols(RET_V, h)]
        gate_act = pm_ref[:, _head_cols(RET_GATE, h)]
        q_dec = jnp.exp(lg * (idx + 1.0))
        k_dec = jnp.exp(lg * (tl - 1.0 - idx))
        s = sret[h]
        qb = q.astype(BF16)
        vb = v.astype(BF16)
        scores = _dot_nt(qb, k.astype(BF16)) * intra_scr[h]
        o = _dot(scores.astype(BF16), vb) + _dot(qb, s.astype(BF16)) * q_dec
        sret[h] = s * math.exp(lg * tl) + _dot_tn((k * k_dec).astype(BF16), vb)
        mu = jnp.mean(o, axis=-1, keepdims=True)
        var = jnp.mean(jnp.square(o - mu), axis=-1, keepdims=True)
        ro = (o - mu) * lax.rsqrt(var + EPS) * gnw_ref[:, h * hd:(h + 1) * hd]
        mix_ref[:, h * hd:(h + 1) * hd] = (gate_act * ro).astype(BF16)

    tail = pt_ref[...]
    g_all = -jnp.exp(alog_ref[...]) * _softplus(tail + dtb_ref[...])
    beta_all = _sigmoid(tail)
    prefix, total = _chunk_sum_matrices(tl)
    g_parts = _pack3(g_all)
    gc_all = _unpack3(_dot(prefix.astype(BF16), g_parts))
    gl_all = _unpack3(_dot(total.astype(BF16), g_parts))
    yield

    n = N_HEADS * CHUNK
    si = lax.broadcasted_iota(jnp.int32, (n, n), 0)
    sj = lax.broadcasted_iota(jnp.int32, (n, n), 1)
    same_head = si // CHUNK == sj // CHUNK
    causal = same_head & (si >= sj)
    strict = same_head & (si > sj)

    def stack(fn):
        return jnp.concatenate([fn(h) for h in range(N_HEADS)], axis=0)

    chunks = []
    for c in range(tl // CHUNK):
        r = slice(c * CHUNK, (c + 1) * CHUNK)
        q_st = stack(lambda h: pm_ref[r, _head_cols(GDN_Q, h)])
        k_st = stack(lambda h: pm_ref[r, _head_cols(GDN_K, h)])
        v_st = stack(lambda h: pm_ref[r, _head_cols(GDN_V, h)])
        beta = stack(lambda h: beta_all[r, 4 + h:5 + h])
        gc = stack(lambda h: gc_all[r, h:h + 1])
        gl = stack(lambda h: gl_all[r, h:h + 1])
        gc_col = jnp.broadcast_to(gc, (n, n))
        seg = gc_col - gc_col.T
        decay = jnp.where(causal, jnp.exp(jnp.where(causal, seg, 0.0)), 0.0)
        k_b16 = k_st.astype(BF16)
        qk = _dot_nt(jnp.concatenate([q_st.astype(BF16), k_b16], axis=0), k_b16)
        egc = jnp.exp(gc)
        chunks.append(dict(
            r=r, gl=gl,
            low=jnp.where(strict, qk[n:] * beta * decay, 0.0),
            rhs=jnp.concatenate([v_st * beta, k_st * (beta * egc)], axis=1),
            scores=(qk[:n] * decay).astype(BF16),
            q_in=(q_st * egc).astype(BF16),
            k_out=(k_st * jnp.exp(gl - gc)).astype(BF16)))
        yield
    t_offs = yield from _unit_lower_inverses(
        [ch["low"] for ch in chunks],
        interleaved=[functools.partial(retention_head, h) for h in range(N_HEADS)])
    sols = [ch["rhs"] + _dot(t_off.astype(BF16), ch["rhs"].astype(BF16))
            for ch, t_off in zip(chunks, t_offs)]
    yield

    for ch, sol in zip(chunks, sols):
        r = ch["r"]
        w_b16 = sol[:, hd:].astype(BF16)
        v_new = []
        o_inter = []
        for h in range(N_HEADS):
            hr = slice(h * CHUNK, (h + 1) * CHUNK)
            lhs = jnp.concatenate([w_b16[hr], ch["q_in"][hr]], axis=0)
            prod = _dot(lhs, sgdn[h].astype(BF16))
            v_new.append(sol[hr, :hd] - prod[:CHUNK])
            o_inter.append(prod[CHUNK:])
        yield
        vn_b16 = jnp.concatenate(v_new, axis=0).astype(BF16)
        o_st = jnp.concatenate(o_inter, axis=0) + _dot(ch["scores"], vn_b16)
        yield
        for h in range(N_HEADS):
            hr = slice(h * CHUNK, (h + 1) * CHUNK)
            s_dec = jnp.exp(jnp.broadcast_to(ch["gl"][hr], (CHUNK, hd)))
            s_dec = jnp.concatenate([s_dec, s_dec], axis=0)
            sgdn[h] = sgdn[h] * s_dec + _dot_tn(ch["k_out"][hr], vn_b16[hr])
            o_h = o_st[hr]
            go = o_h * lax.rsqrt(jnp.mean(o_h * o_h, axis=-1, keepdims=True) + EPS) * gnn_ref[...]
            gz_act = pm_ref[r, _head_cols(GDN_GATE, h)]
            mix_ref[r, _head_cols(HEADS_W, h)] = (go * gz_act).astype(BF16)
        yield

    @pl.when(t == n_tiles - 1)
    def _():
        rs_ref[...] = sret[...]
        gs_ref[...] = sgdn[...]


def _hyb_core_prompt(pm, pt, alog, dtb, gnw, gnn, batch, seq):
    n_tiles = seq // SEQ_TILE
    ns = SEQS_PER_STEP
    tile = lambda b, t: (b, t, 0)
    state = pl.BlockSpec((ns, N_HEADS, HEAD_DIM, HEAD_DIM), lambda b, t: (b, 0, 0, 0))
    per_seq = lambda shape: pltpu.VMEM((ns,) + shape, F32)
    mixed, rs, gs = pl.pallas_call(
        functools.partial(_per_sequence(_hyb_core_kernel, ns, 2, 4), n_tiles=n_tiles),
        grid=(batch // ns, n_tiles),
        in_specs=[pl.BlockSpec((ns, SEQ_TILE, HYB_MAIN), tile),
                  pl.BlockSpec((ns, SEQ_TILE, LANES), tile),
                  _resident((1, LANES)), _resident((1, LANES)),
                  _resident((1, HEADS_W)), _resident((1, LANES))],
        out_specs=[pl.BlockSpec((ns, SEQ_TILE, D_MODEL), tile), state, state],
        out_shape=[jax.ShapeDtypeStruct((batch, seq, D_MODEL), BF16),
                   jax.ShapeDtypeStruct((batch, N_HEADS, HEAD_DIM, HEAD_DIM), F32),
                   jax.ShapeDtypeStruct((batch, N_HEADS, HEAD_DIM, HEAD_DIM), F32)],
        scratch_shapes=[per_seq((N_HEADS, HEAD_DIM, HEAD_DIM)),
                        per_seq((N_HEADS, HEAD_DIM, HEAD_DIM)),
                        per_seq((N_HEADS, SEQ_TILE, SEQ_TILE))],
        compiler_params=_params(2),
        name="hyb_core_prompt",
    )(pm.reshape(batch, seq, HYB_MAIN), pt.reshape(batch, seq, LANES), alog, dtb, gnw, gnn)
    return mixed.reshape(batch * seq, D_MODEL), rs, gs


def _group_rmsnorm_gate(y, z, nw_ref, o_ref, rows):
    for g in range(SSM_GROUPS):
        sl = slice(g * SSM_GROUP_W, (g + 1) * SSM_GROUP_W)
        yg = y[:, sl] * _silu(z[:, sl])
        yg = yg * lax.rsqrt(jnp.mean(yg * yg, axis=-1, keepdims=True) + EPS)
        o_ref[rows, sl] = (yg * nw_ref[:, sl]).astype(o_ref.dtype)


def _ssm_core_kernel(pm_ref, pt_ref, alog_ref, dtb_ref, dskip_ref, nw_ref,
                     y_ref, st_ref, s_t, xdt, xdec, eacs, acs_e, y_scr, *, n_tiles):
    t = pl.program_id(1)
    tl = SEQ_TILE
    xbc = pm_ref.at[:, SSM_DINNER:SSM_MAIN]

    @pl.when(t == 0)
    def _():
        s_t[...] = jnp.zeros_like(s_t)

    n_chunks = tl // CHUNK
    expand = _head_expand_matrix()
    dt = _softplus(pt_ref[...] + dtb_ref[...])
    a = dt * (-jnp.exp(alog_ref[...]))
    prefix, total = _chunk_sum_matrices(tl)
    a_parts = _pack3(a)
    acs = _unpack3(_dot(prefix.astype(BF16), a_parts))
    atot = _unpack3(_dot(total.astype(BF16), a_parts))
    dt_parts = _pack3(dt)
    dec_parts = _pack3(jnp.exp(atot - acs))
    eacs_parts = _pack3(jnp.exp(acs))
    acs_parts = _pack3(acs)
    chunk_rows = [atot[c * CHUNK:c * CHUNK + 1] for c in range(n_chunks)]
    chunk_rows.append(jnp.zeros((SUBLANES - n_chunks, LANES), F32))
    cdec_parts = _pack3(jnp.exp(jnp.concatenate(chunk_rows, axis=0)))

    def expand_group(parts, g):
        return _dot(parts, expand[:, g * SSM_GROUP_W:(g + 1) * SSM_GROUP_W])

    yield
    chunk_dec = []
    for g in range(SSM_GROUPS):
        gs = slice(g * SSM_GROUP_W, (g + 1) * SSM_GROUP_W)
        x_dt = xbc[:, gs] * expand_group(dt_parts, g)
        xdt[:, gs] = x_dt
        yield
        xdec[:, gs] = (x_dt * expand_group(dec_parts, g)).astype(BF16)
        yield
        eacs[:, gs] = expand_group(eacs_parts, g)
        yield
        acs_e[:, gs] = expand_group(acs_parts, g)
        chunk_dec.append(expand_group(cdec_parts, g))
        yield

    lane = lax.broadcasted_iota(jnp.int32, (CHUNK, LANES), 1)
    row_i = lax.broadcasted_iota(jnp.int32, (CHUNK, LANES), 0)
    first_half = lane < SSM_HEADDIM
    causal2 = row_i >= (lane & (CHUNK - 1))
    lane2 = lax.broadcasted_iota(jnp.int32, (2 * CHUNK, LANES), 1)
    row2 = lax.broadcasted_iota(jnp.int32, (2 * CHUNK, LANES), 0)
    pair_diag = (row2 < CHUNK) == (lane2 < SSM_HEADDIM)

    def bc_cc(r, g):
        bc = xbc[r, SSM_DINNER + g * LANES:SSM_DINNER + (g + 1) * LANES]
        cc = xbc[r, SSM_DINNER + 512 + g * LANES:SSM_DINNER + 512 + (g + 1) * LANES]
        return bc.astype(BF16), cc.astype(BF16)

    for c in range(n_chunks):
        r = slice(c * CHUNK, (c + 1) * CHUNK)
        acs_c = acs[r]
        acs_t = jnp.concatenate([acs_c, acs_c], axis=0).T
        cb2 = []
        for g in range(SSM_GROUPS):
            bc_b16, cc_b16 = bc_cc(r, g)
            cb2.append(_dot_nt(cc_b16, jnp.concatenate([bc_b16, bc_b16], axis=0)))
        yield
        for p in range(SSM_HEADS // 2):
            ha = 2 * p
            rowv = jnp.where(first_half, acs_t[ha:ha + 1, :], acs_t[ha + 1:ha + 2, :])
            seg = acs_e[r, p * LANES:(p + 1) * LANES] - rowv
            lmat = jnp.where(causal2, jnp.exp(jnp.where(causal2, seg, 0.0)), 0.0)
            m = (cb2[p // 4] * lmat).astype(BF16)
            xblk = xdt[r, p * LANES:(p + 1) * LANES]
            xblk2 = jnp.concatenate([xblk, xblk], axis=0)
            x_bd = jnp.where(pair_diag, xblk2, 0.0).astype(BF16)
            y_scr[r, p * LANES:(p + 1) * LANES] = _dot(m, x_bd)
            yield

    for c in range(n_chunks):
        r = slice(c * CHUNK, (c + 1) * CHUNK)
        groups = range(SSM_GROUPS)
        gsl = [slice(g * SSM_GROUP_W, (g + 1) * SSM_GROUP_W) for g in groups]
        bcs, ccs = zip(*[bc_cc(r, g) for g in groups])
        s_old = [s_t[:, gsl[g]] for g in groups]
        y_inter = [_dot(ccs[g], s_old[g].astype(BF16)) for g in groups]
        yield
        s_add = [_dot_tn(bcs[g], xdec[r, gsl[g]]) for g in groups]
        yield
        for g in groups:
            s_t[:, gsl[g]] = s_old[g] * chunk_dec[g][c:c + 1, :] + s_add[g]
            y_scr[r, gsl[g]] = y_scr[r, gsl[g]] + y_inter[g] * eacs[r, gsl[g]]
        yield

    for g in range(SSM_GROUPS):
        gs = slice(g * SSM_GROUP_W, (g + 1) * SSM_GROUP_W)
        yg = (y_scr[:, gs] + xbc[:, gs] * dskip_ref[:, gs]) * pm_ref[:, gs]
        yg = yg * lax.rsqrt(jnp.mean(yg * yg, axis=-1, keepdims=True) + EPS)
        y_ref[:, gs] = (yg * nw_ref[:, gs]).astype(y_ref.dtype)
        yield

    @pl.when(t == n_tiles - 1)
    def _():
        for kblk in range(SSM_DINNER // LANES):
            ks = slice(kblk * LANES, (kblk + 1) * LANES)
            st_ref[ks, :] = s_t[:, ks].T


def _per_sequence(body, n_seq, n_tiled_in, n_shared_in):
    def kernel(*refs, **kw):
        tiled = refs[:n_tiled_in]
        shared = refs[n_tiled_in:n_tiled_in + n_shared_in]
        rest = refs[n_tiled_in + n_shared_in:]
        running = [body(*[r.at[s] for r in tiled], *shared, *[r.at[s] for r in rest], **kw)
                   for s in range(n_seq)]
        while running:
            for gen in list(running):
                if next(gen, StopIteration) is StopIteration:
                    running.remove(gen)
    return kernel


def _ssm_core_prompt(pm, pt, alog, dtb, dskip, nw, batch, seq):
    n_tiles = seq // SEQ_TILE
    ns = SEQS_PER_STEP
    tile = lambda b, t: (b, t, 0)
    per_seq = lambda shape, dtype: pltpu.VMEM((ns,) + shape, dtype)
    y, st = pl.pallas_call(
        functools.partial(_per_sequence(_ssm_core_kernel, ns, 2, 4), n_tiles=n_tiles),
        grid=(batch // ns, n_tiles),
        in_specs=[pl.BlockSpec((ns, SEQ_TILE, SSM_MAIN), tile),
                  pl.BlockSpec((ns, SEQ_TILE, LANES), tile),
                  _resident((1, LANES)), _resident((1, LANES)),
                  _resident((1, SSM_DINNER)), _resident((1, SSM_DINNER))],
        out_specs=[pl.BlockSpec((ns, SEQ_TILE, SSM_DINNER), tile),
                   pl.BlockSpec((ns, SSM_DINNER, SSM_DSTATE), lambda b, t: (b, 0, 0))],
        out_shape=[jax.ShapeDtypeStruct((batch, seq, SSM_DINNER), BF16),
                   jax.ShapeDtypeStruct((batch, SSM_DINNER, SSM_DSTATE), F32)],
        scratch_shapes=[per_seq((SSM_DSTATE, SSM_DINNER), F32),
                        per_seq((SEQ_TILE, SSM_DINNER), F32),
                        per_seq((SEQ_TILE, SSM_DINNER), BF16),
                        per_seq((SEQ_TILE, SSM_DINNER), F32),
                        per_seq((SEQ_TILE, SSM_DINNER), F32),
                        per_seq((SEQ_TILE, SSM_DINNER), F32)],
        compiler_params=_params(2),
        name="ssm_core_prompt",
    )(pm.reshape(batch, seq, SSM_MAIN), pt.reshape(batch, seq, LANES), alog, dtb, dskip, nw)
    return y.reshape(batch * seq, SSM_DINNER), st


def _hyb_step_front_kernel(pm_ref, conv_ref, cos_ref, sin_ref, cw_ref, o_ref, conv_out):
    hd = HEAD_DIM
    ch = GDN_CONV_CH
    cos = cos_ref[...]
    sin = sin_ref[...]
    for h in range(N_HEADS):
        qs, ks = _head_cols(RET_Q, h), _head_cols(RET_K, h)
        o_ref[:, qs] = _rotary(pm_ref[:, qs], cos, sin)
        o_ref[:, ks] = _rotary(pm_ref[:, ks], cos, sin) * (hd ** -0.5)
    o_ref[:, RET_V:RET_GATE] = pm_ref[:, RET_V:RET_GATE]
    o_ref[:, RET_GATE:GDN_Q] = _silu(pm_ref[:, RET_GATE:GDN_Q])
    o_ref[:, GDN_GATE:HYB_MAIN] = _silu(pm_ref[:, GDN_GATE:HYB_MAIN])
    for blk in range(ch // LANES):
        cs = slice(blk * LANES, (blk + 1) * LANES)
        u = pm_ref[:, GDN_Q + blk * LANES:GDN_Q + (blk + 1) * LANES]
        c0 = conv_ref[:, blk * LANES:(blk + 1) * LANES]
        c1 = conv_ref[:, ch + blk * LANES:ch + (blk + 1) * LANES]
        c2 = conv_ref[:, 2 * ch + blk * LANES:2 * ch + (blk + 1) * LANES]
        conv = c0 * cw_ref[0:1, cs]
        conv = conv + c1 * cw_ref[1:2, cs]
        conv = conv + c2 * cw_ref[2:3, cs]
        conv = conv + u * cw_ref[3:4, cs]
        act = _silu(conv)
        if blk < N_HEADS:
            act = _l2norm(act) * (hd ** -0.5)
        elif blk < 2 * N_HEADS:
            act = _l2norm(act)
        o_ref[:, GDN_Q + blk * LANES:GDN_Q + (blk + 1) * LANES] = act
        conv_out[:, blk * LANES:(blk + 1) * LANES] = c1
        conv_out[:, ch + blk * LANES:ch + (blk + 1) * LANES] = c2
        conv_out[:, 2 * ch + blk * LANES:2 * ch + (blk + 1) * LANES] = u


def _hyb_step_front(pm, conv_state, cos, sin, cw_t):
    t = pm.shape[0]
    full = lambda a: pl.BlockSpec(a.shape, lambda i: (0,) * a.ndim)
    return pl.pallas_call(
        _hyb_step_front_kernel,
        grid=(1,),
        in_specs=[full(pm), full(conv_state), full(cos), full(sin), full(cw_t)],
        out_specs=[full(pm), full(conv_state)],
        out_shape=[jax.ShapeDtypeStruct(pm.shape, F32), jax.ShapeDtypeStruct(conv_state.shape, F32)],
        compiler_params=_params(1),
        name="hyb_step_front",
    )(pm, conv_state, cos, sin, cw_t)


def _hyb_step_kernel(pm_ref, pt_ref, rs_in, gs_in, alog_ref, dtb_ref, gnw_ref, gnn_ref,
                     mix_ref, rs_out, gs_out, o_scr):
    bb = STEP_BATCH
    hd = HEAD_DIM
    tail = pt_ref[...]
    g_all = -jnp.exp(alog_ref[...]) * _softplus(tail + dtb_ref[...])
    beta_all = _sigmoid(tail)

    for h in range(N_HEADS):
        q = pm_ref[:, h * hd:(h + 1) * hd]
        k = pm_ref[:, _head_cols(RET_K, h)]
        v = pm_ref[:, _head_cols(RET_V, h)]
        k_cols = _columns(k)
        gamma = math.exp(LOG_GAMMA[h])
        for b in range(bb):
            s = rs_in[b, h] * gamma + k_cols[:, b:b + 1] * v[b:b + 1, :]
            rs_out[b, h] = s
            o_scr[b:b + 1, h * hd:(h + 1) * hd] = _dot(q, s)[b:b + 1, :]
        gq = pm_ref[:, _head_cols(GDN_Q, h)]
        gk = pm_ref[:, _head_cols(GDN_K, h)]
        gv = pm_ref[:, _head_cols(GDN_V, h)]
        k_cols = _columns(gk)
        decay = jnp.broadcast_to(jnp.exp(g_all[:, h:h + 1]), (bb, hd))
        beta = jnp.broadcast_to(beta_all[:, 4 + h:5 + h], (bb, hd))
        for b in range(bb):
            s = gs_in[b, h] * decay[b:b + 1, :]
            delta = (gv[b:b + 1, :] - _dot(gk, s)[b:b + 1, :]) * beta[b:b + 1, :]
            s = s + k_cols[:, b:b + 1] * delta
            gs_out[b, h] = s
            o_scr[b:b + 1, _head_cols(HEADS_W, h)] = _dot(gq, s)[b:b + 1, :]

    for h in range(N_HEADS):
        o = o_scr[:, h * hd:(h + 1) * hd]
        mu = jnp.mean(o, axis=-1, keepdims=True)
        var = jnp.mean(jnp.square(o - mu), axis=-1, keepdims=True)
        ro = (o - mu) * lax.rsqrt(var + EPS) * gnw_ref[:, h * hd:(h + 1) * hd]
        mix_ref[:, h * hd:(h + 1) * hd] = pm_ref[:, _head_cols(RET_GATE, h)] * ro
        go = o_scr[:, _head_cols(HEADS_W, h)]
        go = go * lax.rsqrt(jnp.mean(go * go, axis=-1, keepdims=True) + EPS) * gnn_ref[...]
        mix_ref[:, _head_cols(HEADS_W, h)] = go * pm_ref[:, _head_cols(GDN_GATE, h)]


def _layer_chained_outputs(state_arrays, layer, partial):
    shapes = [jax.ShapeDtypeStruct(s.shape, s.dtype) for s in state_arrays]
    if layer == 0:
        return [], [], shapes
    return list(partial), [pl.BlockSpec(memory_space=pl.ANY)] * len(partial), shapes


def _hyb_core_step(pm_act, pt, ret_all, gdn_all, layer, partial, alog, dtb, gnw, gnn):
    batch = pm_act.shape[0]
    bb = STEP_BATCH
    row = lambda i: (i, 0)
    state = pl.BlockSpec((None, bb, N_HEADS, HEAD_DIM, HEAD_DIM), lambda i: (layer, i, 0, 0, 0))
    extra, extra_specs, state_shapes = _layer_chained_outputs((ret_all, gdn_all), layer, partial)
    n_in = 8
    aliases = {n_in + k: 1 + k for k in range(len(extra))}

    def body(*refs):
        _hyb_step_kernel(*refs[:n_in], *refs[n_in + len(extra):])

    return pl.pallas_call(
        body,
        grid=(batch // bb,),
        in_specs=[pl.BlockSpec((bb, HYB_MAIN), row), pl.BlockSpec((bb, LANES), row), state, state,
                  _resident((1, LANES)), _resident((1, LANES)),
                  _resident((1, HEADS_W)), _resident((1, LANES))] + extra_specs,
        out_specs=[pl.BlockSpec((bb, D_MODEL), row), state, state],
        out_shape=[jax.ShapeDtypeStruct((batch, D_MODEL), F32)] + state_shapes,
        scratch_shapes=[pltpu.VMEM((bb, D_MODEL), F32)],
        input_output_aliases=aliases,
        compiler_params=_params(1),
        name="hyb_core_step",
    )(pm_act, pt, ret_all, gdn_all, alog, dtb, gnw, gnn, *extra)


def _ssm_step_kernel(pm_ref, pt_ref, conv_ref, s_in, cw_ref, cbias_ref, alog_ref, dtb_ref,
                     dskip_ref, nw_ref, y_ref, conv_out, s_out, y_scr):
    bb = STEP_BATCH
    ch = SSM_CONV_CH
    u = pm_ref[:, SSM_DINNER:SSM_MAIN]
    c0 = conv_ref[:, 0:ch]
    c1 = conv_ref[:, ch:2 * ch]
    c2 = conv_ref[:, 2 * ch:3 * ch]
    conv = c0 * cw_ref[0:1, :]
    conv = conv + c1 * cw_ref[1:2, :]
    conv = conv + c2 * cw_ref[2:3, :]
    conv = conv + u * cw_ref[3:4, :]
    xbc = _silu(conv + cbias_ref[...])
    conv_out[:, 0:ch] = c1
    conv_out[:, ch:2 * ch] = c2
    conv_out[:, 2 * ch:3 * ch] = u
    xs = xbc[:, 0:SSM_DINNER]
    expand = _head_expand_matrix()
    dt = _softplus(pt_ref[...] + dtb_ref[...])
    a = dt * (-jnp.exp(alog_ref[...]))
    x_dt = xs * _expand_heads(dt, expand)
    decay = jnp.exp(a)
    half = SSM_HEADDIM
    for kblk in range(SSM_DINNER // LANES):
        ks = slice(kblk * LANES, (kblk + 1) * LANES)
        g = kblk // 4
        x_cols = _columns(x_dt[:, ks])
        bm = xbc[:, SSM_DINNER + g * LANES:SSM_DINNER + (g + 1) * LANES]
        cm = xbc[:, SSM_DINNER + 512 + g * LANES:SSM_DINNER + 512 + (g + 1) * LANES]
        for b in range(bb):
            d0 = decay[b:b + 1, 2 * kblk:2 * kblk + 1]
            d1 = decay[b:b + 1, 2 * kblk + 1:2 * kblk + 2]
            s_dec = jnp.concatenate([s_in[b, kblk * LANES:kblk * LANES + half, :] * d0,
                                     s_in[b, kblk * LANES + half:(kblk + 1) * LANES, :] * d1], axis=0)
            s = s_dec + x_cols[:, b:b + 1] * bm[b:b + 1, :]
            s_out[b, ks, :] = s
            y_scr[b:b + 1, ks] = _dot_nt(cm, s)[b:b + 1, :]
    y = y_scr[...] + xs * dskip_ref[...]
    _group_rmsnorm_gate(y, pm_ref[:, 0:SSM_DINNER], nw_ref, y_ref, slice(None))


def _ssm_core_step(pm, pt, conv_state, ssm_all, layer, partial, cw_t, cbias, alog, dtb, dskip, nw):
    batch = pm.shape[0]
    bb = STEP_BATCH
    row = lambda i: (i, 0)
    state = pl.BlockSpec((None, bb, SSM_DINNER, SSM_DSTATE), lambda i: (layer, i, 0, 0))
    extra, extra_specs, state_shapes = _layer_chained_outputs((ssm_all,), layer, partial)
    n_in = 10
    aliases = {n_in + k: 2 + k for k in range(len(extra))}

    def body(*refs):
        _ssm_step_kernel(*refs[:n_in], *refs[n_in + len(extra):])

    return pl.pallas_call(
        body,
        grid=(batch // bb,),
        in_specs=[pl.BlockSpec((bb, SSM_MAIN), row), pl.BlockSpec((bb, LANES), row),
                  pl.BlockSpec((bb, 3 * SSM_CONV_CH), row), state,
                  _resident((CONV_W, SSM_CONV_CH)), _resident((1, SSM_CONV_CH)),
                  _resident((1, LANES)), _resident((1, LANES)),
                  _resident((1, SSM_DINNER)), _resident((1, SSM_DINNER))] + extra_specs,
        out_specs=[pl.BlockSpec((bb, SSM_DINNER), row), pl.BlockSpec((bb, 3 * SSM_CONV_CH), row),
                   state],
        out_shape=[jax.ShapeDtypeStruct((batch, SSM_DINNER), F32),
                   jax.ShapeDtypeStruct((batch, 3 * SSM_CONV_CH), F32)] + state_shapes,
        scratch_shapes=[pltpu.VMEM((bb, SSM_DINNER), F32)],
        input_output_aliases=aliases,
        compiler_params=_params(1),
        name="ssm_core_step",
    )(pm, pt, conv_state, ssm_all, cw_t, cbias, alog, dtb, dskip, nw, *extra)


def _rope_tables(pos):
    inv_freq = 1.0 / (ROPE_BASE ** jnp.linspace(0.0, 1.0, HEAD_DIM // 2, dtype=F32))
    ang = pos.astype(F32)[:, None] * inv_freq[None, :]
    cos = jnp.repeat(jnp.cos(ang), 2, axis=-1)
    sign = jnp.tile(jnp.array([-1.0, 1.0], F32), HEAD_DIM // 2)
    sin = jnp.repeat(jnp.sin(ang), 2, axis=-1) * sign
    return cos, sin


def _lane_row(v):
    return jnp.pad(v.astype(F32), (0, LANES - v.shape[0])).reshape(1, LANES)


def kernel(x_prompt, x_sample, state_ret, state_gdn, state_gdn_conv, state_ssm, state_ssm_conv,
           norm_mix_w, norm_ffn_w, norm_final_w, hyb_w_in, hyb_w_out, ret_gn_w, gdn_conv_w,
           gdn_a_log, gdn_dt_bias, gdn_norm_w, ssm_w_in, ssm_conv_w, ssm_conv_b, ssm_dt_bias,
           ssm_a_log, ssm_d, ssm_norm_w, ssm_w_out, ffn_w_gate, ffn_w_up, ffn_w_down):
    depth = norm_mix_w.shape[0]
    bp, seq, d = x_prompt.shape
    bs = x_sample.shape[0]
    past_len = 16384

    hyb_in = hyb_w_in.astype(BF16)
    hyb_tail = jnp.pad(hyb_w_in[:, :, HYB_MAIN:], ((0, 0), (0, 0), (0, LANES - 2 * N_HEADS))).astype(BF16)
    ssm_in = ssm_w_in.astype(BF16)
    ssm_tail = jnp.pad(ssm_w_in[:, :, SSM_MAIN:], ((0, 0), (0, 0), (0, LANES - SSM_HEADS))).astype(BF16)
    hyb_out = hyb_w_out.astype(BF16)
    ssm_out = ssm_w_out.astype(BF16)
    w_gate = ffn_w_gate.astype(BF16)
    w_up = ffn_w_up.astype(BF16)
    w_down = ffn_w_down.astype(BF16)

    gdn_cw_t = jnp.swapaxes(gdn_conv_w, 1, 2)
    ssm_cw_t = jnp.swapaxes(ssm_conv_w, 1, 2)
    dskip = jnp.repeat(ssm_d, SSM_HEADDIM, axis=-1)
    cos_p, sin_p = _rope_tables(jnp.arange(seq))
    cos_s, sin_s = _rope_tables(jnp.arange(past_len, past_len + 1))

    ssm_state_rows = state_ssm.reshape(state_ssm.shape[0], bs, SSM_DINNER, SSM_DSTATE)

    def trunk(x, prompt):
        t = x.shape[0]
        ret_l, gdn_l, gconv_l, ssm_l, sconv_l = [], [], [], [], []
        hyb_states, ssm_states = (), ()
        for i in range(depth):
            j = i // 2
            if i % 2 == 0:
                args = (_lane_row(gdn_a_log[j]), _lane_row(gdn_dt_bias[j]),
                        ret_gn_w[j].reshape(1, HEADS_W), gdn_norm_w[j].reshape(1, LANES))
                if prompt:
                    rope = pl.BlockSpec((_row_tile(t), LANES), lambda r: (r % (seq // _row_tile(t)), 0))
                    pm, pt, cb = _prompt_proj(
                        _hyb_proj_kernel, x, norm_mix_w[i], hyb_in, j, HYB_MAIN, hyb_tail[j],
                        GDN_CONV_CH, seq, (cos_p, sin_p, gdn_cw_t[j]),
                        [rope, rope, _resident((CONV_W, GDN_CONV_CH))])
                    mixed, rs, gs = _hyb_core_prompt(pm, pt, *args, bp, seq)
                    cb = cb[:, SUBLANES - (CONV_W - 1):, :]
                    ret_l.append(rs)
                    gdn_l.append(gs)
                else:
                    pm, pt = _norm_proj(x, norm_mix_w[i], hyb_in, j, HYB_MAIN, hyb_tail[j])
                    pm_act, cb = _hyb_step_front(pm, state_gdn_conv[j].reshape(t, -1), cos_s, sin_s,
                                                 gdn_cw_t[j])
                    mixed, *hyb_states = _hyb_core_step(pm_act, pt, state_ret, state_gdn, j,
                                                        hyb_states, *args)
                    cb = cb.reshape(t, CONV_W - 1, GDN_CONV_CH)
                gconv_l.append(cb)
                w_mix = hyb_out
            else:
                args = (_lane_row(ssm_a_log[j]), _lane_row(ssm_dt_bias[j]), dskip[j].reshape(1, -1),
                        ssm_norm_w[j].reshape(1, -1))
                conv_args = (ssm_cw_t[j], ssm_conv_b[j].reshape(1, -1))
                if prompt:
                    pm, pt, cb = _prompt_proj(
                        _ssm_proj_kernel, x, norm_mix_w[i], ssm_in, j, SSM_MAIN, ssm_tail[j],
                        SSM_CONV_CH, seq, conv_args,
                        [_resident((CONV_W, SSM_CONV_CH)), _resident((1, SSM_CONV_CH))])
                    y, ss = _ssm_core_prompt(pm, pt, *args, bp, seq)
                    cb = cb[:, SUBLANES - (CONV_W - 1):, :]
                    ssm_l.append(ss.reshape(-1, SSM_HEADS, SSM_HEADDIM, SSM_DSTATE))
                else:
                    pm, pt = _norm_proj(x, norm_mix_w[i], ssm_in, j, SSM_MAIN, ssm_tail[j])
                    y, cb, *ssm_states = _ssm_core_step(
                        pm, pt, state_ssm_conv[j].reshape(t, -1), ssm_state_rows, j, ssm_states,
                        *conv_args, *args)
                    cb = cb.reshape(t, CONV_W - 1, SSM_CONV_CH)
                sconv_l.append(cb)
                mixed, w_mix = y, ssm_out
            x = _mix_ffn(x, mixed, w_mix, j, norm_ffn_w[i], w_gate, w_up, w_down, i, norm_final_w,
                         final_norm=(i == depth - 1))
        if prompt:
            ret_new, gdn_new, ssm_new = jnp.stack(ret_l), jnp.stack(gdn_l), jnp.stack(ssm_l)
        else:
            ret_new, gdn_new = hyb_states
            ssm_new = ssm_states[0].reshape(state_ssm.shape)
        return x, ret_new, gdn_new, jnp.stack(gconv_l), ssm_new, jnp.stack(sconv_l)

    yp, ret_p, gdn_p, gconv_p, ssm_p, sconv_p = trunk(x_prompt.reshape(bp * seq, d), True)
    ys, ret_s, gdn_s, gconv_s, ssm_s, sconv_s = trunk(x_sample.reshape(bs, d), False)
    return (yp.reshape(bp, seq, d), ys.reshape(bs, 1, d),
            ret_p, gdn_p, gconv_p, ssm_p, sconv_p,
            ret_s, gdn_s, gconv_s, ssm_s, sconv_s)
```

```python
import functools
import math

import jax
import jax.numpy as jnp
from jax import lax
from jax.experimental import pallas as pl
from jax.experimental.pallas import tpu as pltpu

F32 = jnp.float32
BF16 = jnp.bfloat16

D_MODEL = 1024
N_HEADS = 4
HEAD_DIM = 128
HEADS_W = N_HEADS * HEAD_DIM
HYB_MAIN = 8 * HEADS_W
RET_Q, RET_K, RET_V, RET_GATE, GDN_Q, GDN_K, GDN_V, GDN_GATE = (i * HEADS_W for i in range(8))
GDN_CONV_CH = 1536
SSM_DINNER = 2048
SSM_HEADS = 32
SSM_HEADDIM = 64
SSM_GROUPS = 4
SSM_DSTATE = 128
SSM_GROUP_W = SSM_DINNER // SSM_GROUPS
SSM_CONV_CH = 3072
SSM_MAIN = 5120
D_FF = 2816
FF_CHUNK = 256
CONV_W = 4
EPS = 1e-6
ROPE_BASE = 10000.0

LANES = 128
SUBLANES = 8
VMEM_LIMIT = 56 * 1024 * 1024

PROJ_PIECE_LANES = 256
FFN_ROW_TILE = 512
SEQ_TILE = 256
SEQS_PER_STEP = 2
CHUNK = 64
STEP_BATCH = 8
LOG_GAMMA = tuple(math.log1p(-2.0 ** (-5.0 - h)) for h in range(N_HEADS))

_NT = (((1,), (1,)), ((), ()))
_TN = (((0,), (0,)), ((), ()))


def _dot(a, b):
    return jnp.dot(a, b, preferred_element_type=F32)


def _dot_nt(a, b):
    return lax.dot_general(a, b, _NT, preferred_element_type=F32)


def _dot_tn(a, b):
    return lax.dot_general(a, b, _TN, preferred_element_type=F32)


def _dot_exact(a, b):
    return jnp.dot(a, b, preferred_element_type=F32, precision=lax.Precision.HIGHEST)


def _silu(x):
    return x / (1.0 + jnp.exp(-x))


def _sigmoid(x):
    return 1.0 / (1.0 + jnp.exp(-x))


def _softplus(x):
    return jnp.maximum(x, 0.0) + jnp.log1p(jnp.exp(-jnp.abs(x)))


def _rmsnorm(x, w):
    return x * lax.rsqrt(jnp.mean(x * x, axis=-1, keepdims=True) + EPS) * w


def _resident(shape):
    nd = len(shape)
    return pl.BlockSpec(shape, lambda *_: (0,) * nd, pipeline_mode=pl.Buffered(1))


def _resident_layer(shape, layer):
    nd = len(shape)
    return pl.BlockSpec((None,) + tuple(shape), lambda *_: (layer,) + (0,) * nd,
                        pipeline_mode=pl.Buffered(1))


def _params(n_axes):
    return pltpu.CompilerParams(dimension_semantics=("arbitrary",) * n_axes,
                                vmem_limit_bytes=VMEM_LIMIT)


def _head_cols(base, h):
    return slice(base + h * HEAD_DIM, base + (h + 1) * HEAD_DIM)


def _row_tile(t):
    return 512 if t % 512 == 0 else t


def _norm_proj_kernel(x_ref, nw_ref, w_ref, wt_ref, o_ref, ot_ref, *, n_chunk):
    h = _rmsnorm(x_ref[...], nw_ref[...]).astype(BF16)
    n = w_ref.shape[1]
    for c in range(n // n_chunk):
        sl = slice(c * n_chunk, (c + 1) * n_chunk)
        o_ref[:, sl] = _dot(h, w_ref[:, sl])
    ot_ref[...] = _dot(h, wt_ref[...])


def _norm_proj(x, nw, w_all, layer, n, w_tail):
    t, d = x.shape
    tm = _row_tile(t)
    return pl.pallas_call(
        functools.partial(_norm_proj_kernel, n_chunk=512),
        grid=(t // tm,),
        in_specs=[pl.BlockSpec((tm, d), lambda i: (i, 0)),
                  _resident((1, d)), _resident_layer((d, n), layer), _resident((d, LANES))],
        out_specs=[pl.BlockSpec((tm, n), lambda i: (i, 0)),
                   pl.BlockSpec((tm, LANES), lambda i: (i, 0))],
        out_shape=[jax.ShapeDtypeStruct((t, n), F32), jax.ShapeDtypeStruct((t, LANES), F32)],
        compiler_params=_params(1),
        name="norm_proj",
    )(x, nw.reshape(1, d), w_all, w_tail)


def _causal_conv_block(u, xx, cw_ref, cs):
    n = u.shape[0]
    xx[SUBLANES:SUBLANES + n, cs] = u
    ext = xx[:, cs]
    prev = pltpu.roll(ext, 1, 0)
    newer = ext * cw_ref[3:4, cs] + prev * cw_ref[2:3, cs]
    older = ext * cw_ref[1:2, cs] + prev * cw_ref[0:1, cs]
    return (newer + pltpu.roll(older, 2, 0))[SUBLANES:, :]


def _prompt_proj_body(x_ref, nw_ref, w_ref, wt_ref, o_ref, ot_ref, cb_ref, xx, acc_scr, tiles_per_seq,
                      epilogue):
    t = lax.rem(pl.program_id(0), tiles_per_seq)
    tm = x_ref.shape[0]
    piece = PROJ_PIECE_LANES
    per_piece = piece // LANES

    @pl.when(t == 0)
    def _():
        xx[0:SUBLANES, :] = jnp.zeros((SUBLANES, xx.shape[1]), F32)

    h = _rmsnorm(x_ref[...], nw_ref[...]).astype(BF16)
    n_pieces = w_ref.shape[1] // piece
    for p in range(n_pieces):
        acc = _dot(h, w_ref[:, p * piece:(p + 1) * piece])
        for j in range(per_piece):
            blk = p * per_piece + j
            o_ref[:, blk * LANES:(blk + 1) * LANES] = epilogue(
                blk // 4, blk % 4, acc[:, j * LANES:(j + 1) * LANES])
    ot_ref[...] = _dot(h, wt_ref[...])
    xx[0:SUBLANES, :] = xx[tm:tm + SUBLANES, :]

    @pl.when(t == tiles_per_seq - 1)
    def _():
        cb_ref[0] = xx[tm:tm + SUBLANES, :]


def _hyb_proj_kernel(x_ref, nw_ref, w_ref, wt_ref, cos_ref, sin_ref, cw_ref, o_ref, ot_ref, cb_ref,
                     xx, acc_scr, *, tiles_per_seq):
    cos = cos_ref[...]
    sin = sin_ref[...]
    scale = HEAD_DIM ** -0.5

    def epilogue(c, j, blk):
        if c == 0:
            return _rotary(blk, cos, sin)
        if c == 1:
            return _rotary(blk, cos, sin) * scale
        if c == 2:
            return blk
        if c in (3, 7):
            return _silu(blk)
        ch = (c - 4) * 4 + j
        act = _silu(_causal_conv_block(blk, xx, cw_ref, slice(ch * LANES, (ch + 1) * LANES)))
        if c == 4:
            return _l2norm(act) * scale
        if c == 5:
            return _l2norm(act)
        return act

    _prompt_proj_body(x_ref, nw_ref, w_ref, wt_ref, o_ref, ot_ref, cb_ref, xx, acc_scr, tiles_per_seq,
                      epilogue)


def _ssm_proj_kernel(x_ref, nw_ref, w_ref, wt_ref, cw_ref, cbias_ref, o_ref, ot_ref, cb_ref, xx,
                     acc_scr, *, tiles_per_seq):
    def epilogue(c, j, blk):
        if c < 4:
            return _silu(blk)
        ch = (c - 4) * 4 + j
        cs = slice(ch * LANES, (ch + 1) * LANES)
        return _silu(_causal_conv_block(blk, xx, cw_ref, cs) + cbias_ref[:, cs])

    _prompt_proj_body(x_ref, nw_ref, w_ref, wt_ref, o_ref, ot_ref, cb_ref, xx, acc_scr, tiles_per_seq,
                      epilogue)


def _prompt_proj(kernel_fn, x, nw, w_all, layer, n, w_tail, conv_ch, seq, extra, extra_specs):
    t, d = x.shape
    tm = _row_tile(t)
    tiles_per_seq = seq // tm
    return pl.pallas_call(
        functools.partial(kernel_fn, tiles_per_seq=tiles_per_seq),
        grid=(t // tm,),
        in_specs=[pl.BlockSpec((tm, d), lambda i: (i, 0)),
                  _resident((1, d)), _resident_layer((d, n), layer), _resident((d, LANES))] + extra_specs,
        out_specs=[pl.BlockSpec((tm, n), lambda i: (i, 0)),
                   pl.BlockSpec((tm, LANES), lambda i: (i, 0)),
                   pl.BlockSpec((1, SUBLANES, conv_ch), lambda i: (i // tiles_per_seq, 0, 0))],
        out_shape=[jax.ShapeDtypeStruct((t, n), F32), jax.ShapeDtypeStruct((t, LANES), F32),
                   jax.ShapeDtypeStruct((t // seq, SUBLANES, conv_ch), F32)],
        scratch_shapes=[pltpu.VMEM((tm + SUBLANES, conv_ch), F32),
                        pltpu.VMEM((2, tm, PROJ_PIECE_LANES), F32)],
        compiler_params=_params(1),
        name=kernel_fn.__name__.strip("_"),
    )(x, nw.reshape(1, d), w_all, w_tail, *extra)


def _ffn_kernel(x_ref, a_ref, wo_ref, nw_ref, wg_ref, wu_ref, wd_ref, fw_ref, o_ref, act_ref, *,
                final_norm):
    x = x_ref[...] + _dot(a_ref[...].astype(BF16), wo_ref[...])
    h = _rmsnorm(x, nw_ref[...]).astype(BF16)
    for c in range(D_FF // FF_CHUNK):
        sl = slice(c * FF_CHUNK, (c + 1) * FF_CHUNK)
        gate = _dot(h, wg_ref[:, sl])
        up = _dot(h, wu_ref[:, sl])
        act_ref[:, sl] = (_silu(gate) * up).astype(BF16)
    y = x + _dot(act_ref[...], wd_ref[...])
    if final_norm:
        y = _rmsnorm(y, fw_ref[...])
    o_ref[...] = y


def _mix_ffn(x, a, w_out, out_layer, nw, wg, wu, wd, layer, final_w, final_norm):
    t, d = x.shape
    k = a.shape[1]
    tm = FFN_ROW_TILE if t % FFN_ROW_TILE == 0 else t
    return pl.pallas_call(
        functools.partial(_ffn_kernel, final_norm=final_norm),
        grid=(t // tm,),
        in_specs=[pl.BlockSpec((tm, d), lambda i: (i, 0)),
                  pl.BlockSpec((tm, k), lambda i: (i, 0)),
                  _resident_layer((k, d), out_layer),
                  _resident((1, d)), _resident_layer((d, D_FF), layer),
                  _resident_layer((d, D_FF), layer), _resident_layer((D_FF, d), layer),
                  _resident((1, d))],
        out_specs=pl.BlockSpec((tm, d), lambda i: (i, 0)),
        out_shape=jax.ShapeDtypeStruct((t, d), F32),
        scratch_shapes=[pltpu.VMEM((tm, D_FF), BF16)],
        compiler_params=_params(1),
        name="mix_ffn",
    )(x, a, w_out, nw.reshape(1, d), wg, wu, wd, final_w.reshape(1, d))


def _rotary(x, cos, sin_signed):
    lane = lax.broadcasted_iota(jnp.int32, x.shape, 1)
    partner = jnp.where((lane & 1) == 0, pltpu.roll(x, LANES - 1, 1), pltpu.roll(x, 1, 1))
    return x * cos + partner * sin_signed


def _l2norm(x):
    return x * lax.rsqrt(jnp.sum(x * x, axis=-1, keepdims=True) + EPS)


def _split3(x):
    hi = x.astype(BF16)
    r1 = x - hi.astype(F32)
    mid = r1.astype(BF16)
    lo = (r1 - mid.astype(F32)).astype(BF16)
    return hi, mid, lo


def _pack3(x):
    lane = lax.broadcasted_iota(jnp.int32, x.shape, 1)
    x = jnp.where(lane < SSM_HEADS, x, 0.0)
    hi = x.astype(BF16).astype(F32)
    r1 = x - hi
    mid = r1.astype(BF16).astype(F32)
    lo = r1 - mid
    return (hi + pltpu.roll(mid, SSM_HEADS, 1) + pltpu.roll(lo, 2 * SSM_HEADS, 1)).astype(BF16)


def _unpack3(y):
    return y + pltpu.roll(y, LANES - SSM_HEADS, 1) + pltpu.roll(y, LANES - 2 * SSM_HEADS, 1)


def _expand_heads(x, expand_mat):
    return _dot(_pack3(x), expand_mat)


def _head_expand_matrix():
    row = lax.broadcasted_iota(jnp.int32, (LANES, SSM_DINNER), 0)
    col = lax.broadcasted_iota(jnp.int32, (LANES, SSM_DINNER), 1)
    hit = ((row & (SSM_HEADS - 1)) == col // SSM_HEADDIM) & (row < 3 * SSM_HEADS)
    return jnp.where(hit, 1.0, 0.0).astype(BF16)


def _chunk_sum_matrices(n):
    ii = lax.broadcasted_iota(jnp.int32, (n, n), 0)
    jj = lax.broadcasted_iota(jnp.int32, (n, n), 1)
    same = ii // CHUNK == jj // CHUNK
    prefix = jnp.where(same & (ii >= jj), 1.0, 0.0).astype(F32)
    total = jnp.where(same, 1.0, 0.0).astype(F32)
    return prefix, total


def _columns(rows):
    pad = jnp.zeros((LANES - rows.shape[0], LANES), F32)
    return jnp.concatenate([rows, pad], axis=0).T


def _unit_lower_inverses(lows, interleaved=()):
    interleaved = list(interleaved)
    n = lows[0].shape[0]
    ii = lax.broadcasted_iota(jnp.int32, (n, n), 0)
    jj = lax.broadcasted_iota(jnp.int32, (n, n), 1)
    eye = (ii == jj).astype(F32)
    pair = (ii >> 1) == (jj >> 1)
    ts = [eye - jnp.where(pair, low, 0.0) for low in lows]
    lows_b = [low.astype(BF16) for low in lows]
    for k in range(1, 6):
        size = 1 << k
        join = ((ii >> (k + 1)) == (jj >> (k + 1))) & ((ii >> k) != (jj >> k))
        join_b = join.astype(F32).astype(BF16)
        ts_b = [t.astype(BF16) for t in ts]
        if size < SUBLANES:
            txs = [_dot(t_b, low_b * join_b).astype(BF16) for t_b, low_b in zip(ts_b, lows_b)]
            ts = [t - _dot(tx, t_b) for t, tx, t_b in zip(ts, txs, ts_b)]
        else:
            lower = [slice(b0 + size, b0 + 2 * size) for b0 in range(0, n, 2 * size)]
            upper = [slice(b0, b0 + size) for b0 in range(0, n, 2 * size)]
            t_low = [jnp.concatenate([t[r] for r in lower], axis=0) for t in ts]
            txs = [_dot(tl.astype(BF16), low_b * join_b).astype(BF16)
                   for tl, low_b in zip(t_low, lows_b)]
            new_lower = [tl - _dot(tx, t_b) for tl, tx, t_b in zip(t_low, txs, ts_b)]
            ts = [jnp.concatenate(
                      [piece for i, r in enumerate(upper)
                       for piece in (t[r], nl[i * size:(i + 1) * size])], axis=0)
                  for t, nl in zip(ts, new_lower)]
        if interleaved:
            interleaved.pop(0)()
        yield
    for fn in interleaved:
        fn()
    return [t - eye for t in ts]


def _hyb_core_kernel(pm_ref, pt_ref, alog_ref, dtb_ref, gnw_ref, gnn_ref,
                     mix_ref, rs_ref, gs_ref, sret, sgdn, intra_scr, *, n_tiles):
    t = pl.program_id(1)
    tl = SEQ_TILE
    hd = HEAD_DIM

    @pl.when(t == 0)
    def _():
        sret[...] = jnp.zeros_like(sret)
        sgdn[...] = jnp.zeros_like(sgdn)

    @pl.when((t == 0) & (pl.program_id(0) == 0))
    def _():
        ii = lax.broadcasted_iota(jnp.int32, (tl, tl), 0)
        jj = lax.broadcasted_iota(jnp.int32, (tl, tl), 1)
        diff = (ii - jj).astype(F32)
        for h in range(N_HEADS):
            intra_scr[h] = jnp.where(diff >= 0.0, jnp.exp(LOG_GAMMA[h] * jnp.maximum(diff, 0.0)), 0.0)

    idx = lax.broadcasted_iota(jnp.int32, (tl, 1), 0).astype(F32)

    def retention_head(h):
        lg = LOG_GAMMA[h]
        q = pm_ref[:, h * hd:(h + 1) * hd]
        k = pm_ref[:, _head_cols(RET_K, h)]
        v = pm_ref[:, _head_cols(RET_V, h)]
        gate_act = pm_ref[:, _head_cols(RET_GATE, h)]
        q_dec = jnp.exp(lg * (idx + 1.0))
        k_dec = jnp.exp(lg * (tl - 1.0 - idx))
        s = sret[h]
        qb = q.astype(BF16)
        vb = v.astype(BF16)
        scores = _dot_nt(qb, k.astype(BF16)) * intra_scr[h]
        o = _dot(scores.astype(BF16), vb) + _dot(qb, s.astype(BF16)) * q_dec
        sret[h] = s * math.exp(lg * tl) + _dot_tn((k * k_dec).astype(BF16), vb)
        mu = jnp.mean(o, axis=-1, keepdims=True)
        var = jnp.mean(jnp.square(o - mu), axis=-1, keepdims=True)
        ro = (o - mu) * lax.rsqrt(var + EPS) * gnw_ref[:, h * hd:(h + 1) * hd]
        mix_ref[:, h * hd:(h + 1) * hd] = (gate_act * ro).astype(BF16)

    tail = pt_ref[...]
    g_all = -jnp.exp(alog_ref[...]) * _softplus(tail + dtb_ref[...])
    beta_all = _sigmoid(tail)
    prefix, total = _chunk_sum_matrices(tl)
    g_parts = _pack3(g_all)
    gc_all = _unpack3(_dot(prefix.astype(BF16), g_parts))
    gl_all = _unpack3(_dot(total.astype(BF16), g_parts))
    yield

    n = N_HEADS * CHUNK
    si = lax.broadcasted_iota(jnp.int32, (n, n), 0)
    sj = lax.broadcasted_iota(jnp.int32, (n, n), 1)
    same_head = si // CHUNK == sj // CHUNK
    causal = same_head & (si >= sj)
    strict = same_head & (si > sj)

    def stack(fn):
        return jnp.concatenate([fn(h) for h in range(N_HEADS)], axis=0)

    chunks = []
    for c in range(tl // CHUNK):
        r = slice(c * CHUNK, (c + 1) * CHUNK)
        q_st = stack(lambda h: pm_ref[r, _head_cols(GDN_Q, h)])
        k_st = stack(lambda h: pm_ref[r, _head_cols(GDN_K, h)])
        v_st = stack(lambda h: pm_ref[r, _head_cols(GDN_V, h)])
        beta = stack(lambda h: beta_all[r, 4 + h:5 + h])
        gc = stack(lambda h: gc_all[r, h:h + 1])
        gl = stack(lambda h: gl_all[r, h:h + 1])
        gc_col = jnp.broadcast_to(gc, (n, n))
        seg = gc_col - gc_col.T
        decay = jnp.where(causal, jnp.exp(jnp.where(causal, seg, 0.0)), 0.0)
        k_b16 = k_st.astype(BF16)
        qk = _dot_nt(jnp.concatenate([q_st.astype(BF16), k_b16], axis=0), k_b16)
        egc = jnp.exp(gc)
        chunks.append(dict(
            r=r, gl=gl,
            low=jnp.where(strict, qk[n:] * beta * decay, 0.0),
            rhs=jnp.concatenate([v_st * beta, k_st * (beta * egc)], axis=1),
            scores=(qk[:n] * decay).astype(BF16),
            q_in=(q_st * egc).astype(BF16),
            k_out=(k_st * jnp.exp(gl - gc)).astype(BF16)))
        yield
    t_offs = yield from _unit_lower_inverses(
        [ch["low"] for ch in chunks],
        interleaved=[functools.partial(retention_head, h) for h in range(N_HEADS)])
    sols = [ch["rhs"] + _dot(t_off.astype(BF16), ch["rhs"].astype(BF16))
            for ch, t_off in zip(chunks, t_offs)]
    yield

    for ch, sol in zip(chunks, sols):
        r = ch["r"]
        w_b16 = sol[:, hd:].astype(BF16)
        v_new = []
        o_inter = []
        for h in range(N_HEADS):
            hr = slice(h * CHUNK, (h + 1) * CHUNK)
            lhs = jnp.concatenate([w_b16[hr], ch["q_in"][hr]], axis=0)
            prod = _dot(lhs, sgdn[h].astype(BF16))
            v_new.append(sol[hr, :hd] - prod[:CHUNK])
            o_inter.append(prod[CHUNK:])
        yield
        vn_b16 = jnp.concatenate(v_new, axis=0).astype(BF16)
        o_st = jnp.concatenate(o_inter, axis=0) + _dot(ch["scores"], vn_b16)
        yield
        for h in range(N_HEADS):
            hr = slice(h * CHUNK, (h + 1) * CHUNK)
            s_dec = jnp.exp(jnp.broadcast_to(ch["gl"][hr], (CHUNK, hd)))
            s_dec = jnp.concatenate([s_dec, s_dec], axis=0)
            sgdn[h] = sgdn[h] * s_dec + _dot_tn(ch["k_out"][hr], vn_b16[hr])
            o_h = o_st[hr]
            go = o_h * lax.rsqrt(jnp.mean(o_h * o_h, axis=-1, keepdims=True) + EPS) * gnn_ref[...]
            gz_act = pm_ref[r, _head_cols(GDN_GATE, h)]
            mix_ref[r, _head_cols(HEADS_W, h)] = (go * gz_act).astype(BF16)
        yield

    @pl.when(t == n_tiles - 1)
    def _():
        rs_ref[...] = sret[...]
        gs_ref[...] = sgdn[...]


def _hyb_core_prompt(pm, pt, alog, dtb, gnw, gnn, batch, seq):
    n_tiles = seq // SEQ_TILE
    ns = SEQS_PER_STEP
    tile = lambda b, t: (b, t, 0)
    state = pl.BlockSpec((ns, N_HEADS, HEAD_DIM, HEAD_DIM), lambda b, t: (b, 0, 0, 0))
    per_seq = lambda shape: pltpu.VMEM((ns,) + shape, F32)
    mixed, rs, gs = pl.pallas_call(
        functools.partial(_per_sequence(_hyb_core_kernel, ns, 2, 4), n_tiles=n_tiles),
        grid=(batch // ns, n_tiles),
        in_specs=[pl.BlockSpec((ns, SEQ_TILE, HYB_MAIN), tile),
                  pl.BlockSpec((ns, SEQ_TILE, LANES), tile),
                  _resident((1, LANES)), _resident((1, LANES)),
                  _resident((1, HEADS_W)), _resident((1, LANES))],
        out_specs=[pl.BlockSpec((ns, SEQ_TILE, D_MODEL), tile), state, state],
        out_shape=[jax.ShapeDtypeStruct((batch, seq, D_MODEL), BF16),
                   jax.ShapeDtypeStruct((batch, N_HEADS, HEAD_DIM, HEAD_DIM), F32),
                   jax.ShapeDtypeStruct((batch, N_HEADS, HEAD_DIM, HEAD_DIM), F32)],
        scratch_shapes=[per_seq((N_HEADS, HEAD_DIM, HEAD_DIM)),
                        per_seq((N_HEADS, HEAD_DIM, HEAD_DIM)),
                        per_seq((N_HEADS, SEQ_TILE, SEQ_TILE))],
        compiler_params=_params(2),
        name="hyb_core_prompt",
    )(pm.reshape(batch, seq, HYB_MAIN), pt.reshape(batch, seq, LANES), alog, dtb, gnw, gnn)
    return mixed.reshape(batch * seq, D_MODEL), rs, gs


def _group_rmsnorm_gate(y, z, nw_ref, o_ref, rows):
    for g in range(SSM_GROUPS):
        sl = slice(g * SSM_GROUP_W, (g + 1) * SSM_GROUP_W)
        yg = y[:, sl] * _silu(z[:, sl])
        yg = yg * lax.rsqrt(jnp.mean(yg * yg, axis=-1, keepdims=True) + EPS)
        o_ref[rows, sl] = (yg * nw_ref[:, sl]).astype(o_ref.dtype)


def _ssm_core_kernel(pm_ref, pt_ref, alog_ref, dtb_ref, dskip_ref, nw_ref,
                     y_ref, st_ref, s_t, xdt, xdec, eacs, acs_e, y_scr, *, n_tiles):
    t = pl.program_id(1)
    tl = SEQ_TILE
    xbc = pm_ref.at[:, SSM_DINNER:SSM_MAIN]

    @pl.when(t == 0)
    def _():
        s_t[...] = jnp.zeros_like(s_t)

    n_chunks = tl // CHUNK
    expand = _head_expand_matrix()
    dt = _softplus(pt_ref[...] + dtb_ref[...])
    a = dt * (-jnp.exp(alog_ref[...]))
    prefix, total = _chunk_sum_matrices(tl)
    a_parts = _pack3(a)
    acs = _unpack3(_dot(prefix.astype(BF16), a_parts))
    atot = _unpack3(_dot(total.astype(BF16), a_parts))
    dt_parts = _pack3(dt)
    acs_parts = _pack3(acs)
    chunk_rows = [atot[c * CHUNK:c * CHUNK + 1] for c in range(n_chunks)]
    chunk_rows.append(jnp.zeros((SUBLANES - n_chunks, LANES), F32))
    atot_parts = _pack3(jnp.concatenate(chunk_rows, axis=0))

    def expand_group(parts, g):
        return _dot(parts, expand[:, g * SSM_GROUP_W:(g + 1) * SSM_GROUP_W])

    yield
    chunk_dec = []
    for g in range(SSM_GROUPS):
        gs = slice(g * SSM_GROUP_W, (g + 1) * SSM_GROUP_W)
        x_dt = xbc[:, gs] * expand_group(dt_parts, g)
        xdt[:, gs] = x_dt
        yield
        acs_g = expand_group(acs_parts, g)
        acs_e[:, gs] = acs_g
        eacs[:, gs] = jnp.exp(acs_g)
        yield
        atot_g = expand_group(atot_parts, g)
        chunk_dec.append(jnp.exp(atot_g))
        for c in range(n_chunks):
            r = slice(c * CHUNK, (c + 1) * CHUNK)
            to_end = jnp.exp(atot_g[c:c + 1, :] - acs_g[r])
            xdec[r, gs] = (x_dt[r] * to_end).astype(BF16)
        yield

    lane = lax.broadcasted_iota(jnp.int32, (CHUNK, LANES), 1)
    row_i = lax.broadcasted_iota(jnp.int32, (CHUNK, LANES), 0)
    first_half = lane < SSM_HEADDIM
    causal2 = row_i >= (lane & (CHUNK - 1))
    lane2 = lax.broadcasted_iota(jnp.int32, (2 * CHUNK, LANES), 1)
    row2 = lax.broadcasted_iota(jnp.int32, (2 * CHUNK, LANES), 0)
    pair_diag = (row2 < CHUNK) == (lane2 < SSM_HEADDIM)

    def bc_cc(r, g):
        bc = xbc[r, SSM_DINNER + g * LANES:SSM_DINNER + (g + 1) * LANES]
        cc = xbc[r, SSM_DINNER + 512 + g * LANES:SSM_DINNER + 512 + (g + 1) * LANES]
        return bc.astype(BF16), cc.astype(BF16)

    for c in range(n_chunks):
        r = slice(c * CHUNK, (c + 1) * CHUNK)
        acs_c = acs[r]
        acs_t = jnp.concatenate([acs_c, acs_c], axis=0).T
        cb2 = []
        for g in range(SSM_GROUPS):
            bc_b16, cc_b16 = bc_cc(r, g)
            cb2.append(_dot_nt(cc_b16, jnp.concatenate([bc_b16, bc_b16], axis=0)))
        yield
        for p in range(SSM_HEADS // 2):
            ha = 2 * p
            rowv = jnp.where(first_half, acs_t[ha:ha + 1, :], acs_t[ha + 1:ha + 2, :])
            seg = acs_e[r, p * LANES:(p + 1) * LANES] - rowv
            lmat = jnp.where(causal2, jnp.exp(jnp.where(causal2, seg, 0.0)), 0.0)
            m = (cb2[p // 4] * lmat).astype(BF16)
            xblk = xdt[r, p * LANES:(p + 1) * LANES]
            xblk2 = jnp.concatenate([xblk, xblk], axis=0)
            x_bd = jnp.where(pair_diag, xblk2, 0.0).astype(BF16)
            y_scr[r, p * LANES:(p + 1) * LANES] = _dot(m, x_bd)
            yield

    for c in range(n_chunks):
        r = slice(c * CHUNK, (c + 1) * CHUNK)
        groups = range(SSM_GROUPS)
        gsl = [slice(g * SSM_GROUP_W, (g + 1) * SSM_GROUP_W) for g in groups]
        bcs, ccs = zip(*[bc_cc(r, g) for g in groups])
        s_old = [s_t[:, gsl[g]] for g in groups]
        y_inter = [_dot(ccs[g], s_old[g].astype(BF16)) for g in groups]
        yield
        s_add = [_dot_tn(bcs[g], xdec[r, gsl[g]]) for g in groups]
        yield
        for g in groups:
            s_t[:, gsl[g]] = s_old[g] * chunk_dec[g][c:c + 1, :] + s_add[g]
            y_scr[r, gsl[g]] = y_scr[r, gsl[g]] + y_inter[g] * eacs[r, gsl[g]]
        yield

    for g in range(SSM_GROUPS):
        gs = slice(g * SSM_GROUP_W, (g + 1) * SSM_GROUP_W)
        yg = (y_scr[:, gs] + xbc[:, gs] * dskip_ref[:, gs]) * pm_ref[:, gs]
        yg = yg * lax.rsqrt(jnp.mean(yg * yg, axis=-1, keepdims=True) + EPS)
        y_ref[:, gs] = (yg * nw_ref[:, gs]).astype(y_ref.dtype)
        yield

    @pl.when(t == n_tiles - 1)
    def _():
        for kblk in range(SSM_DINNER // LANES):
            ks = slice(kblk * LANES, (kblk + 1) * LANES)
            st_ref[ks, :] = s_t[:, ks].T


def _per_sequence(body, n_seq, n_tiled_in, n_shared_in):
    def kernel(*refs, **kw):
        tiled = refs[:n_tiled_in]
        shared = refs[n_tiled_in:n_tiled_in + n_shared_in]
        rest = refs[n_tiled_in + n_shared_in:]
        running = [body(*[r.at[s] for r in tiled], *shared, *[r.at[s] for r in rest], **kw)
                   for s in range(n_seq)]
        while running:
            for gen in list(running):
                if next(gen, StopIteration) is StopIteration:
                    running.remove(gen)
    return kernel


def _ssm_core_prompt(pm, pt, alog, dtb, dskip, nw, batch, seq):
    n_tiles = seq // SEQ_TILE
    ns = SEQS_PER_STEP
    tile = lambda b, t: (b, t, 0)
    per_seq = lambda shape, dtype: pltpu.VMEM((ns,) + shape, dtype)
    y, st = pl.pallas_call(
        functools.partial(_per_sequence(_ssm_core_kernel, ns, 2, 4), n_tiles=n_tiles),
        grid=(batch // ns, n_tiles),
        in_specs=[pl.BlockSpec((ns, SEQ_TILE, SSM_MAIN), tile),
                  pl.BlockSpec((ns, SEQ_TILE, LANES), tile),
                  _resident((1, LANES)), _resident((1, LANES)),
                  _resident((1, SSM_DINNER)), _resident((1, SSM_DINNER))],
        out_specs=[pl.BlockSpec((ns, SEQ_TILE, SSM_DINNER), tile),
                   pl.BlockSpec((ns, SSM_DINNER, SSM_DSTATE), lambda b, t: (b, 0, 0))],
        out_shape=[jax.ShapeDtypeStruct((batch, seq, SSM_DINNER), BF16),
                   jax.ShapeDtypeStruct((batch, SSM_DINNER, SSM_DSTATE), F32)],
        scratch_shapes=[per_seq((SSM_DSTATE, SSM_DINNER), F32),
                        per_seq((SEQ_TILE, SSM_DINNER), F32),
                        per_seq((SEQ_TILE, SSM_DINNER), BF16),
                        per_seq((SEQ_TILE, SSM_DINNER), F32),
                        per_seq((SEQ_TILE, SSM_DINNER), F32),
                        per_seq((SEQ_TILE, SSM_DINNER), F32)],
        compiler_params=_params(2),
        name="ssm_core_prompt",
    )(pm.reshape(batch, seq, SSM_MAIN), pt.reshape(batch, seq, LANES), alog, dtb, dskip, nw)
    return y.reshape(batch * seq, SSM_DINNER), st


def _hyb_step_front_kernel(pm_ref, conv_ref, cos_ref, sin_ref, cw_ref, o_ref, conv_out):
    hd = HEAD_DIM
    ch = GDN_CONV_CH
    cos = cos_ref[...]
    sin = sin_ref[...]
    for h in range(N_HEADS):
        qs, ks = _head_cols(RET_Q, h), _head_cols(RET_K, h)
        o_ref[:, qs] = _rotary(pm_ref[:, qs], cos, sin)
        o_ref[:, ks] = _rotary(pm_ref[:, ks], cos, sin) * (hd ** -0.5)
    o_ref[:, RET_V:RET_GATE] = pm_ref[:, RET_V:RET_GATE]
    o_ref[:, RET_GATE:GDN_Q] = _silu(pm_ref[:, RET_GATE:GDN_Q])
    o_ref[:, GDN_GATE:HYB_MAIN] = _silu(pm_ref[:, GDN_GATE:HYB_MAIN])
    for blk in range(ch // LANES):
        cs = slice(blk * LANES, (blk + 1) * LANES)
        u = pm_ref[:, GDN_Q + blk * LANES:GDN_Q + (blk + 1) * LANES]
        c0 = conv_ref[:, blk * LANES:(blk + 1) * LANES]
        c1 = conv_ref[:, ch + blk * LANES:ch + (blk + 1) * LANES]
        c2 = conv_ref[:, 2 * ch + blk * LANES:2 * ch + (blk + 1) * LANES]
        conv = c0 * cw_ref[0:1, cs]
        conv = conv + c1 * cw_ref[1:2, cs]
        conv = conv + c2 * cw_ref[2:3, cs]
        conv = conv + u * cw_ref[3:4, cs]
        act = _silu(conv)
        if blk < N_HEADS:
            act = _l2norm(act) * (hd ** -0.5)
        elif blk < 2 * N_HEADS:
            act = _l2norm(act)
        o_ref[:, GDN_Q + blk * LANES:GDN_Q + (blk + 1) * LANES] = act
        conv_out[:, blk * LANES:(blk + 1) * LANES] = c1
        conv_out[:, ch + blk * LANES:ch + (blk + 1) * LANES] = c2
        conv_out[:, 2 * ch + blk * LANES:2 * ch + (blk + 1) * LANES] = u


def _hyb_step_front(pm, conv_state, cos, sin, cw_t):
    t = pm.shape[0]
    full = lambda a: pl.BlockSpec(a.shape, lambda i: (0,) * a.ndim)
    return pl.pallas_call(
        _hyb_step_front_kernel,
        grid=(1,),
        in_specs=[full(pm), full(conv_state), full(cos), full(sin), full(cw_t)],
        out_specs=[full(pm), full(conv_state)],
        out_shape=[jax.ShapeDtypeStruct(pm.shape, F32), jax.ShapeDtypeStruct(conv_state.shape, F32)],
        compiler_params=_params(1),
        name="hyb_step_front",
    )(pm, conv_state, cos, sin, cw_t)


def _hyb_step_kernel(pm_ref, pt_ref, rs_in, gs_in, alog_ref, dtb_ref, gnw_ref, gnn_ref,
                     mix_ref, rs_out, gs_out, o_scr):
    bb = STEP_BATCH
    hd = HEAD_DIM
    tail = pt_ref[...]
    g_all = -jnp.exp(alog_ref[...]) * _softplus(tail + dtb_ref[...])
    beta_all = _sigmoid(tail)

    for h in range(N_HEADS):
        q = pm_ref[:, h * hd:(h + 1) * hd]
        k = pm_ref[:, _head_cols(RET_K, h)]
        v = pm_ref[:, _head_cols(RET_V, h)]
        k_cols = _columns(k)
        gamma = math.exp(LOG_GAMMA[h])
        for b in range(bb):
            s = rs_in[b, h] * gamma + k_cols[:, b:b + 1] * v[b:b + 1, :]
            rs_out[b, h] = s
            o_scr[b:b + 1, h * hd:(h + 1) * hd] = _dot(q, s)[b:b + 1, :]
        gq = pm_ref[:, _head_cols(GDN_Q, h)]
        gk = pm_ref[:, _head_cols(GDN_K, h)]
        gv = pm_ref[:, _head_cols(GDN_V, h)]
        k_cols = _columns(gk)
        decay = jnp.broadcast_to(jnp.exp(g_all[:, h:h + 1]), (bb, hd))
        beta = jnp.broadcast_to(beta_all[:, 4 + h:5 + h], (bb, hd))
        for b in range(bb):
            s = gs_in[b, h] * decay[b:b + 1, :]
            delta = (gv[b:b + 1, :] - _dot(gk, s)[b:b + 1, :]) * beta[b:b + 1, :]
            s = s + k_cols[:, b:b + 1] * delta
            gs_out[b, h] = s
            o_scr[b:b + 1, _head_cols(HEADS_W, h)] = _dot(gq, s)[b:b + 1, :]

    for h in range(N_HEADS):
        o = o_scr[:, h * hd:(h + 1) * hd]
        mu = jnp.mean(o, axis=-1, keepdims=True)
        var = jnp.mean(jnp.square(o - mu), axis=-1, keepdims=True)
        ro = (o - mu) * lax.rsqrt(var + EPS) * gnw_ref[:, h * hd:(h + 1) * hd]
        mix_ref[:, h * hd:(h + 1) * hd] = pm_ref[:, _head_cols(RET_GATE, h)] * ro
        go = o_scr[:, _head_cols(HEADS_W, h)]
        go = go * lax.rsqrt(jnp.mean(go * go, axis=-1, keepdims=True) + EPS) * gnn_ref[...]
        mix_ref[:, _head_cols(HEADS_W, h)] = go * pm_ref[:, _head_cols(GDN_GATE, h)]


def _layer_chained_outputs(state_arrays, layer, partial):
    shapes = [jax.ShapeDtypeStruct(s.shape, s.dtype) for s in state_arrays]
    if layer == 0:
        return [], [], shapes
    return list(partial), [pl.BlockSpec(memory_space=pl.ANY)] * len(partial), shapes


def _hyb_core_step(pm_act, pt, ret_all, gdn_all, layer, partial, alog, dtb, gnw, gnn):
    batch = pm_act.shape[0]
    bb = STEP_BATCH
    row = lambda i: (i, 0)
    state = pl.BlockSpec((None, bb, N_HEADS, HEAD_DIM, HEAD_DIM), lambda i: (layer, i, 0, 0, 0))
    extra, extra_specs, state_shapes = _layer_chained_outputs((ret_all, gdn_all), layer, partial)
    n_in = 8
    aliases = {n_in + k: 1 + k for k in range(len(extra))}

    def body(*refs):
        _hyb_step_kernel(*refs[:n_in], *refs[n_in + len(extra):])

    return pl.pallas_call(
        body,
        grid=(batch // bb,),
        in_specs=[pl.BlockSpec((bb, HYB_MAIN), row), pl.BlockSpec((bb, LANES), row), state, state,
                  _resident((1, LANES)), _resident((1, LANES)),
                  _resident((1, HEADS_W)), _resident((1, LANES))] + extra_specs,
        out_specs=[pl.BlockSpec((bb, D_MODEL), row), state, state],
        out_shape=[jax.ShapeDtypeStruct((batch, D_MODEL), F32)] + state_shapes,
        scratch_shapes=[pltpu.VMEM((bb, D_MODEL), F32)],
        input_output_aliases=aliases,
        compiler_params=_params(1),
        name="hyb_core_step",
    )(pm_act, pt, ret_all, gdn_all, alog, dtb, gnw, gnn, *extra)


def _ssm_step_kernel(pm_ref, pt_ref, conv_ref, s_in, cw_ref, cbias_ref, alog_ref, dtb_ref,
                     dskip_ref, nw_ref, y_ref, conv_out, s_out, y_scr):
    bb = STEP_BATCH
    ch = SSM_CONV_CH
    u = pm_ref[:, SSM_DINNER:SSM_MAIN]
    c0 = conv_ref[:, 0:ch]
    c1 = conv_ref[:, ch:2 * ch]
    c2 = conv_ref[:, 2 * ch:3 * ch]
    conv = c0 * cw_ref[0:1, :]
    conv = conv + c1 * cw_ref[1:2, :]
    conv = conv + c2 * cw_ref[2:3, :]
    conv = conv + u * cw_ref[3:4, :]
    xbc = _silu(conv + cbias_ref[...])
    conv_out[:, 0:ch] = c1
    conv_out[:, ch:2 * ch] = c2
    conv_out[:, 2 * ch:3 * ch] = u
    xs = xbc[:, 0:SSM_DINNER]
    expand = _head_expand_matrix()
    dt = _softplus(pt_ref[...] + dtb_ref[...])
    a = dt * (-jnp.exp(alog_ref[...]))
    x_dt = xs * _expand_heads(dt, expand)
    decay = jnp.exp(a)
    half = SSM_HEADDIM
    for kblk in range(SSM_DINNER // LANES):
        ks = slice(kblk * LANES, (kblk + 1) * LANES)
        g = kblk // 4
        x_cols = _columns(x_dt[:, ks])
        bm = xbc[:, SSM_DINNER + g * LANES:SSM_DINNER + (g + 1) * LANES]
        cm = xbc[:, SSM_DINNER + 512 + g * LANES:SSM_DINNER + 512 + (g + 1) * LANES]
        for b in range(bb):
            d0 = decay[b:b + 1, 2 * kblk:2 * kblk + 1]
            d1 = decay[b:b + 1, 2 * kblk + 1:2 * kblk + 2]
            s_dec = jnp.concatenate([s_in[b, kblk * LANES:kblk * LANES + half, :] * d0,
                                     s_in[b, kblk * LANES + half:(kblk + 1) * LANES, :] * d1], axis=0)
            s = s_dec + x_cols[:, b:b + 1] * bm[b:b + 1, :]
            s_out[b, ks, :] = s
            y_scr[b:b + 1, ks] = _dot_nt(cm, s)[b:b + 1, :]
    y = y_scr[...] + xs * dskip_ref[...]
    _group_rmsnorm_gate(y, pm_ref[:, 0:SSM_DINNER], nw_ref, y_ref, slice(None))


def _ssm_core_step(pm, pt, conv_state, ssm_all, layer, partial, cw_t, cbias, alog, dtb, dskip, nw):
    batch = pm.shape[0]
    bb = STEP_BATCH
    row = lambda i: (i, 0)
    state = pl.BlockSpec((None, bb, SSM_DINNER, SSM_DSTATE), lambda i: (layer, i, 0, 0))
    extra, extra_specs, state_shapes = _layer_chained_outputs((ssm_all,), layer, partial)
    n_in = 10
    aliases = {n_in + k: 2 + k for k in range(len(extra))}

    def body(*refs):
        _ssm_step_kernel(*refs[:n_in], *refs[n_in + len(extra):])

    return pl.pallas_call(
        body,
        grid=(batch // bb,),
        in_specs=[pl.BlockSpec((bb, SSM_MAIN), row), pl.BlockSpec((bb, LANES), row),
                  pl.BlockSpec((bb, 3 * SSM_CONV_CH), row), state,
                  _resident((CONV_W, SSM_CONV_CH)), _resident((1, SSM_CONV_CH)),
                  _resident((1, LANES)), _resident((1, LANES)),
                  _resident((1, SSM_DINNER)), _resident((1, SSM_DINNER))] + extra_specs,
        out_specs=[pl.BlockSpec((bb, SSM_DINNER), row), pl.BlockSpec((bb, 3 * SSM_CONV_CH), row),
                   state],
        out_shape=[jax.ShapeDtypeStruct((batch, SSM_DINNER), F32),
                   jax.ShapeDtypeStruct((batch, 3 * SSM_CONV_CH), F32)] + state_shapes,
        scratch_shapes=[pltpu.VMEM((bb, SSM_DINNER), F32)],
        input_output_aliases=aliases,
        compiler_params=_params(1),
        name="ssm_core_step",
    )(pm, pt, conv_state, ssm_all, cw_t, cbias, alog, dtb, dskip, nw, *extra)


def _rope_tables(pos):
    inv_freq = 1.0 / (ROPE_BASE ** jnp.linspace(0.0, 1.0, HEAD_DIM // 2, dtype=F32))
    ang = pos.astype(F32)[:, None] * inv_freq[None, :]
    cos = jnp.repeat(jnp.cos(ang), 2, axis=-1)
    sign = jnp.tile(jnp.array([-1.0, 1.0], F32), HEAD_DIM // 2)
    sin = jnp.repeat(jnp.sin(ang), 2, axis=-1) * sign
    return cos, sin


def _lane_row(v):
    return jnp.pad(v.astype(F32), (0, LANES - v.shape[0])).reshape(1, LANES)


def kernel(x_prompt, x_sample, state_ret, state_gdn, state_gdn_conv, state_ssm, state_ssm_conv,
           norm_mix_w, norm_ffn_w, norm_final_w, hyb_w_in, hyb_w_out, ret_gn_w, gdn_conv_w,
           gdn_a_log, gdn_dt_bias, gdn_norm_w, ssm_w_in, ssm_conv_w, ssm_conv_b, ssm_dt_bias,
           ssm_a_log, ssm_d, ssm_norm_w, ssm_w_out, ffn_w_gate, ffn_w_up, ffn_w_down):
    depth = norm_mix_w.shape[0]
    bp, seq, d = x_prompt.shape
    bs = x_sample.shape[0]
    past_len = 16384

    hyb_in = hyb_w_in.astype(BF16)
    hyb_tail = jnp.pad(hyb_w_in[:, :, HYB_MAIN:], ((0, 0), (0, 0), (0, LANES - 2 * N_HEADS))).astype(BF16)
    ssm_in = ssm_w_in.astype(BF16)
    ssm_tail = jnp.pad(ssm_w_in[:, :, SSM_MAIN:], ((0, 0), (0, 0), (0, LANES - SSM_HEADS))).astype(BF16)
    hyb_out = hyb_w_out.astype(BF16)
    ssm_out = ssm_w_out.astype(BF16)
    w_gate = ffn_w_gate.astype(BF16)
    w_up = ffn_w_up.astype(BF16)
    w_down = ffn_w_down.astype(BF16)

    gdn_cw_t = jnp.swapaxes(gdn_conv_w, 1, 2)
    ssm_cw_t = jnp.swapaxes(ssm_conv_w, 1, 2)
    dskip = jnp.repeat(ssm_d, SSM_HEADDIM, axis=-1)
    cos_p, sin_p = _rope_tables(jnp.arange(seq))
    cos_s, sin_s = _rope_tables(jnp.arange(past_len, past_len + 1))

    ssm_state_rows = state_ssm.reshape(state_ssm.shape[0], bs, SSM_DINNER, SSM_DSTATE)

    def trunk(x, prompt):
        t = x.shape[0]
        ret_l, gdn_l, gconv_l, ssm_l, sconv_l = [], [], [], [], []
        hyb_states, ssm_states = (), ()
        for i in range(depth):
            j = i // 2
            if i % 2 == 0:
                args = (_lane_row(gdn_a_log[j]), _lane_row(gdn_dt_bias[j]),
                        ret_gn_w[j].reshape(1, HEADS_W), gdn_norm_w[j].reshape(1, LANES))
                if prompt:
                    rope = pl.BlockSpec((_row_tile(t), LANES), lambda r: (r % (seq // _row_tile(t)), 0))
                    pm, pt, cb = _prompt_proj(
                        _hyb_proj_kernel, x, norm_mix_w[i], hyb_in, j, HYB_MAIN, hyb_tail[j],
                        GDN_CONV_CH, seq, (cos_p, sin_p, gdn_cw_t[j]),
                        [rope, rope, _resident((CONV_W, GDN_CONV_CH))])
                    mixed, rs, gs = _hyb_core_prompt(pm, pt, *args, bp, seq)
                    cb = cb[:, SUBLANES - (CONV_W - 1):, :]
                    ret_l.append(rs)
                    gdn_l.append(gs)
                else:
                    pm, pt = _norm_proj(x, norm_mix_w[i], hyb_in, j, HYB_MAIN, hyb_tail[j])
                    pm_act, cb = _hyb_step_front(pm, state_gdn_conv[j].reshape(t, -1), cos_s, sin_s,
                                                 gdn_cw_t[j])
                    mixed, *hyb_states = _hyb_core_step(pm_act, pt, state_ret, state_gdn, j,
                                                        hyb_states, *args)
                    cb = cb.reshape(t, CONV_W - 1, GDN_CONV_CH)
                gconv_l.append(cb)
                w_mix = hyb_out
            else:
                args = (_lane_row(ssm_a_log[j]), _lane_row(ssm_dt_bias[j]), dskip[j].reshape(1, -1),
                        ssm_norm_w[j].reshape(1, -1))
                conv_args = (ssm_cw_t[j], ssm_conv_b[j].reshape(1, -1))
                if prompt:
                    pm, pt, cb = _prompt_proj(
                        _ssm_proj_kernel, x, norm_mix_w[i], ssm_in, j, SSM_MAIN, ssm_tail[j],
                        SSM_CONV_CH, seq, conv_args,
                        [_resident((CONV_W, SSM_CONV_CH)), _resident((1, SSM_CONV_CH))])
                    y, ss = _ssm_core_prompt(pm, pt, *args, bp, seq)
                    cb = cb[:, SUBLANES - (CONV_W - 1):, :]
                    ssm_l.append(ss.reshape(-1, SSM_HEADS, SSM_HEADDIM, SSM_DSTATE))
                else:
                    pm, pt = _norm_proj(x, norm_mix_w[i], ssm_in, j, SSM_MAIN, ssm_tail[j])
                    y, cb, *ssm_states = _ssm_core_step(
                        pm, pt, state_ssm_conv[j].reshape(t, -1), ssm_state_rows, j, ssm_states,
                        *conv_args, *args)
                    cb = cb.reshape(t, CONV_W - 1, SSM_CONV_CH)
                sconv_l.append(cb)
                mixed, w_mix = y, ssm_out
            x = _mix_ffn(x, mixed, w_mix, j, norm_ffn_w[i], w_gate, w_up, w_down, i, norm_final_w,
                         final_norm=(i == depth - 1))
        if prompt:
            ret_new, gdn_new, ssm_new = jnp.stack(ret_l), jnp.stack(gdn_l), jnp.stack(ssm_l)
        else:
            ret_new, gdn_new = hyb_states
            ssm_new = ssm_states[0].reshape(state_ssm.shape)
        return x, ret_new, gdn_new, jnp.stack(gconv_l), ssm_new, jnp.stack(sconv_l)

    yp, ret_p, gdn_p, gconv_p, ssm_p, sconv_p = trunk(x_prompt.reshape(bp * seq, d), True)
    ys, ret_s, gdn_s, gconv_s, ssm_s, sconv_s = trunk(x_sample.reshape(bs, d), False)
    return (yp.reshape(bp, seq, d), ys.reshape(bs, 1, d),
            ret_p, gdn_p, gconv_p, ssm_p, sconv_p,
            ret_s, gdn_s, gconv_s, ssm_s, sconv_s)
```

```python
import functools
import math

import jax
import jax.numpy as jnp
from jax import lax
from jax.experimental import pallas as pl
from jax.experimental.pallas import tpu as pltpu

F32 = jnp.float32
BF16 = jnp.bfloat16

D_MODEL = 1024
N_HEADS = 4
HEAD_DIM = 128
HEADS_W = N_HEADS * HEAD_DIM
HYB_MAIN = 8 * HEADS_W
RET_Q, RET_K, RET_V, RET_GATE, GDN_Q, GDN_K, GDN_V, GDN_GATE = (i * HEADS_W for i in range(8))
GDN_CONV_CH = 1536
SSM_DINNER = 2048
SSM_HEADS = 32
SSM_HEADDIM = 64
SSM_GROUPS = 4
SSM_DSTATE = 128
SSM_GROUP_W = SSM_DINNER // SSM_GROUPS
SSM_CONV_CH = 3072
SSM_MAIN = 5120
D_FF = 2816
FF_CHUNK = 256
CONV_W = 4
EPS = 1e-6
ROPE_BASE = 10000.0

LANES = 128
SUBLANES = 8
VMEM_LIMIT = 56 * 1024 * 1024

PROJ_ROW_TILE = 512
PROJ_PIECE_LANES = 256
FFN_ROW_TILE = 512
SEQ_TILE = 256
SEQS_PER_STEP = 2
CHUNK = 64
STEP_BATCH = 8
LOG_GAMMA = tuple(math.log1p(-2.0 ** (-5.0 - h)) for h in range(N_HEADS))

_NT = (((1,), (1,)), ((), ()))
_TN = (((0,), (0,)), ((), ()))


def _dot(a, b):
    return jnp.dot(a, b, preferred_element_type=F32)


def _dot_nt(a, b):
    return lax.dot_general(a, b, _NT, preferred_element_type=F32)


def _dot_tn(a, b):
    return lax.dot_general(a, b, _TN, preferred_element_type=F32)


_NEG_LOG2_E = -1.0 / math.log(2.0)


def _exp_neg(x):
    return jnp.exp2(x * _NEG_LOG2_E)


def _silu(x):
    return x / (1.0 + _exp_neg(x))


def _sigmoid(x):
    return 1.0 / (1.0 + _exp_neg(x))


def _softplus(x):
    return jnp.maximum(x, 0.0) + jnp.log1p(jnp.exp(-jnp.abs(x)))


def _rmsnorm(x, w):
    return x * lax.rsqrt(jnp.mean(x * x, axis=-1, keepdims=True) + EPS) * w


def _resident(shape):
    nd = len(shape)
    return pl.BlockSpec(shape, lambda *_: (0,) * nd, pipeline_mode=pl.Buffered(1))


def _resident_layer(shape, layer):
    nd = len(shape)
    return pl.BlockSpec((None,) + tuple(shape), lambda *_: (layer,) + (0,) * nd,
                        pipeline_mode=pl.Buffered(1))


def _params(n_axes):
    return pltpu.CompilerParams(dimension_semantics=("arbitrary",) * n_axes,
                                vmem_limit_bytes=VMEM_LIMIT)


def _head_cols(base, h):
    return slice(base + h * HEAD_DIM, base + (h + 1) * HEAD_DIM)


def _row_tile(t):
    return 512 if t % 512 == 0 else t


def _norm_proj_kernel(x_ref, nw_ref, w_ref, wt_ref, o_ref, ot_ref, *, n_chunk):
    h = _rmsnorm(x_ref[...], nw_ref[...]).astype(BF16)
    n = w_ref.shape[1]
    for c in range(n // n_chunk):
        sl = slice(c * n_chunk, (c + 1) * n_chunk)
        o_ref[:, sl] = _dot(h, w_ref[:, sl])
    ot_ref[...] = _dot(h, wt_ref[...])


def _norm_proj(x, nw, w_all, layer, n, w_tail):
    t, d = x.shape
    tm = _row_tile(t)
    return pl.pallas_call(
        functools.partial(_norm_proj_kernel, n_chunk=512),
        grid=(t // tm,),
        in_specs=[pl.BlockSpec((tm, d), lambda i: (i, 0)),
                  _resident((1, d)), _resident_layer((d, n), layer), _resident((d, LANES))],
        out_specs=[pl.BlockSpec((tm, n), lambda i: (i, 0)),
                   pl.BlockSpec((tm, LANES), lambda i: (i, 0))],
        out_shape=[jax.ShapeDtypeStruct((t, n), F32), jax.ShapeDtypeStruct((t, LANES), F32)],
        compiler_params=_params(1),
        name="norm_proj",
    )(x, nw.reshape(1, d), w_all, w_tail)


def _causal_conv_block(u, xx, cw_ref, cs):
    n = u.shape[0]
    xx[SUBLANES:SUBLANES + n, cs] = u
    ext = xx[:, cs]
    prev = pltpu.roll(ext, 1, 0)
    newer = ext * cw_ref[3:4, cs] + prev * cw_ref[2:3, cs]
    older = ext * cw_ref[1:2, cs] + prev * cw_ref[0:1, cs]
    return (newer + pltpu.roll(older, 2, 0))[SUBLANES:, :]


def _prompt_proj_body(x_ref, nw_ref, w_ref, wt_ref, o_ref, ot_ref, cb_ref, xx, tiles_per_seq, epilogue):
    t = lax.rem(pl.program_id(0), tiles_per_seq)
    tm = x_ref.shape[0]
    piece = PROJ_PIECE_LANES
    per_piece = piece // LANES

    @pl.when(t == 0)
    def _():
        xx[0:SUBLANES, :] = jnp.zeros((SUBLANES, xx.shape[1]), F32)

    h = _rmsnorm(x_ref[...], nw_ref[...]).astype(BF16)
    n_pieces = w_ref.shape[1] // piece
    for p in range(n_pieces):
        acc = _dot(h, w_ref[:, p * piece:(p + 1) * piece])
        for j in range(per_piece):
            blk = p * per_piece + j
            o_ref[:, blk * LANES:(blk + 1) * LANES] = epilogue(
                blk // 4, blk % 4, acc[:, j * LANES:(j + 1) * LANES])
    ot_ref[...] = _dot(h, wt_ref[...])
    xx[0:SUBLANES, :] = xx[tm:tm + SUBLANES, :]

    @pl.when(t == tiles_per_seq - 1)
    def _():
        cb_ref[0] = xx[tm:tm + SUBLANES, :]


def _hyb_proj_kernel(x_ref, nw_ref, w_ref, wt_ref, cos_ref, sin_ref, cw_ref, o_ref, ot_ref, cb_ref,
                     xx, *, tiles_per_seq):
    cos = cos_ref[...]
    sin = sin_ref[...]
    scale = HEAD_DIM ** -0.5

    def epilogue(c, j, blk):
        if c == 0:
            return _rotary(blk, cos, sin)
        if c == 1:
            return _rotary(blk, cos, sin) * scale
        if c == 2:
            return blk
        if c in (3, 7):
            return _silu(blk)
        ch = (c - 4) * 4 + j
        act = _silu(_causal_conv_block(blk, xx, cw_ref, slice(ch * LANES, (ch + 1) * LANES)))
        if c == 4:
            return _l2norm(act) * scale
        if c == 5:
            return _l2norm(act)
        return act

    _prompt_proj_body(x_ref, nw_ref, w_ref, wt_ref, o_ref, ot_ref, cb_ref, xx, tiles_per_seq, epilogue)


def _ssm_proj_kernel(x_ref, nw_ref, w_ref, wt_ref, cw_ref, cbias_ref, o_ref, ot_ref, cb_ref, xx,
                     *, tiles_per_seq):
    def epilogue(c, j, blk):
        if c < 4:
            return _silu(blk)
        ch = (c - 4) * 4 + j
        cs = slice(ch * LANES, (ch + 1) * LANES)
        return _silu(_causal_conv_block(blk, xx, cw_ref, cs) + cbias_ref[:, cs])

    _prompt_proj_body(x_ref, nw_ref, w_ref, wt_ref, o_ref, ot_ref, cb_ref, xx, tiles_per_seq, epilogue)


def _prompt_proj(kernel_fn, x, nw, w_all, layer, n, w_tail, conv_ch, seq, extra, extra_specs):
    t, d = x.shape
    tm = min(PROJ_ROW_TILE, seq)
    tiles_per_seq = seq // tm
    return pl.pallas_call(
        functools.partial(kernel_fn, tiles_per_seq=tiles_per_seq),
        grid=(t // tm,),
        in_specs=[pl.BlockSpec((tm, d), lambda i: (i, 0)),
                  _resident((1, d)), _resident_layer((d, n), layer), _resident((d, LANES))] + extra_specs,
        out_specs=[pl.BlockSpec((tm, n), lambda i: (i, 0)),
                   pl.BlockSpec((tm, LANES), lambda i: (i, 0)),
                   pl.BlockSpec((1, SUBLANES, conv_ch), lambda i: (i // tiles_per_seq, 0, 0))],
        out_shape=[jax.ShapeDtypeStruct((t, n), F32), jax.ShapeDtypeStruct((t, LANES), F32),
                   jax.ShapeDtypeStruct((t // seq, SUBLANES, conv_ch), F32)],
        scratch_shapes=[pltpu.VMEM((tm + SUBLANES, conv_ch), F32)],
        compiler_params=_params(1),
        name=kernel_fn.__name__.strip("_"),
    )(x, nw.reshape(1, d), w_all, w_tail, *extra)


def _ffn_kernel(x_ref, a_ref, wo_ref, nw_ref, wg_ref, wu_ref, wd_ref, fw_ref, o_ref, act_ref, *,
                final_norm):
    x = x_ref[...] + _dot(a_ref[...].astype(BF16), wo_ref[...])
    h = _rmsnorm(x, nw_ref[...]).astype(BF16)
    for c in range(D_FF // FF_CHUNK):
        sl = slice(c * FF_CHUNK, (c + 1) * FF_CHUNK)
        gate = _dot(h, wg_ref[:, sl])
        up = _dot(h, wu_ref[:, sl])
        act_ref[:, sl] = (_silu(gate) * up).astype(BF16)
    y = x + _dot(act_ref[...], wd_ref[...])
    if final_norm:
        y = _rmsnorm(y, fw_ref[...])
    o_ref[...] = y


def _mix_ffn(x, a, w_out, out_layer, nw, wg, wu, wd, layer, final_w, final_norm):
    t, d = x.shape
    k = a.shape[1]
    tm = FFN_ROW_TILE if t % FFN_ROW_TILE == 0 else t
    return pl.pallas_call(
        functools.partial(_ffn_kernel, final_norm=final_norm),
        grid=(t // tm,),
        in_specs=[pl.BlockSpec((tm, d), lambda i: (i, 0)),
                  pl.BlockSpec((tm, k), lambda i: (i, 0)),
                  _resident_layer((k, d), out_layer),
                  _resident((1, d)), _resident_layer((d, D_FF), layer),
                  _resident_layer((d, D_FF), layer), _resident_layer((D_FF, d), layer),
                  _resident((1, d))],
        out_specs=pl.BlockSpec((tm, d), lambda i: (i, 0)),
        out_shape=jax.ShapeDtypeStruct((t, d), F32),
        scratch_shapes=[pltpu.VMEM((tm, D_FF), BF16)],
        compiler_params=_params(1),
        name="mix_ffn",
    )(x, a, w_out, nw.reshape(1, d), wg, wu, wd, final_w.reshape(1, d))


def _rotary(x, cos, sin_signed):
    lane = lax.broadcasted_iota(jnp.int32, x.shape, 1)
    partner = jnp.where((lane & 1) == 0, pltpu.roll(x, LANES - 1, 1), pltpu.roll(x, 1, 1))
    return x * cos + partner * sin_signed


def _l2norm(x):
    return x * lax.rsqrt(jnp.sum(x * x, axis=-1, keepdims=True) + EPS)


def _pack3(x):
    lane = lax.broadcasted_iota(jnp.int32, x.shape, 1)
    x = jnp.where(lane < SSM_HEADS, x, 0.0)
    hi = x.astype(BF16).astype(F32)
    r1 = x - hi
    mid = r1.astype(BF16).astype(F32)
    lo = r1 - mid
    return (hi + pltpu.roll(mid, SSM_HEADS, 1) + pltpu.roll(lo, 2 * SSM_HEADS, 1)).astype(BF16)


def _unpack3(y):
    return y + pltpu.roll(y, LANES - SSM_HEADS, 1) + pltpu.roll(y, LANES - 2 * SSM_HEADS, 1)


def _expand_heads(x, expand_mat):
    return _dot(_pack3(x), expand_mat)


def _head_expand_matrix():
    row = lax.broadcasted_iota(jnp.int32, (LANES, SSM_DINNER), 0)
    col = lax.broadcasted_iota(jnp.int32, (LANES, SSM_DINNER), 1)
    hit = ((row & (SSM_HEADS - 1)) == col // SSM_HEADDIM) & (row < 3 * SSM_HEADS)
    return jnp.where(hit, 1.0, 0.0).astype(BF16)


def _chunk_sum_matrices(n):
    ii = lax.broadcasted_iota(jnp.int32, (n, n), 0)
    jj = lax.broadcasted_iota(jnp.int32, (n, n), 1)
    same = ii // CHUNK == jj // CHUNK
    prefix = jnp.where(same & (ii >= jj), 1.0, 0.0).astype(F32)
    total = jnp.where(same, 1.0, 0.0).astype(F32)
    return prefix, total


def _columns(rows):
    pad = jnp.zeros((LANES - rows.shape[0], LANES), F32)
    return jnp.concatenate([rows, pad], axis=0).T


def _unit_lower_inverses(lows, interleaved=()):
    interleaved = list(interleaved)
    n = lows[0].shape[0]
    ii = lax.broadcasted_iota(jnp.int32, (n, n), 0)
    jj = lax.broadcasted_iota(jnp.int32, (n, n), 1)
    eye = (ii == jj).astype(F32)
    pair = (ii >> 1) == (jj >> 1)
    ts = [eye - jnp.where(pair, low, 0.0) for low in lows]
    lows_b = [low.astype(BF16) for low in lows]
    for k in range(1, 6):
        size = 1 << k
        join = ((ii >> (k + 1)) == (jj >> (k + 1))) & ((ii >> k) != (jj >> k))
        join_b = join.astype(F32).astype(BF16)
        ts_b = [t.astype(BF16) for t in ts]
        if size < SUBLANES:
            txs = [_dot(t_b, low_b * join_b).astype(BF16) for t_b, low_b in zip(ts_b, lows_b)]
            ts = [t - _dot(tx, t_b) for t, tx, t_b in zip(ts, txs, ts_b)]
        else:
            lower = [slice(b0 + size, b0 + 2 * size) for b0 in range(0, n, 2 * size)]
            upper = [slice(b0, b0 + size) for b0 in range(0, n, 2 * size)]
            t_low = [jnp.concatenate([t[r] for r in lower], axis=0) for t in ts]
            txs = [_dot(tl.astype(BF16), low_b * join_b).astype(BF16)
                   for tl, low_b in zip(t_low, lows_b)]
            new_lower = [tl - _dot(tx, t_b) for tl, tx, t_b in zip(t_low, txs, ts_b)]
            ts = [jnp.concatenate(
                      [piece for i, r in enumerate(upper)
                       for piece in (t[r], nl[i * size:(i + 1) * size])], axis=0)
                  for t, nl in zip(ts, new_lower)]
        if interleaved:
            interleaved.pop(0)()
        yield
    for fn in interleaved:
        fn()
    return [t - eye for t in ts]


def _hyb_core_kernel(pm_ref, pt_ref, alog_ref, dtb_ref, gnw_ref, gnn_ref,
                     mix_ref, rs_ref, gs_ref, sret, sgdn, intra_scr, *, n_tiles):
    t = pl.program_id(1)
    tl = SEQ_TILE
    hd = HEAD_DIM

    @pl.when(t == 0)
    def _():
        sret[...] = jnp.zeros_like(sret)
        sgdn[...] = jnp.zeros_like(sgdn)

    @pl.when((t == 0) & (pl.program_id(0) == 0))
    def _():
        ii = lax.broadcasted_iota(jnp.int32, (tl, tl), 0)
        jj = lax.broadcasted_iota(jnp.int32, (tl, tl), 1)
        diff = (ii - jj).astype(F32)
        for h in range(N_HEADS):
            intra_scr[h] = jnp.where(diff >= 0.0, jnp.exp(LOG_GAMMA[h] * jnp.maximum(diff, 0.0)), 0.0)

    idx = lax.broadcasted_iota(jnp.int32, (tl, 1), 0).astype(F32)

    def retention_head(h):
        lg = LOG_GAMMA[h]
        q = pm_ref[:, h * hd:(h + 1) * hd]
        k = pm_ref[:, _head_cols(RET_K, h)]
        v = pm_ref[:, _head_cols(RET_V, h)]
        gate_act = pm_ref[:, _head_cols(RET_GATE, h)]
        q_dec = jnp.exp(lg * (idx + 1.0))
        k_dec = jnp.exp(lg * (tl - 1.0 - idx))
        s = sret[h]
        qb = q.astype(BF16)
        vb = v.astype(BF16)
        scores = _dot_nt(qb, k.astype(BF16)) * intra_scr[h]
        o = _dot(scores.astype(BF16), vb) + _dot(qb, s.astype(BF16)) * q_dec
        sret[h] = s * math.exp(lg * tl) + _dot_tn((k * k_dec).astype(BF16), vb)
        mu = jnp.mean(o, axis=-1, keepdims=True)
        var = jnp.mean(jnp.square(o - mu), axis=-1, keepdims=True)
        ro = (o - mu) * lax.rsqrt(var + EPS) * gnw_ref[:, h * hd:(h + 1) * hd]
        mix_ref[:, h * hd:(h + 1) * hd] = (gate_act * ro).astype(BF16)

    tail = pt_ref[...]
    g_all = -jnp.exp(alog_ref[...]) * _softplus(tail + dtb_ref[...])
    beta_all = _sigmoid(tail)
    prefix, total = _chunk_sum_matrices(tl)
    g_parts = _pack3(g_all)
    gc_all = _unpack3(_dot(prefix.astype(BF16), g_parts))
    gl_all = _unpack3(_dot(total.astype(BF16), g_parts))
    yield

    n = N_HEADS * CHUNK
    si = lax.broadcasted_iota(jnp.int32, (n, n), 0)
    sj = lax.broadcasted_iota(jnp.int32, (n, n), 1)
    same_head = si // CHUNK == sj // CHUNK
    causal = same_head & (si >= sj)
    strict = same_head & (si > sj)

    def stack(fn):
        return jnp.concatenate([fn(h) for h in range(N_HEADS)], axis=0)

    chunks = []
    for c in range(tl // CHUNK):
        r = slice(c * CHUNK, (c + 1) * CHUNK)
        q_st = stack(lambda h: pm_ref[r, _head_cols(GDN_Q, h)])
        k_st = stack(lambda h: pm_ref[r, _head_cols(GDN_K, h)])
        v_st = stack(lambda h: pm_ref[r, _head_cols(GDN_V, h)])
        beta = stack(lambda h: beta_all[r, 4 + h:5 + h])
        gc = stack(lambda h: gc_all[r, h:h + 1])
        gl = stack(lambda h: gl_all[r, h:h + 1])
        gc_col = jnp.broadcast_to(gc, (n, n))
        seg = gc_col - gc_col.T
        decay = jnp.where(causal, jnp.exp(jnp.where(causal, seg, 0.0)), 0.0)
        k_b16 = k_st.astype(BF16)
        qk = _dot_nt(jnp.concatenate([q_st.astype(BF16), k_b16], axis=0), k_b16)
        egc = jnp.exp(gc)
        chunks.append(dict(
            r=r, gl=gl,
            low=jnp.where(strict, qk[n:] * beta * decay, 0.0),
            rhs=jnp.concatenate([v_st * beta, k_st * (beta * egc)], axis=1),
            scores=(qk[:n] * decay).astype(BF16),
            q_in=(q_st * egc).astype(BF16),
            k_out=(k_st * jnp.exp(gl - gc)).astype(BF16)))
        yield
    t_offs = yield from _unit_lower_inverses(
        [ch["low"] for ch in chunks],
        interleaved=[functools.partial(retention_head, h) for h in range(N_HEADS)])
    sols = [ch["rhs"] + _dot(t_off.astype(BF16), ch["rhs"].astype(BF16))
            for ch, t_off in zip(chunks, t_offs)]
    yield

    for ch, sol in zip(chunks, sols):
        r = ch["r"]
        w_b16 = sol[:, hd:].astype(BF16)
        v_new = []
        o_inter = []
        for h in range(N_HEADS):
            hr = slice(h * CHUNK, (h + 1) * CHUNK)
            lhs = jnp.concatenate([w_b16[hr], ch["q_in"][hr]], axis=0)
            prod = _dot(lhs, sgdn[h].astype(BF16))
            v_new.append(sol[hr, :hd] - prod[:CHUNK])
            o_inter.append(prod[CHUNK:])
        yield
        vn_b16 = jnp.concatenate(v_new, axis=0).astype(BF16)
        o_st = jnp.concatenate(o_inter, axis=0) + _dot(ch["scores"], vn_b16)
        yield
        for h in range(N_HEADS):
            hr = slice(h * CHUNK, (h + 1) * CHUNK)
            s_dec = jnp.exp(jnp.broadcast_to(ch["gl"][hr], (CHUNK, hd)))
            s_dec = jnp.concatenate([s_dec, s_dec], axis=0)
            sgdn[h] = sgdn[h] * s_dec + _dot_tn(ch["k_out"][hr], vn_b16[hr])
            o_h = o_st[hr]
            go = o_h * lax.rsqrt(jnp.mean(o_h * o_h, axis=-1, keepdims=True) + EPS) * gnn_ref[...]
            gz_act = pm_ref[r, _head_cols(GDN_GATE, h)]
            mix_ref[r, _head_cols(HEADS_W, h)] = (go * gz_act).astype(BF16)
        yield

    @pl.when(t == n_tiles - 1)
    def _():
        rs_ref[...] = sret[...]
        gs_ref[...] = sgdn[...]


def _hyb_core_prompt(pm, pt, alog, dtb, gnw, gnn, batch, seq):
    n_tiles = seq // SEQ_TILE
    ns = SEQS_PER_STEP
    tile = lambda b, t: (b, t, 0)
    state = pl.BlockSpec((ns, N_HEADS, HEAD_DIM, HEAD_DIM), lambda b, t: (b, 0, 0, 0))
    per_seq = lambda shape: pltpu.VMEM((ns,) + shape, F32)
    mixed, rs, gs = pl.pallas_call(
        functools.partial(_per_sequence(_hyb_core_kernel, ns, 2, 4), n_tiles=n_tiles),
        grid=(batch // ns, n_tiles),
        in_specs=[pl.BlockSpec((ns, SEQ_TILE, HYB_MAIN), tile),
                  pl.BlockSpec((ns, SEQ_TILE, LANES), tile),
                  _resident((1, LANES)), _resident((1, LANES)),
                  _resident((1, HEADS_W)), _resident((1, LANES))],
        out_specs=[pl.BlockSpec((ns, SEQ_TILE, D_MODEL), tile), state, state],
        out_shape=[jax.ShapeDtypeStruct((batch, seq, D_MODEL), BF16),
                   jax.ShapeDtypeStruct((batch, N_HEADS, HEAD_DIM, HEAD_DIM), F32),
                   jax.ShapeDtypeStruct((batch, N_HEADS, HEAD_DIM, HEAD_DIM), F32)],
        scratch_shapes=[per_seq((N_HEADS, HEAD_DIM, HEAD_DIM)),
                        per_seq((N_HEADS, HEAD_DIM, HEAD_DIM)),
                        per_seq((N_HEADS, SEQ_TILE, SEQ_TILE))],
        compiler_params=_params(2),
        name="hyb_core_prompt",
    )(pm.reshape(batch, seq, HYB_MAIN), pt.reshape(batch, seq, LANES), alog, dtb, gnw, gnn)
    return mixed.reshape(batch * seq, D_MODEL), rs, gs


def _group_rmsnorm_gate(y, z, nw_ref, o_ref, rows):
    for g in range(SSM_GROUPS):
        sl = slice(g * SSM_GROUP_W, (g + 1) * SSM_GROUP_W)
        yg = y[:, sl] * _silu(z[:, sl])
        yg = yg * lax.rsqrt(jnp.mean(yg * yg, axis=-1, keepdims=True) + EPS)
        o_ref[rows, sl] = (yg * nw_ref[:, sl]).astype(o_ref.dtype)


def _ssm_core_kernel(pm_ref, pt_ref, alog_ref, dtb_ref, dskip_ref, nw_ref,
                     y_ref, st_ref, s_t, xdt, xdec, eacs, acs_e, y_scr, *, n_tiles):
    t = pl.program_id(1)
    tl = SEQ_TILE
    xbc = pm_ref.at[:, SSM_DINNER:SSM_MAIN]

    @pl.when(t == 0)
    def _():
        s_t[...] = jnp.zeros_like(s_t)

    n_chunks = tl // CHUNK
    expand = _head_expand_matrix()
    dt = _softplus(pt_ref[...] + dtb_ref[...])
    a = dt * (-jnp.exp(alog_ref[...]))
    prefix, total = _chunk_sum_matrices(tl)
    a_parts = _pack3(a)
    acs = _unpack3(_dot(prefix.astype(BF16), a_parts))
    atot = _unpack3(_dot(total.astype(BF16), a_parts))
    dt_parts = _pack3(dt)
    acs_parts = _pack3(acs)
    chunk_rows = [atot[c * CHUNK:c * CHUNK + 1] for c in range(n_chunks)]
    chunk_rows.append(jnp.zeros((SUBLANES - n_chunks, LANES), F32))
    atot_parts = _pack3(jnp.concatenate(chunk_rows, axis=0))

    def expand_group(parts, g):
        return _dot(parts, expand[:, g * SSM_GROUP_W:(g + 1) * SSM_GROUP_W])

    yield
    chunk_dec = []
    for g in range(SSM_GROUPS):
        gs = slice(g * SSM_GROUP_W, (g + 1) * SSM_GROUP_W)
        x_dt = xbc[:, gs] * expand_group(dt_parts, g)
        xdt[:, gs] = x_dt
        yield
        acs_g = expand_group(acs_parts, g)
        acs_e[:, gs] = acs_g
        eacs[:, gs] = jnp.exp(acs_g)
        yield
        atot_g = expand_group(atot_parts, g)
        chunk_dec.append(jnp.exp(atot_g))
        for c in range(n_chunks):
            r = slice(c * CHUNK, (c + 1) * CHUNK)
            to_end = jnp.exp(atot_g[c:c + 1, :] - acs_g[r])
            xdec[r, gs] = (x_dt[r] * to_end).astype(BF16)
        yield

    lane = lax.broadcasted_iota(jnp.int32, (CHUNK, LANES), 1)
    row_i = lax.broadcasted_iota(jnp.int32, (CHUNK, LANES), 0)
    first_half = lane < SSM_HEADDIM
    causal2 = row_i >= (lane & (CHUNK - 1))
    lane2 = lax.broadcasted_iota(jnp.int32, (2 * CHUNK, LANES), 1)
    row2 = lax.broadcasted_iota(jnp.int32, (2 * CHUNK, LANES), 0)
    pair_diag = (row2 < CHUNK) == (lane2 < SSM_HEADDIM)

    def bc_cc(r, g):
        bc = xbc[r, SSM_DINNER + g * LANES:SSM_DINNER + (g + 1) * LANES]
        cc = xbc[r, SSM_DINNER + 512 + g * LANES:SSM_DINNER + 512 + (g + 1) * LANES]
        return bc.astype(BF16), cc.astype(BF16)

    for c in range(n_chunks):
        r = slice(c * CHUNK, (c + 1) * CHUNK)
        acs_c = acs[r]
        acs_t = jnp.concatenate([acs_c, acs_c], axis=0).T
        cb2 = []
        for g in range(SSM_GROUPS):
            bc_b16, cc_b16 = bc_cc(r, g)
            cb2.append(_dot_nt(cc_b16, jnp.concatenate([bc_b16, bc_b16], axis=0)))
        yield
        for p in range(SSM_HEADS // 2):
            ha = 2 * p
            rowv = jnp.where(first_half, acs_t[ha:ha + 1, :], acs_t[ha + 1:ha + 2, :])
            seg = acs_e[r, p * LANES:(p + 1) * LANES] - rowv
            lmat = jnp.where(causal2, jnp.exp(jnp.where(causal2, seg, 0.0)), 0.0)
            m = (cb2[p // 4] * lmat).astype(BF16)
            xblk = xdt[r, p * LANES:(p + 1) * LANES]
            xblk2 = jnp.concatenate([xblk, xblk], axis=0)
            x_bd = jnp.where(pair_diag, xblk2, 0.0).astype(BF16)
            y_scr[r, p * LANES:(p + 1) * LANES] = _dot(m, x_bd)
            yield

    for c in range(n_chunks):
        r = slice(c * CHUNK, (c + 1) * CHUNK)
        groups = range(SSM_GROUPS)
        gsl = [slice(g * SSM_GROUP_W, (g + 1) * SSM_GROUP_W) for g in groups]
        bcs, ccs = zip(*[bc_cc(r, g) for g in groups])
        s_old = [s_t[:, gsl[g]] for g in groups]
        y_inter = [_dot(ccs[g], s_old[g].astype(BF16)) for g in groups]
        yield
        s_add = [_dot_tn(bcs[g], xdec[r, gsl[g]]) for g in groups]
        yield
        for g in groups:
            s_t[:, gsl[g]] = s_old[g] * chunk_dec[g][c:c + 1, :] + s_add[g]
            y_scr[r, gsl[g]] = y_scr[r, gsl[g]] + y_inter[g] * eacs[r, gsl[g]]
        yield

    for g in range(SSM_GROUPS):
        gs = slice(g * SSM_GROUP_W, (g + 1) * SSM_GROUP_W)
        yg = (y_scr[:, gs] + xbc[:, gs] * dskip_ref[:, gs]) * pm_ref[:, gs]
        yg = yg * lax.rsqrt(jnp.mean(yg * yg, axis=-1, keepdims=True) + EPS)
        y_ref[:, gs] = (yg * nw_ref[:, gs]).astype(y_ref.dtype)
        yield

    @pl.when(t == n_tiles - 1)
    def _():
        for kblk in range(SSM_DINNER // LANES):
            ks = slice(kblk * LANES, (kblk + 1) * LANES)
            st_ref[ks, :] = s_t[:, ks].T


def _per_sequence(body, n_seq, n_tiled_in, n_shared_in):
    def kernel(*refs, **kw):
        tiled = refs[:n_tiled_in]
        shared = refs[n_tiled_in:n_tiled_in + n_shared_in]
        rest = refs[n_tiled_in + n_shared_in:]
        running = [body(*[r.at[s] for r in tiled], *shared, *[r.at[s] for r in rest], **kw)
                   for s in range(n_seq)]
        while running:
            for gen in list(running):
                if next(gen, StopIteration) is StopIteration:
                    running.remove(gen)
    return kernel


def _ssm_core_prompt(pm, pt, alog, dtb, dskip, nw, batch, seq):
    n_tiles = seq // SEQ_TILE
    ns = SEQS_PER_STEP
    tile = lambda b, t: (b, t, 0)
    per_seq = lambda shape, dtype: pltpu.VMEM((ns,) + shape, dtype)
    y, st = pl.pallas_call(
        functools.partial(_per_sequence(_ssm_core_kernel, ns, 2, 4), n_tiles=n_tiles),
        grid=(batch // ns, n_tiles),
        in_specs=[pl.BlockSpec((ns, SEQ_TILE, SSM_MAIN), tile),
                  pl.BlockSpec((ns, SEQ_TILE, LANES), tile),
                  _resident((1, LANES)), _resident((1, LANES)),
                  _resident((1, SSM_DINNER)), _resident((1, SSM_DINNER))],
        out_specs=[pl.BlockSpec((ns, SEQ_TILE, SSM_DINNER), tile),
                   pl.BlockSpec((ns, SSM_DINNER, SSM_DSTATE), lambda b, t: (b, 0, 0))],
        out_shape=[jax.ShapeDtypeStruct((batch, seq, SSM_DINNER), BF16),
                   jax.ShapeDtypeStruct((batch, SSM_DINNER, SSM_DSTATE), F32)],
        scratch_shapes=[per_seq((SSM_DSTATE, SSM_DINNER), F32),
                        per_seq((SEQ_TILE, SSM_DINNER), F32),
                        per_seq((SEQ_TILE, SSM_DINNER), BF16),
                        per_seq((SEQ_TILE, SSM_DINNER), F32),
                        per_seq((SEQ_TILE, SSM_DINNER), F32),
                        per_seq((SEQ_TILE, SSM_DINNER), F32)],
        compiler_params=_params(2),
        name="ssm_core_prompt",
    )(pm.reshape(batch, seq, SSM_MAIN), pt.reshape(batch, seq, LANES), alog, dtb, dskip, nw)
    return y.reshape(batch * seq, SSM_DINNER), st


def _hyb_step_front_kernel(pm_ref, conv_ref, cos_ref, sin_ref, cw_ref, o_ref, conv_out):
    hd = HEAD_DIM
    ch = GDN_CONV_CH
    cos = cos_ref[...]
    sin = sin_ref[...]
    for h in range(N_HEADS):
        qs, ks = _head_cols(RET_Q, h), _head_cols(RET_K, h)
        o_ref[:, qs] = _rotary(pm_ref[:, qs], cos, sin)
        o_ref[:, ks] = _rotary(pm_ref[:, ks], cos, sin) * (hd ** -0.5)
    o_ref[:, RET_V:RET_GATE] = pm_ref[:, RET_V:RET_GATE]
    o_ref[:, RET_GATE:GDN_Q] = _silu(pm_ref[:, RET_GATE:GDN_Q])
    o_ref[:, GDN_GATE:HYB_MAIN] = _silu(pm_ref[:, GDN_GATE:HYB_MAIN])
    for blk in range(ch // LANES):
        cs = slice(blk * LANES, (blk + 1) * LANES)
        u = pm_ref[:, GDN_Q + blk * LANES:GDN_Q + (blk + 1) * LANES]
        c0 = conv_ref[:, blk * LANES:(blk + 1) * LANES]
        c1 = conv_ref[:, ch + blk * LANES:ch + (blk + 1) * LANES]
        c2 = conv_ref[:, 2 * ch + blk * LANES:2 * ch + (blk + 1) * LANES]
        conv = c0 * cw_ref[0:1, cs]
        conv = conv + c1 * cw_ref[1:2, cs]
        conv = conv + c2 * cw_ref[2:3, cs]
        conv = conv + u * cw_ref[3:4, cs]
        act = _silu(conv)
        if blk < N_HEADS:
            act = _l2norm(act) * (hd ** -0.5)
        elif blk < 2 * N_HEADS:
            act = _l2norm(act)
        o_ref[:, GDN_Q + blk * LANES:GDN_Q + (blk + 1) * LANES] = act
        conv_out[:, blk * LANES:(blk + 1) * LANES] = c1
        conv_out[:, ch + blk * LANES:ch + (blk + 1) * LANES] = c2
        conv_out[:, 2 * ch + blk * LANES:2 * ch + (blk + 1) * LANES] = u


def _hyb_step_front(pm, conv_state, cos, sin, cw_t):
    t = pm.shape[0]
    full = lambda a: pl.BlockSpec(a.shape, lambda i: (0,) * a.ndim)
    return pl.pallas_call(
        _hyb_step_front_kernel,
        grid=(1,),
        in_specs=[full(pm), full(conv_state), full(cos), full(sin), full(cw_t)],
        out_specs=[full(pm), full(conv_state)],
        out_shape=[jax.ShapeDtypeStruct(pm.shape, F32), jax.ShapeDtypeStruct(conv_state.shape, F32)],
        compiler_params=_params(1),
        name="hyb_step_front",
    )(pm, conv_state, cos, sin, cw_t)


def _hyb_step_kernel(pm_ref, pt_ref, rs_in, gs_in, alog_ref, dtb_ref, gnw_ref, gnn_ref,
                     mix_ref, rs_out, gs_out, o_scr):
    bb = STEP_BATCH
    hd = HEAD_DIM
    tail = pt_ref[...]
    g_all = -jnp.exp(alog_ref[...]) * _softplus(tail + dtb_ref[...])
    beta_all = _sigmoid(tail)

    for h in range(N_HEADS):
        q = pm_ref[:, h * hd:(h + 1) * hd]
        k = pm_ref[:, _head_cols(RET_K, h)]
        v = pm_ref[:, _head_cols(RET_V, h)]
        k_cols = _columns(k)
        gamma = math.exp(LOG_GAMMA[h])
        for b in range(bb):
            s = rs_in[b, h] * gamma + k_cols[:, b:b + 1] * v[b:b + 1, :]
            rs_out[b, h] = s
            o_scr[b:b + 1, h * hd:(h + 1) * hd] = _dot(q, s)[b:b + 1, :]
        gq = pm_ref[:, _head_cols(GDN_Q, h)]
        gk = pm_ref[:, _head_cols(GDN_K, h)]
        gv = pm_ref[:, _head_cols(GDN_V, h)]
        k_cols = _columns(gk)
        decay = jnp.broadcast_to(jnp.exp(g_all[:, h:h + 1]), (bb, hd))
        beta = jnp.broadcast_to(beta_all[:, 4 + h:5 + h], (bb, hd))
        for b in range(bb):
            s = gs_in[b, h] * decay[b:b + 1, :]
            delta = (gv[b:b + 1, :] - _dot(gk, s)[b:b + 1, :]) * beta[b:b + 1, :]
            s = s + k_cols[:, b:b + 1] * delta
            gs_out[b, h] = s
            o_scr[b:b + 1, _head_cols(HEADS_W, h)] = _dot(gq, s)[b:b + 1, :]

    for h in range(N_HEADS):
        o = o_scr[:, h * hd:(h + 1) * hd]
        mu = jnp.mean(o, axis=-1, keepdims=True)
        var = jnp.mean(jnp.square(o - mu), axis=-1, keepdims=True)
        ro = (o - mu) * lax.rsqrt(var + EPS) * gnw_ref[:, h * hd:(h + 1) * hd]
        mix_ref[:, h * hd:(h + 1) * hd] = pm_ref[:, _head_cols(RET_GATE, h)] * ro
        go = o_scr[:, _head_cols(HEADS_W, h)]
        go = go * lax.rsqrt(jnp.mean(go * go, axis=-1, keepdims=True) + EPS) * gnn_ref[...]
        mix_ref[:, _head_cols(HEADS_W, h)] = go * pm_ref[:, _head_cols(GDN_GATE, h)]


def _layer_chained_outputs(state_arrays, layer, partial):
    shapes = [jax.ShapeDtypeStruct(s.shape, s.dtype) for s in state_arrays]
    if layer == 0:
        return [], [], shapes
    return list(partial), [pl.BlockSpec(memory_space=pl.ANY)] * len(partial), shapes


def _hyb_core_step(pm_act, pt, ret_all, gdn_all, layer, partial, alog, dtb, gnw, gnn):
    batch = pm_act.shape[0]
    bb = STEP_BATCH
    row = lambda i: (i, 0)
    state = pl.BlockSpec((None, bb, N_HEADS, HEAD_DIM, HEAD_DIM), lambda i: (layer, i, 0, 0, 0))
    extra, extra_specs, state_shapes = _layer_chained_outputs((ret_all, gdn_all), layer, partial)
    n_in = 8
    aliases = {n_in + k: 1 + k for k in range(len(extra))}

    def body(*refs):
        _hyb_step_kernel(*refs[:n_in], *refs[n_in + len(extra):])

    return pl.pallas_call(
        body,
        grid=(batch // bb,),
        in_specs=[pl.BlockSpec((bb, HYB_MAIN), row), pl.BlockSpec((bb, LANES), row), state, state,
                  _resident((1, LANES)), _resident((1, LANES)),
                  _resident((1, HEADS_W)), _resident((1, LANES))] + extra_specs,
        out_specs=[pl.BlockSpec((bb, D_MODEL), row), state, state],
        out_shape=[jax.ShapeDtypeStruct((batch, D_MODEL), F32)] + state_shapes,
        scratch_shapes=[pltpu.VMEM((bb, D_MODEL), F32)],
        input_output_aliases=aliases,
        compiler_params=_params(1),
        name="hyb_core_step",
    )(pm_act, pt, ret_all, gdn_all, alog, dtb, gnw, gnn, *extra)


def _ssm_step_kernel(pm_ref, pt_ref, conv_ref, s_in, cw_ref, cbias_ref, alog_ref, dtb_ref,
                     dskip_ref, nw_ref, y_ref, conv_out, s_out, y_scr):
    bb = STEP_BATCH
    ch = SSM_CONV_CH
    u = pm_ref[:, SSM_DINNER:SSM_MAIN]
    c0 = conv_ref[:, 0:ch]
    c1 = conv_ref[:, ch:2 * ch]
    c2 = conv_ref[:, 2 * ch:3 * ch]
    conv = c0 * cw_ref[0:1, :]
    conv = conv + c1 * cw_ref[1:2, :]
    conv = conv + c2 * cw_ref[2:3, :]
    conv = conv + u * cw_ref[3:4, :]
    xbc = _silu(conv + cbias_ref[...])
    conv_out[:, 0:ch] = c1
    conv_out[:, ch:2 * ch] = c2
    conv_out[:, 2 * ch:3 * ch] = u
    xs = xbc[:, 0:SSM_DINNER]
    expand = _head_expand_matrix()
    dt = _softplus(pt_ref[...] + dtb_ref[...])
    a = dt * (-jnp.exp(alog_ref[...]))
    x_dt = xs * _expand_heads(dt, expand)
    decay = jnp.exp(a)
    half = SSM_HEADDIM
    for kblk in range(SSM_DINNER // LANES):
        ks = slice(kblk * LANES, (kblk + 1) * LANES)
        g = kblk // 4
        x_cols = _columns(x_dt[:, ks])
        bm = xbc[:, SSM_DINNER + g * LANES:SSM_DINNER + (g + 1) * LANES]
        cm = xbc[:, SSM_DINNER + 512 + g * LANES:SSM_DINNER + 512 + (g + 1) * LANES]
        for b in range(bb):
            d0 = decay[b:b + 1, 2 * kblk:2 * kblk + 1]
            d1 = decay[b:b + 1, 2 * kblk + 1:2 * kblk + 2]
            s_dec = jnp.concatenate([s_in[b, kblk * LANES:kblk * LANES + half, :] * d0,
                                     s_in[b, kblk * LANES + half:(kblk + 1) * LANES, :] * d1], axis=0)
            s = s_dec + x_cols[:, b:b + 1] * bm[b:b + 1, :]
            s_out[b, ks, :] = s
            y_scr[b:b + 1, ks] = _dot_nt(cm, s)[b:b + 1, :]
    y = y_scr[...] + xs * dskip_ref[...]
    _group_rmsnorm_gate(y, pm_ref[:, 0:SSM_DINNER], nw_ref, y_ref, slice(None))


def _ssm_core_step(pm, pt, conv_state, ssm_all, layer, partial, cw_t, cbias, alog, dtb, dskip, nw):
    batch = pm.shape[0]
    bb = STEP_BATCH
    row = lambda i: (i, 0)
    state = pl.BlockSpec((None, bb, SSM_DINNER, SSM_DSTATE), lambda i: (layer, i, 0, 0))
    extra, extra_specs, state_shapes = _layer_chained_outputs((ssm_all,), layer, partial)
    n_in = 10
    aliases = {n_in + k: 2 + k for k in range(len(extra))}

    def body(*refs):
        _ssm_step_kernel(*refs[:n_in], *refs[n_in + len(extra):])

    return pl.pallas_call(
        body,
        grid=(batch // bb,),
        in_specs=[pl.BlockSpec((bb, SSM_MAIN), row), pl.BlockSpec((bb, LANES), row),
                  pl.BlockSpec((bb, 3 * SSM_CONV_CH), row), state,
                  _resident((CONV_W, SSM_CONV_CH)), _resident((1, SSM_CONV_CH)),
                  _resident((1, LANES)), _resident((1, LANES)),
                  _resident((1, SSM_DINNER)), _resident((1, SSM_DINNER))] + extra_specs,
        out_specs=[pl.BlockSpec((bb, SSM_DINNER), row), pl.BlockSpec((bb, 3 * SSM_CONV_CH), row),
                   state],
        out_shape=[jax.ShapeDtypeStruct((batch, SSM_DINNER), F32),
                   jax.ShapeDtypeStruct((batch, 3 * SSM_CONV_CH), F32)] + state_shapes,
        scratch_shapes=[pltpu.VMEM((bb, SSM_DINNER), F32)],
        input_output_aliases=aliases,
        compiler_params=_params(1),
        name="ssm_core_step",
    )(pm, pt, conv_state, ssm_all, cw_t, cbias, alog, dtb, dskip, nw, *extra)


def _rope_tables(pos):
    inv_freq = 1.0 / (ROPE_BASE ** jnp.linspace(0.0, 1.0, HEAD_DIM // 2, dtype=F32))
    ang = pos.astype(F32)[:, None] * inv_freq[None, :]
    cos = jnp.repeat(jnp.cos(ang), 2, axis=-1)
    sign = jnp.tile(jnp.array([-1.0, 1.0], F32), HEAD_DIM // 2)
    sin = jnp.repeat(jnp.sin(ang), 2, axis=-1) * sign
    return cos, sin


def _lane_row(v):
    return jnp.pad(v.astype(F32), (0, LANES - v.shape[0])).reshape(1, LANES)


def kernel(x_prompt, x_sample, state_ret, state_gdn, state_gdn_conv, state_ssm, state_ssm_conv,
           norm_mix_w, norm_ffn_w, norm_final_w, hyb_w_in, hyb_w_out, ret_gn_w, gdn_conv_w,
           gdn_a_log, gdn_dt_bias, gdn_norm_w, ssm_w_in, ssm_conv_w, ssm_conv_b, ssm_dt_bias,
           ssm_a_log, ssm_d, ssm_norm_w, ssm_w_out, ffn_w_gate, ffn_w_up, ffn_w_down):
    depth = norm_mix_w.shape[0]
    bp, seq, d = x_prompt.shape
    bs = x_sample.shape[0]
    past_len = 16384

    hyb_in = hyb_w_in.astype(BF16)
    hyb_tail = jnp.pad(hyb_w_in[:, :, HYB_MAIN:], ((0, 0), (0, 0), (0, LANES - 2 * N_HEADS))).astype(BF16)
    ssm_in = ssm_w_in.astype(BF16)
    ssm_tail = jnp.pad(ssm_w_in[:, :, SSM_MAIN:], ((0, 0), (0, 0), (0, LANES - SSM_HEADS))).astype(BF16)
    hyb_out = hyb_w_out.astype(BF16)
    ssm_out = ssm_w_out.astype(BF16)
    w_gate = ffn_w_gate.astype(BF16)
    w_up = ffn_w_up.astype(BF16)
    w_down = ffn_w_down.astype(BF16)

    gdn_cw_t = jnp.swapaxes(gdn_conv_w, 1, 2)
    ssm_cw_t = jnp.swapaxes(ssm_conv_w, 1, 2)
    dskip = jnp.repeat(ssm_d, SSM_HEADDIM, axis=-1)
    cos_p, sin_p = _rope_tables(jnp.arange(seq))
    cos_s, sin_s = _rope_tables(jnp.arange(past_len, past_len + 1))

    ssm_state_rows = state_ssm.reshape(state_ssm.shape[0], bs, SSM_DINNER, SSM_DSTATE)

    def trunk(x, prompt):
        t = x.shape[0]
        ret_l, gdn_l, gconv_l, ssm_l, sconv_l = [], [], [], [], []
        hyb_states, ssm_states = (), ()
        for i in range(depth):
            j = i // 2
            if i % 2 == 0:
                args = (_lane_row(gdn_a_log[j]), _lane_row(gdn_dt_bias[j]),
                        ret_gn_w[j].reshape(1, HEADS_W), gdn_norm_w[j].reshape(1, LANES))
                if prompt:
                    rope_rows = min(PROJ_ROW_TILE, seq)
                    rope = pl.BlockSpec((rope_rows, LANES), lambda r: (r % (seq // rope_rows), 0))
                    pm, pt, cb = _prompt_proj(
                        _hyb_proj_kernel, x, norm_mix_w[i], hyb_in, j, HYB_MAIN, hyb_tail[j],
                        GDN_CONV_CH, seq, (cos_p, sin_p, gdn_cw_t[j]),
                        [rope, rope, _resident((CONV_W, GDN_CONV_CH))])
                    mixed, rs, gs = _hyb_core_prompt(pm, pt, *args, bp, seq)
                    cb = cb[:, SUBLANES - (CONV_W - 1):, :]
                    ret_l.append(rs)
                    gdn_l.append(gs)
                else:
                    pm, pt = _norm_proj(x, norm_mix_w[i], hyb_in, j, HYB_MAIN, hyb_tail[j])
                    pm_act, cb = _hyb_step_front(pm, state_gdn_conv[j].reshape(t, -1), cos_s, sin_s,
                                                 gdn_cw_t[j])
                    mixed, *hyb_states = _hyb_core_step(pm_act, pt, state_ret, state_gdn, j,
                                                        hyb_states, *args)
                    cb = cb.reshape(t, CONV_W - 1, GDN_CONV_CH)
                gconv_l.append(cb)
                w_mix = hyb_out
            else:
                args = (_lane_row(ssm_a_log[j]), _lane_row(ssm_dt_bias[j]), dskip[j].reshape(1, -1),
                        ssm_norm_w[j].reshape(1, -1))
                conv_args = (ssm_cw_t[j], ssm_conv_b[j].reshape(1, -1))
                if prompt:
                    pm, pt, cb = _prompt_proj(
                        _ssm_proj_kernel, x, norm_mix_w[i], ssm_in, j, SSM_MAIN, ssm_tail[j],
                        SSM_CONV_CH, seq, conv_args,
                        [_resident((CONV_W, SSM_CONV_CH)), _resident((1, SSM_CONV_CH))])
                    y, ss = _ssm_core_prompt(pm, pt, *args, bp, seq)
                    cb = cb[:, SUBLANES - (CONV_W - 1):, :]
                    ssm_l.append(ss.reshape(-1, SSM_HEADS, SSM_HEADDIM, SSM_DSTATE))
                else:
                    pm, pt = _norm_proj(x, norm_mix_w[i], ssm_in, j, SSM_MAIN, ssm_tail[j])
                    y, cb, *ssm_states = _ssm_core_step(
                        pm, pt, state_ssm_conv[j].reshape(t, -1), ssm_state_rows, j, ssm_states,
                        *conv_args, *args)
                    cb = cb.reshape(t, CONV_W - 1, SSM_CONV_CH)
                sconv_l.append(cb)
                mixed, w_mix = y, ssm_out
            x = _mix_ffn(x, mixed, w_mix, j, norm_ffn_w[i], w_gate, w_up, w_down, i, norm_final_w,
                         final_norm=(i == depth - 1))
        if prompt:
            ret_new, gdn_new, ssm_new = jnp.stack(ret_l), jnp.stack(gdn_l), jnp.stack(ssm_l)
        else:
            ret_new, gdn_new = hyb_states
            ssm_new = ssm_states[0].reshape(state_ssm.shape)
        return x, ret_new, gdn_new, jnp.stack(gconv_l), ssm_new, jnp.stack(sconv_l)

    yp, ret_p, gdn_p, gconv_p, ssm_p, sconv_p = trunk(x_prompt.reshape(bp * seq, d), True)
    ys, ret_s, gdn_s, gconv_s, ssm_s, sconv_s = trunk(x_sample.reshape(bs, d), False)
    return (yp.reshape(bp, seq, d), ys.reshape(bs, 1, d),
            ret_p, gdn_p, gconv_p, ssm_p, sconv_p,
            ret_s, gdn_s, gconv_s, ssm_s, sconv_s)
```
